```python
import math
import jax
import jax.numpy as jnp
from jax import lax
import numpy as np

D_MODEL = 1024
BATCH = 16
SEQ = 4096
DEPTH = 4

HEAD_DIM = 64
GRID_W = 64
EPS = 1e-6
NEG_INF = -1e30

NA_HEADS = 4
NA_KH_MAX = 8
NA_KW = 16

DIL_HEADS = 8
DIL_PATTERNS = ((128, 1), (512, 4), (2048, 16))
DIL_BLK = 128

MLA_HEADS = 4
MLA_Q_LORA = 384
MLA_KV_LORA = 256
MLA_NOPE = 64
MLA_ROPE = 32
MLA_V = 64
MLA_BLK = 128
ROPE_THETA = 10000.0

T5_BUCKETS = 32
T5_MAX_DIST = 1024

D_FF = 2816
CONV_W = 3

D_A = NA_HEADS * HEAD_DIM
D_B = DIL_HEADS * HEAD_DIM
D_C = MLA_HEADS * MLA_V
D_MIX = D_A + D_B + D_C
D_IN = 3 * D_A + 3 * D_B + MLA_Q_LORA + MLA_KV_LORA + MLA_ROPE
N_MOD = 6

kernel_name = "hybrid_na_dilated_mla_encoder"


def rms_norm(x, g):
    xf = x.astype(jnp.float32)
    y = xf * lax.rsqrt(jnp.mean(xf * xf, axis=-1, keepdims=True) + EPS)
    return (y * g.astype(jnp.float32)).astype(x.dtype)


def rope_tables(seq_len):
    inv_freq = jnp.asarray(ROPE_THETA ** (-np.arange(0, MLA_ROPE, 2, dtype=np.float32) / MLA_ROPE), jnp.float32)
    ang = jnp.arange(seq_len, dtype=jnp.float32)[:, None] * inv_freq[None, :]
    return jnp.cos(ang), jnp.sin(ang)


def apply_rope(x, cos, sin):
    half = x.shape[-1] // 2
    x1, x2 = x[..., :half], x[..., half:]
    c = cos[None, :, None, :].astype(x.dtype)
    s = sin[None, :, None, :].astype(x.dtype)
    return jnp.concatenate([x1 * c - x2 * s, x1 * s + x2 * c], axis=-1)


def neighbourhood_attention(q, k, v, rpb):
    B, S, H, Dh = q.shape
    rows = S // GRID_W
    kh = min(NA_KH_MAX, rows)
    qg = q.reshape(B, rows, GRID_W, H, Dh)
    kg = k.reshape(B, rows, GRID_W, H, Dh)
    vg = v.reshape(B, rows, GRID_W, H, Dh)
    col = np.arange(GRID_W)
    c_start = np.clip(col - NA_KW // 2, 0, GRID_W - NA_KW)
    c_idx = c_start[:, None] + np.arange(NA_KW)[None, :]
    d_col = c_idx - col[:, None] + (NA_KW - 1)
    row = np.arange(rows)
    r_start = np.clip(row - kh // 2, 0, rows - kh)
    d_row = r_start[:, None] + np.arange(kh)[None, :] - row[:, None] + (NA_KH_MAX - 1)
    scale = Dh ** -0.5

    def one_row(args):
        r, r0, dr = args
        q_r = lax.dynamic_index_in_dim(qg, r, axis=1, keepdims=False)
        k_r = lax.dynamic_slice_in_dim(kg, r0, kh, axis=1)[:, :, c_idx]
        v_r = lax.dynamic_slice_in_dim(vg, r0, kh, axis=1)[:, :, c_idx]
        bias = rpb[:, dr[:, None, None], d_col[None, :, :]]
        s = jnp.einsum("bwhd,biwjhd->bhwij", q_r, k_r).astype(jnp.float32) * scale
        s = s + jnp.transpose(bias, (0, 2, 1, 3)).astype(jnp.float32)[None]
        p = jax.nn.softmax(s.reshape(B, H, GRID_W, kh * NA_KW), axis=-1)
        p = p.reshape(B, H, GRID_W, kh, NA_KW).astype(v.dtype)
        return jnp.einsum("bhwij,biwjhd->bwhd", p, v_r)

    out = lax.map(one_row, (jnp.arange(rows, dtype=jnp.int32),
                            jnp.asarray(r_start, jnp.int32),
                            jnp.asarray(d_row, jnp.int32)))
    return jnp.transpose(out, (1, 0, 2, 3, 4)).reshape(B, S, H, Dh)


def t5_bucket(rel):
    nb = T5_BUCKETS // 2
    max_exact = nb // 2
    n = np.abs(rel)
    large = max_exact + (np.log(np.maximum(n, 1) / max_exact)
                         / math.log(T5_MAX_DIST / max_exact) * (nb - max_exact)).astype(np.int64)
    large = np.minimum(large, nb - 1)
    return (np.where(rel > 0, nb, 0) + np.where(n < max_exact, n, large)).astype(np.int32)


def dilated_branch(q, k, v, t5_table, window, dilation):
    B, S, H, Dh = q.shape
    d = dilation
    half = window // (2 * d)
    L = S // d
    nblk = -(-L // DIL_BLK)
    Lp = nblk * DIL_BLK
    kb_len = DIL_BLK + 2 * half

    def residue_major(t):
        return jnp.transpose(t.reshape(B, L, d, H, Dh), (0, 2, 1, 3, 4))

    qr = jnp.pad(residue_major(q), ((0, 0), (0, 0), (0, Lp - L), (0, 0), (0, 0)))
    qr = qr.reshape(B, d, nblk, DIL_BLK, H, Dh)
    kv_pad = ((0, 0), (0, 0), (half, Lp - L + half), (0, 0), (0, 0))
    k_idx = np.arange(nblk)[:, None] * DIL_BLK + np.arange(kb_len)[None, :]
    kr = jnp.take(jnp.pad(residue_major(k), kv_pad), k_idx, axis=2)
    vr = jnp.take(jnp.pad(residue_major(v), kv_pad), k_idx, axis=2)

    a = np.arange(DIL_BLK)[:, None]
    j = np.arange(kb_len)[None, :]
    rel_m = j - half - a
    m_key = np.arange(nblk)[:, None, None] * DIL_BLK + (j - half)[None]
    valid = (np.abs(rel_m) <= half)[None] & (m_key >= 0) & (m_key < L)
    bias = jnp.transpose(t5_table[t5_bucket(rel_m * d)], (2, 0, 1)).astype(jnp.float32)

    s = jnp.einsum("brnqhd,brnkhd->brnhqk", qr, kr).astype(jnp.float32) * (Dh ** -0.5)
    s = jnp.where(valid[None, None, :, None], s + bias, NEG_INF)
    lse = jax.nn.logsumexp(s, axis=-1)
    p = jnp.exp(s - lse[..., None]).astype(v.dtype)
    o = jnp.einsum("brnhqk,brnkhd->brnqhd", p, vr)
    o = o.reshape(B, d, Lp, H, Dh)[:, :, :L]
    o = jnp.transpose(o, (0, 2, 1, 3, 4)).reshape(B, S, H, Dh)
    lse = jnp.transpose(lse, (0, 1, 2, 4, 3)).reshape(B, d, Lp, H)[:, :, :L]
    lse = jnp.transpose(lse, (0, 2, 1, 3)).reshape(B, S, H)
    return o, lse


def dilated_attention(q, k, v, t5_table):
    outs, lses = [], []
    for window, dilation in DIL_PATTERNS:
        o, lse = dilated_branch(q, k, v, t5_table, window, dilation)
        outs.append(o)
        lses.append(lse)
    w = jax.nn.softmax(jnp.stack(lses, axis=0), axis=0)
    mixed = jnp.sum(w[..., None] * jnp.stack(outs, axis=0).astype(jnp.float32), axis=0)
    return mixed.astype(q.dtype)


def latent_attention(c_q, c_kv, k_rope_raw, g_q, g_kv, w_uq, w_ukv, cos, sin):
    B, S, _ = c_q.shape
    q = (rms_norm(c_q, g_q) @ w_uq).reshape(B, S, MLA_HEADS, MLA_NOPE + MLA_ROPE)
    q_nope, q_rope = q[..., :MLA_NOPE], apply_rope(q[..., MLA_NOPE:], cos, sin)
    kv = (rms_norm(c_kv, g_kv) @ w_ukv).reshape(B, S, MLA_HEADS, MLA_NOPE + MLA_V)
    k_nope, v = kv[..., :MLA_NOPE], kv[..., MLA_NOPE:]
    k_rope = apply_rope(k_rope_raw[:, :, None, :], cos, sin)[:, :, 0]
    nb = S // MLA_BLK
    scale = (MLA_NOPE + MLA_ROPE) ** -0.5

    def to_blocks(t):
        return jnp.moveaxis(t.reshape(B, nb, MLA_BLK, MLA_HEADS, t.shape[-1]), 1, 0)

    def one_block(args):
        qn, qr = args
        s = jnp.einsum("bqhd,bkhd->bhqk", qn, k_nope) + jnp.einsum("bqhd,bkd->bhqk", qr, k_rope)
        p = jax.nn.softmax(s.astype(jnp.float32) * scale, axis=-1).astype(v.dtype)
        return jnp.einsum("bhqk,bkhd->bqhd", p, v)

    o = lax.map(one_block, (to_blocks(q_nope), to_blocks(q_rope)))
    return jnp.moveaxis(o, 0, 1).reshape(B, S, MLA_HEADS * MLA_V)


def hybrid_mixer(h, w_in, rpb, t5_table, g_q, g_kv, w_uq, w_ukv, w_out, cos, sin):
    B, S, _ = h.shape
    z = h @ w_in
    bounds = [int(b) for b in np.cumsum([D_A, D_A, D_A, D_B, D_B, D_B, MLA_Q_LORA, MLA_KV_LORA])]
    q_a, k_a, v_a, q_b, k_b, v_b, c_q, c_kv, k_rope_raw = jnp.split(z, bounds, axis=-1)

    def heads(t, n):
        return t.reshape(B, S, n, HEAD_DIM)

    o_a = neighbourhood_attention(heads(q_a, NA_HEADS), heads(k_a, NA_HEADS),
                                  heads(v_a, NA_HEADS), rpb).reshape(B, S, D_A)
    o_b = dilated_attention(heads(q_b, DIL_HEADS), heads(k_b, DIL_HEADS),
                            heads(v_b, DIL_HEADS), t5_table).reshape(B, S, D_B)
    o_c = latent_attention(c_q, c_kv, k_rope_raw, g_q, g_kv, w_uq, w_ukv, cos, sin)
    return jnp.concatenate([o_a, o_b, o_c], axis=-1) @ w_out


def conv_ffn(h, w_up, conv_w, conv_b, w_down):
    S = h.shape[1]
    a, g = jnp.split(h @ w_up, 2, axis=-1)
    pad = CONV_W // 2
    gp = jnp.pad(g, ((0, 0), (pad, pad), (0, 0)))
    gc = conv_b
    for i in range(CONV_W):
        gc = gc + gp[:, i:i + S] * conv_w[i]
    return (jax.nn.gelu(gc) * a) @ w_down


def setup_inputs(seed: int = 0) -> dict:
    key = jax.random.key(seed)
    ks = jax.random.split(key, 20)

    def nrm(k, shape, s):
        return jax.random.normal(k, shape, jnp.float32) * s

    def gain(k, shape):
        return 1.0 + 0.05 * jax.random.normal(k, shape, jnp.float32)

    return {
        "x": nrm(ks[0], (BATCH, SEQ, D_MODEL), 1.0),
        "c": nrm(ks[1], (BATCH, D_MODEL), 1.0),
        "w_ada": nrm(ks[2], (DEPTH, D_MODEL, N_MOD * D_MODEL), 0.5 * D_MODEL ** -0.5),
        "b_ada": nrm(ks[3], (DEPTH, N_MOD * D_MODEL), 0.02),
        "g_pre_mix": gain(ks[4], (DEPTH, D_MODEL)),
        "g_post_mix": gain(ks[5], (DEPTH, D_MODEL)),
        "g_pre_ffn": gain(ks[6], (DEPTH, D_MODEL)),
        "g_post_ffn": gain(ks[7], (DEPTH, D_MODEL)),
        "w_in": nrm(ks[8], (DEPTH, D_MODEL, D_IN), D_MODEL ** -0.5),
        "na_rpb": nrm(ks[9], (DEPTH, NA_HEADS, 2 * NA_KH_MAX - 1, 2 * NA_KW - 1), 0.1),
        "t5_table": nrm(ks[10], (T5_BUCKETS, DIL_HEADS), 0.1),
        "mla_g_q": gain(ks[11], (DEPTH, MLA_Q_LORA)),
        "mla_g_kv": gain(ks[12], (DEPTH, MLA_KV_LORA)),
        "w_uq": nrm(ks[13], (DEPTH, MLA_Q_LORA, MLA_HEADS * (MLA_NOPE + MLA_ROPE)), MLA_Q_LORA ** -0.5),
        "w_ukv": nrm(ks[14], (DEPTH, MLA_KV_LORA, MLA_HEADS * (MLA_NOPE + MLA_V)), MLA_KV_LORA ** -0.5),
        "w_out": nrm(ks[15], (DEPTH, D_MIX, D_MODEL), D_MIX ** -0.5),
        "w_up": nrm(ks[16], (DEPTH, D_MODEL, 2 * D_FF), D_MODEL ** -0.5),
        "conv_w": nrm(ks[17], (DEPTH, CONV_W, D_FF), CONV_W ** -0.5),
        "conv_b": nrm(ks[18], (DEPTH, D_FF), 0.02),
        "w_down": nrm(ks[19], (DEPTH, D_FF, D_MODEL), D_FF ** -0.5),
    }


def reference(x, c, w_ada, b_ada, g_pre_mix, g_post_mix, g_pre_ffn, g_post_ffn, w_in, na_rpb,
              t5_table, mla_g_q, mla_g_kv, w_uq, w_ukv, w_out, w_up, conv_w, conv_b, w_down):
    S = x.shape[1]
    cos, sin = rope_tables(S)
    c_act = jax.nn.silu(c)
    for l in range(DEPTH):
        mod = c_act @ w_ada[l] + b_ada[l]
        sh1, sc1, g1, sh2, sc2, g2 = [m[:, None, :] for m in jnp.split(mod, N_MOD, axis=-1)]
        h = rms_norm(x, g_pre_mix[l]) * (1.0 + sc1) + sh1
        y = hybrid_mixer(h, w_in[l], na_rpb[l], t5_table, mla_g_q[l], mla_g_kv[l],
                         w_uq[l], w_ukv[l], w_out[l], cos, sin)
        x = x + g1 * rms_norm(y, g_post_mix[l])
        h = rms_norm(x, g_pre_ffn[l]) * (1.0 + sc2) + sh2
        y = conv_ffn(h, w_up[l], conv_w[l], conv_b[l], w_down[l])
        x = x + g2 * rms_norm(y, g_post_ffn[l])
    return x
```

```python
import functools
import math

import numpy as np
import jax
import jax.numpy as jnp
from jax import lax
from jax.experimental import pallas as pl
from jax.experimental.pallas import tpu as pltpu

F32 = jnp.float32
BF16 = jnp.bfloat16

D_MODEL = 1024
HEAD_DIM = 64
GRID_W = 64
EPS = 1e-6
NEG_INF = -1e30

NA_HEADS = 4
NA_KH_MAX = 8
NA_KW = 16

DIL_HEADS = 8
DIL_PATTERNS = ((128, 1), (512, 4), (2048, 16))
DIL_BLK = 128
DIL_HALF = 64

MLA_HEADS = 4
MLA_Q_LORA = 384
MLA_KV_LORA = 256
MLA_NOPE = 64
MLA_ROPE = 32
MLA_V = 64
ROPE_THETA = 10000.0

T5_BUCKETS = 32
T5_MAX_DIST = 1024

D_FF = 2816
CONV_W = 3

D_A = NA_HEADS * HEAD_DIM
D_B = DIL_HEADS * HEAD_DIM
D_C = MLA_HEADS * MLA_V
N_MOD = 6

LANES = 128
SUBLANES = 8
VMEM_LIMIT = 56 * 1024 * 1024

ZC_W = MLA_Q_LORA + MLA_KV_LORA + 2 * LANES
MLA_QK_W = MLA_HEADS * LANES


def _params(n_axes):
    return pltpu.CompilerParams(dimension_semantics=("arbitrary",) * n_axes,
                                vmem_limit_bytes=VMEM_LIMIT)


def _dot(a, b):
    return jnp.dot(a, b, preferred_element_type=F32)


def _dot_nt(a, b):
    return lax.dot_general(a, b, (((1,), (1,)), ((), ())), preferred_element_type=F32)


def _rms(x, g):
    return x * lax.rsqrt(jnp.mean(x * x, axis=-1, keepdims=True) + EPS) * g


def _mod_kernel(c_ref, w_ref, b_ref, o_ref):
    ca = jax.nn.silu(c_ref[...]).astype(BF16)
    o_ref[0] = _dot(ca, w_ref[0].astype(BF16)) + b_ref[0]


def _modulation(c, w_ada, b_ada):
    depth, d, n = w_ada.shape
    bsz = c.shape[0]
    tn = 1536
    return pl.pallas_call(
        _mod_kernel,
        grid=(depth, n // tn),
        in_specs=[pl.BlockSpec((bsz, d), lambda l, j: (0, 0)),
                  pl.BlockSpec((1, d, tn), lambda l, j: (l, 0, j)),
                  pl.BlockSpec((1, 1, tn), lambda l, j: (l, 0, j))],
        out_specs=pl.BlockSpec((1, bsz, tn), lambda l, j: (l, 0, j)),
        out_shape=jax.ShapeDtypeStruct((depth, bsz, n), F32),
        compiler_params=_params(2),
        name="adaln_mod",
    )(c, w_ada, b_ada.reshape(depth, 1, n))


def _in_kernel(x_ref, g_ref, sh_ref, sc_ref, wa_ref, wb_ref, wc_ref, gq_ref, gkv_ref,
               wuq_ref, wuqr_ref, wuk_ref, wuv_ref, cq_ref, sq_ref, ck_ref, sk_ref,
               za_ref, zb_ref, qm_ref, km_ref, vm_ref):
    x = x_ref[...]
    h = (_rms(x, g_ref[...]) * (1.0 + sc_ref[...]) + sh_ref[...]).astype(BF16)
    za_ref[...] = _dot(h, wa_ref[...]).astype(BF16)
    nb = zb_ref.shape[1]
    for j in range(0, nb, 512):
        zb_ref[:, j:j + 512] = _dot(h, wb_ref[:, j:j + 512])
    zc = _dot(h, wc_ref[...])
    c_q = zc[:, 0:MLA_Q_LORA]
    c_kv = zc[:, MLA_Q_LORA:MLA_Q_LORA + MLA_KV_LORA]
    kr_a = zc[:, MLA_Q_LORA + MLA_KV_LORA:MLA_Q_LORA + MLA_KV_LORA + LANES]
    kr_b = zc[:, MLA_Q_LORA + MLA_KV_LORA + LANES:ZC_W]
    cqn = _rms(c_q, gq_ref[...]).astype(BF16)
    q = _dot(cqn, wuq_ref[...]) * cq_ref[...] + _dot(cqn, wuqr_ref[...]) * sq_ref[...]
    qm_ref[...] = q.astype(BF16)
    ckvn = _rms(c_kv, gkv_ref[...]).astype(BF16)
    kr = kr_a * ck_ref[...] + kr_b * sk_ref[...]
    km_ref[...] = (_dot(ckvn, wuk_ref[...]) + jnp.concatenate([kr] * MLA_HEADS, axis=1)).astype(BF16)
    vm_ref[...] = _dot(ckvn, wuv_ref[...]).astype(BF16)


def _in_proj(x, mod3, l, bsz, seq, g_pre, wa, wb, wc, gq, gkv, wuq, wuqr, wuk, wuv, cq, sq, ck, sk):
    n = x.shape[0]
    tm = 512
    tps = seq // tm
    row = lambda i: (i, 0)
    const = lambda i: (0, 0)
    pos = lambda i: (i % tps, 0)
    mrow = lambda k: (lambda i: ((l * bsz + i // tps) * N_MOD + k, 0, 0))
    full = lambda a: pl.BlockSpec(a.shape, const)
    return pl.pallas_call(
        _in_kernel,
        grid=(n // tm,),
        in_specs=[pl.BlockSpec((tm, D_MODEL), row),
                  full(g_pre),
                  pl.BlockSpec((None, 1, D_MODEL), mrow(0)),
                  pl.BlockSpec((None, 1, D_MODEL), mrow(1)),
                  full(wa), full(wb), full(wc), full(gq), full(gkv),
                  full(wuq), full(wuqr), full(wuk), full(wuv),
                  pl.BlockSpec((tm, MLA_QK_W), pos), pl.BlockSpec((tm, MLA_QK_W), pos),
                  pl.BlockSpec((tm, LANES), pos), pl.BlockSpec((tm, LANES), pos)],
        out_specs=[pl.BlockSpec((tm, 3 * D_A), row),
                   pl.BlockSpec((tm, 3 * D_B), row),
                   pl.BlockSpec((tm, MLA_QK_W), row),
                   pl.BlockSpec((tm, MLA_QK_W), row),
                   pl.BlockSpec((tm, D_C), row)],
        out_shape=[jax.ShapeDtypeStruct((n, 3 * D_A), BF16),
                   jax.ShapeDtypeStruct((n, 3 * D_B), F32),
                   jax.ShapeDtypeStruct((n, MLA_QK_W), BF16),
                   jax.ShapeDtypeStruct((n, MLA_QK_W), BF16),
                   jax.ShapeDtypeStruct((n, D_C), BF16)],
        compiler_params=_params(1),
        name="in_proj",
    )(x, g_pre, mod3, mod3, wa, wb, wc, gq, gkv, wuq, wuqr, wuk, wuv, cq, sq, ck, sk)


NA_ROWS_PER_STEP = 8


def _na_kernel(q_ref, k_ref, v_ref, tbl_ref, o_ref, *, rows):
    i = pl.program_id(1)
    lane = lax.broadcasted_iota(jnp.int32, (1, LANES), 1)
    low = lane < HEAD_DIM
    kh = NA_KH_MAX

    def one_row(j, carry):
        r = i * NA_ROWS_PER_STEP + j
        r0 = jnp.clip(r - kh // 2, 0, rows - kh)
        variant = r - r0
        qs = pl.multiple_of(j * GRID_W, GRID_W)
        ks = pl.multiple_of(r0 * GRID_W, GRID_W)
        for pair in range(NA_HEADS // 2):
            cols = slice(pair * LANES, (pair + 1) * LANES)
            q2 = q_ref[pl.ds(qs, GRID_W), cols] * 0.125
            k2 = k_ref[pl.ds(ks, kh * GRID_W), cols]
            v2 = v_ref[pl.ds(ks, kh * GRID_W), cols]
            out = jnp.zeros((GRID_W, LANES), F32)
            for w in range(2):
                sel = low if w == 0 else jnp.logical_not(low)
                qm = jnp.where(sel, q2, jnp.zeros_like(q2))
                s = _dot_nt(qm, k2) + tbl_ref[variant, 2 * pair + w]
                m = jnp.max(s, axis=-1, keepdims=True)
                p = jnp.exp(s - m)
                den = jnp.sum(p, axis=-1, keepdims=True)
                o = _dot(p.astype(BF16), v2) / den
                out = jnp.where(sel, o, out)
            o_ref[pl.ds(qs, GRID_W), cols] = out.astype(BF16)
        return carry

    lax.fori_loop(0, NA_ROWS_PER_STEP, one_row, 0)


def _na_attention(za3, tbl):
    bsz, seq, _ = za3.shape
    rows = seq // GRID_W
    tq = NA_ROWS_PER_STEP * GRID_W
    return pl.pallas_call(
        functools.partial(_na_kernel, rows=rows),
        grid=(bsz, seq // tq),
        in_specs=[pl.BlockSpec((None, tq, D_A), lambda b, i: (b, i, 0)),
                  pl.BlockSpec((None, seq, D_A), lambda b, i: (b, 0, 1)),
                  pl.BlockSpec((None, seq, D_A), lambda b, i: (b, 0, 2)),
                  pl.BlockSpec(tbl.shape, lambda b, i: (0, 0, 0, 0))],
        out_specs=pl.BlockSpec((None, tq, D_A), lambda b, i: (b, i, 0)),
        out_shape=jax.ShapeDtypeStruct((bsz, seq, D_A), BF16),
        compiler_params=_params(2),
        name="na_attn",
    )(za3, za3, za3, tbl)


def _na_bias_table(rpb, rows):
    kh = min(NA_KH_MAX, rows)
    col = np.arange(GRID_W)
    c_start = np.clip(col - NA_KW // 2, 0, GRID_W - NA_KW)
    kcol = np.arange(GRID_W)
    valid = (kcol[None, :] >= c_start[:, None]) & (kcol[None, :] < c_start[:, None] + NA_KW)
    d_col = np.clip(kcol[None, :] - col[:, None] + (NA_KW - 1), 0, 2 * NA_KW - 2)
    d_row = np.arange(kh)[None, :] - np.arange(kh)[:, None] + (NA_KH_MAX - 1)
    t = rpb[:, d_row[:, None, :, None], d_col[None, :, None, :]]
    t = jnp.where(valid[None, None, :, None, :], t.astype(F32), NEG_INF)
    t = jnp.transpose(t, (1, 0, 2, 3, 4))
    return t.reshape(kh, NA_HEADS, GRID_W, kh * GRID_W)


DIL_PAD = DIL_HALF * max(d for _, d in DIL_PATTERNS)
DIL_KB = DIL_BLK + 2 * DIL_HALF


def _dil_kernel(q_ref, k_ref, v_ref, tbl_ref, o_ref, kp_ref, vp_ref, acc_ref, m_ref, l_ref, *, seq):
    zeros_pad = jnp.zeros((DIL_PAD, LANES), F32)
    kp_ref[0:DIL_PAD, :] = zeros_pad
    kp_ref[DIL_PAD + seq:DIL_PAD + seq + DIL_PAD, :] = zeros_pad
    vp_ref[0:DIL_PAD, :] = zeros_pad
    vp_ref[DIL_PAD + seq:DIL_PAD + seq + DIL_PAD, :] = zeros_pad
    kp_ref[DIL_PAD:DIL_PAD + seq, :] = k_ref[...]
    vp_ref[DIL_PAD:DIL_PAD + seq, :] = v_ref[...]
    acc_ref[...] = jnp.zeros((seq, LANES), F32)
    l_ref[...] = jnp.zeros((seq, LANES), F32)
    m_ref[...] = jnp.full((seq, LANES), NEG_INF, F32)

    lane = lax.broadcasted_iota(jnp.int32, (1, LANES), 1)
    low = lane < HEAD_DIM

    for pat, (_, d) in enumerate(DIL_PATTERNS):
        nblk = seq // d // DIL_BLK

        def block(t, carry, pat=pat, d=d, nblk=nblk):
            r = t // nblk
            n = t % nblk
            variant = (n == 0).astype(jnp.int32) + 2 * (n == nblk - 1).astype(jnp.int32)
            q_start = r + d * DIL_BLK * n
            k_start = DIL_PAD + r + d * (DIL_BLK * n - DIL_HALF)
            if d == 1:
                rows_q = pl.ds(q_start, DIL_BLK)
                rows_k = pl.ds(k_start, DIL_KB)
            else:
                rows_q = pl.ds(q_start, DIL_BLK, stride=d)
                rows_k = pl.ds(k_start, DIL_KB, stride=d)
            q2 = (q_ref[rows_q, :] * 0.125).astype(BF16)
            k2 = kp_ref[rows_k, :].astype(BF16)
            v2 = vp_ref[rows_k, :].astype(BF16)
            m_b = jnp.zeros((DIL_BLK, LANES), F32)
            l_b = jnp.zeros((DIL_BLK, LANES), F32)
            o_b = jnp.zeros((DIL_BLK, LANES), F32)
            for w in range(2):
                sel = low if w == 0 else jnp.logical_not(low)
                qm = jnp.where(sel, q2, jnp.zeros_like(q2))
                s = _dot_nt(qm, k2) + tbl_ref[pat, variant, w]
                m = jnp.max(s, axis=-1, keepdims=True)
                p = jnp.exp(s - m)
                den = jnp.sum(p, axis=-1, keepdims=True)
                o = _dot(p.astype(BF16), v2)
                m_b = jnp.where(sel, m, m_b)
                l_b = jnp.where(sel, den, l_b)
                o_b = jnp.where(sel, o, o_b)
            m_old = m_ref[rows_q, :]
            m_new = jnp.maximum(m_old, m_b)
            a_old = jnp.exp(m_old - m_new)
            a_blk = jnp.exp(m_b - m_new)
            acc_ref[rows_q, :] = acc_ref[rows_q, :] * a_old + o_b * a_blk
            l_ref[rows_q, :] = l_ref[rows_q, :] * a_old + l_b * a_blk
            m_ref[rows_q, :] = m_new
            return carry

        lax.fori_loop(0, d * nblk, block, 0)

    o_ref[...] = (acc_ref[...] / l_ref[...]).astype(BF16)


def _dil_attention(zb3, tbl):
    bsz, seq, _ = zb3.shape
    npair = DIL_HEADS // 2
    blk = lambda off: pl.BlockSpec((None, seq, LANES), lambda b, p: (b, 0, off + p))
    return pl.pallas_call(
        functools.partial(_dil_kernel, seq=seq),
        grid=(bsz, npair),
        in_specs=[blk(0), blk(npair), blk(2 * npair),
                  pl.BlockSpec((len(DIL_PATTERNS), 4, 2, DIL_BLK, DIL_KB), lambda b, p: (0, 0, p, 0, 0))],
        out_specs=pl.BlockSpec((None, seq, LANES), lambda b, p: (b, 0, p)),
        out_shape=jax.ShapeDtypeStruct((bsz, seq, D_B), BF16),
        scratch_shapes=[pltpu.VMEM((seq + 2 * DIL_PAD, LANES), F32),
                        pltpu.VMEM((seq + 2 * DIL_PAD, LANES), F32),
                        pltpu.VMEM((seq, LANES), F32),
                        pltpu.VMEM((seq, LANES), F32),
                        pltpu.VMEM((seq, LANES), F32)],
        compiler_params=_params(2),
        name="dil_attn",
    )(zb3, zb3, zb3, tbl)


def _t5_bucket(rel):
    nb = T5_BUCKETS // 2
    max_exact = nb // 2
    n = np.abs(rel)
    large = max_exact + (np.log(np.maximum(n, 1) / max_exact)
                         / math.log(T5_MAX_DIST / max_exact) * (nb - max_exact)).astype(np.int64)
    large = np.minimum(large, nb - 1)
    return (np.where(rel > 0, nb, 0) + np.where(n < max_exact, n, large)).astype(np.int32)


def _dil_bias_table(t5_table):
    a = np.arange(DIL_BLK)[:, None]
    j = np.arange(DIL_KB)[None, :]
    rel = j - DIL_HALF - a
    band = np.abs(rel) <= DIL_HALF
    not_before = np.broadcast_to(j >= DIL_HALF, band.shape)
    not_after = np.broadcast_to(j < DIL_BLK + DIL_HALF, band.shape)
    valid = np.stack([band, band & not_before, band & not_after, band & not_before & not_after])
    out = []
    for _, d in DIL_PATTERNS:
        bias = jnp.transpose(t5_table[_t5_bucket(rel * d)], (2, 0, 1)).astype(F32)
        out.append(jnp.where(valid[:, None], bias[None], NEG_INF))
    return jnp.stack(out)


MLA_TQ = 256


def _mla_kernel(q_ref, k_ref, v_ref, o_ref):
    scale = (MLA_NOPE + MLA_ROPE) ** -0.5
    lane = lax.broadcasted_iota(jnp.int32, (1, D_C), 1)
    out = jnp.zeros((MLA_TQ, D_C), F32)
    v = v_ref[...]
    for h in range(MLA_HEADS):
        cols = slice(h * LANES, (h + 1) * LANES)
        s = _dot_nt(q_ref[:, cols], k_ref[:, cols]) * scale
        m = jnp.max(s, axis=-1, keepdims=True)
        p = jnp.exp(s - m)
        den = jnp.sum(p, axis=-1, keepdims=True)
        o = _dot(p.astype(BF16), v) / den
        out = jnp.where((lane >= h * MLA_V) & (lane < (h + 1) * MLA_V), o, out)
    o_ref[...] = out.astype(BF16)


def _mla_attention(qm3, km3, vm3):
    bsz, seq, _ = qm3.shape
    return pl.pallas_call(
        _mla_kernel,
        grid=(bsz, seq // MLA_TQ),
        in_specs=[pl.BlockSpec((None, MLA_TQ, MLA_QK_W), lambda b, i: (b, i, 0)),
                  pl.BlockSpec((None, seq, MLA_QK_W), lambda b, i: (b, 0, 0)),
                  pl.BlockSpec((None, seq, D_C), lambda b, i: (b, 0, 0))],
        out_specs=pl.BlockSpec((None, MLA_TQ, D_C), lambda b, i: (b, i, 0)),
        out_shape=jax.ShapeDtypeStruct((bsz, seq, D_C), BF16),
        compiler_params=_params(2),
        name="mla_attn",
    )(qm3, km3, vm3)


def _out_kernel(x_ref, oa_ref, ob_ref, oc_ref, w1_ref, w2_ref, w3_ref, g_ref, gate_ref, o_ref):
    y = _dot(oa_ref[...], w1_ref[...]) + _dot(ob_ref[...], w2_ref[...]) + _dot(oc_ref[...], w3_ref[...])
    o_ref[...] = x_ref[...] + gate_ref[...] * _rms(y, g_ref[...])


def _out_proj(x, oa, ob, oc, w1, w2, w3, g_post, mod3, l, bsz, seq):
    n = x.shape[0]
    tm = 512
    tps = seq // tm
    row = lambda i: (i, 0)
    const = lambda i: (0, 0)
    full = lambda a: pl.BlockSpec(a.shape, const)
    return pl.pallas_call(
        _out_kernel,
        grid=(n // tm,),
        in_specs=[pl.BlockSpec((tm, D_MODEL), row),
                  pl.BlockSpec((tm, D_A), row), pl.BlockSpec((tm, D_B), row), pl.BlockSpec((tm, D_C), row),
                  full(w1), full(w2), full(w3), full(g_post),
                  pl.BlockSpec((None, 1, D_MODEL), lambda i: ((l * bsz + i // tps) * N_MOD + 2, 0, 0))],
        out_specs=pl.BlockSpec((tm, D_MODEL), row),
        out_shape=jax.ShapeDtypeStruct((n, D_MODEL), F32),
        compiler_params=_params(1),
        name="out_proj",
    )(x, oa, ob, oc, w1, w2, w3, g_post, mod3)


FFN_TM = 256
FFN_CHUNK = 256


def _ffn_kernel(x_ref, xp_ref, xn_ref, g_ref, sh_ref, sc_ref, wa_ref, wg_ref, cw_ref, cb_ref, wd_ref,
                gpost_ref, gate_ref, o_ref, *, tiles_per_seq):
    i = pl.program_id(0)
    has_prev = (i % tiles_per_seq) != 0
    has_next = (i % tiles_per_seq) != tiles_per_seq - 1
    g = g_ref[...]
    sc = 1.0 + sc_ref[...]
    sh = sh_ref[...]
    x = x_ref[...]
    pre = lambda t: _rms(t, g) * sc + sh
    h_prev = jnp.where(has_prev, pre(xp_ref[...]), 0.0)
    h_next = jnp.where(has_next, pre(xn_ref[...]), 0.0)
    h_mid = pre(x)
    h = h_mid.astype(BF16)
    h_ext = jnp.concatenate([h_prev, h_mid, h_next], axis=0).astype(BF16)
    tm = x.shape[0]
    y = jnp.zeros((tm, D_MODEL), F32)
    for c0 in range(0, D_FF, FFN_CHUNK):
        cols = slice(c0, c0 + FFN_CHUNK)
        a = _dot(h, wa_ref[:, cols])
        ge = _dot(h_ext, wg_ref[:, cols])
        g_prev = ge[SUBLANES - 1:SUBLANES - 1 + tm]
        g_mid = ge[SUBLANES:SUBLANES + tm]
        g_next = ge[SUBLANES + 1:SUBLANES + 1 + tm]
        gc = cb_ref[:, cols] + g_prev * cw_ref[0:1, cols]
        gc = gc + g_mid * cw_ref[1:2, cols]
        gc = gc + g_next * cw_ref[2:3, cols]
        u = (jax.nn.gelu(gc) * a).astype(BF16)
        y = y + _dot(u, wd_ref[cols, :])
    o_ref[...] = x + gate_ref[...] * _rms(y, gpost_ref[...])


def _ffn(x, mod3, l, bsz, seq, g_pre, wa, wg, cw, cb, wd, g_post):
    n = x.shape[0]
    tm = FFN_TM
    tps = seq // tm
    r8 = tm // SUBLANES
    last8 = n // SUBLANES - 1
    const = lambda i: (0, 0)
    full = lambda a: pl.BlockSpec(a.shape, const)
    mrow = lambda k: (lambda i: ((l * bsz + i // tps) * N_MOD + k, 0, 0))
    return pl.pallas_call(
        functools.partial(_ffn_kernel, tiles_per_seq=tps),
        grid=(n // tm,),
        in_specs=[pl.BlockSpec((tm, D_MODEL), lambda i: (i, 0)),
                  pl.BlockSpec((SUBLANES, D_MODEL), lambda i: (jnp.maximum(i * r8 - 1, 0), 0)),
                  pl.BlockSpec((SUBLANES, D_MODEL), lambda i: (jnp.minimum((i + 1) * r8, last8), 0)),
                  full(g_pre),
                  pl.BlockSpec((None, 1, D_MODEL), mrow(3)),
                  pl.BlockSpec((None, 1, D_MODEL), mrow(4)),
                  full(wa), full(wg), full(cw), full(cb), full(wd), full(g_post),
                  pl.BlockSpec((None, 1, D_MODEL), mrow(5))],
        out_specs=pl.BlockSpec((tm, D_MODEL), lambda i: (i, 0)),
        out_shape=jax.ShapeDtypeStruct((n, D_MODEL), F32),
        compiler_params=_params(1),
        name="conv_ffn",
    )(x, x, x, g_pre, mod3, mod3, wa, wg, cw, cb, wd, g_post, mod3)


def _rope_tables(seq):
    inv_freq = jnp.asarray(ROPE_THETA ** (-np.arange(0, MLA_ROPE, 2, dtype=np.float32) / MLA_ROPE), F32)
    ang = jnp.arange(seq, dtype=F32)[:, None] * inv_freq[None, :]
    cos, sin = jnp.cos(ang), jnp.sin(ang)
    cos2 = jnp.concatenate([cos, cos], axis=1)
    sin2 = jnp.concatenate([sin, sin], axis=1)
    head_c = jnp.concatenate([cos2, jnp.ones((seq, MLA_NOPE), F32), jnp.zeros((seq, LANES - MLA_NOPE - MLA_ROPE), F32)], axis=1)
    head_s = jnp.concatenate([sin2, jnp.zeros((seq, LANES - MLA_ROPE), F32)], axis=1)
    cq = jnp.concatenate([head_c] * MLA_HEADS, axis=1)
    sq = jnp.concatenate([head_s] * MLA_HEADS, axis=1)
    ck = jnp.concatenate([cos2, jnp.zeros((seq, LANES - MLA_ROPE), F32)], axis=1)
    return cq, sq, ck, head_s


def _rot_half_cols(w):
    half = w.shape[1] // 2
    return jnp.concatenate([-w[:, half:], w[:, :half]], axis=1)


def _layer_weights(w_in, w_uq, w_ukv, w_out, w_up, w_down):
    d = w_in.shape[0]
    o_b = 3 * D_A
    o_c = o_b + 3 * D_B
    o_kr = o_c + MLA_Q_LORA + MLA_KV_LORA
    wa = w_in[:, :o_b].astype(BF16)
    wb = w_in[:, o_b:o_c].astype(BF16)
    w_kr = w_in[:, o_kr:o_kr + MLA_ROPE]
    zpad = jnp.zeros((d, LANES - MLA_ROPE), F32)
    wc = jnp.concatenate([w_in[:, o_c:o_kr], w_kr, zpad, _rot_half_cols(w_kr), zpad], axis=1).astype(BF16)

    hq = MLA_NOPE + MLA_ROPE
    q_cols, qr_cols, k_cols, v_cols = [], [], [], []
    zq = jnp.zeros((MLA_Q_LORA, LANES - hq), F32)
    for h in range(MLA_HEADS):
        nope = w_uq[:, h * hq:h * hq + MLA_NOPE]
        rope = w_uq[:, h * hq + MLA_NOPE:(h + 1) * hq]
        q_cols += [rope, nope, zq]
        qr_cols += [_rot_half_cols(rope), jnp.zeros((MLA_Q_LORA, LANES - MLA_ROPE), F32)]
        hk = MLA_NOPE + MLA_V
        k_cols += [jnp.zeros((MLA_KV_LORA, MLA_ROPE), F32), w_ukv[:, h * hk:h * hk + MLA_NOPE],
                   jnp.zeros((MLA_KV_LORA, LANES - hq), F32)]
        v_cols += [w_ukv[:, h * hk + MLA_NOPE:(h + 1) * hk]]
    cat = lambda cols: jnp.concatenate(cols, axis=1).astype(BF16)
    w1 = w_out[:D_A].astype(BF16)
    w2 = w_out[D_A:D_A + D_B].astype(BF16)
    w3 = w_out[D_A + D_B:].astype(BF16)
    w_a = w_up[:, :D_FF].astype(BF16)
    w_g = w_up[:, D_FF:].astype(BF16)
    return (wa, wb, wc, cat(q_cols), cat(qr_cols), cat(k_cols), cat(v_cols), w1, w2, w3, w_a, w_g,
            w_down.astype(BF16))


def kernel(x, c, w_ada, b_ada, g_pre_mix, g_post_mix, g_pre_ffn, g_post_ffn, w_in, na_rpb, t5_table,
           mla_g_q, mla_g_kv, w_uq, w_ukv, w_out, w_up, conv_w, conv_b, w_down):
    bsz, seq, d = x.shape
    depth = w_in.shape[0]
    n = bsz * seq
    rows = seq // GRID_W
    assert d == D_MODEL and seq % (DIL_BLK * max(dd for _, dd in DIL_PATTERNS)) == 0 and rows >= NA_KH_MAX

    mod3 = _modulation(c, w_ada, b_ada).reshape(depth * bsz * N_MOD, 1, D_MODEL)
    cq, sq, ck, sk = _rope_tables(seq)
    dil_tbl = _dil_bias_table(t5_table)
    row2 = lambda v: v.reshape(1, -1)

    xf = x.reshape(n, d)
    for l in range(depth):
        (wa, wb, wc, wuq, wuqr, wuk, wuv, w1, w2, w3, w_a, w_g, w_d) = _layer_weights(
            w_in[l], w_uq[l], w_ukv[l], w_out[l], w_up[l], w_down[l])
        za, zb, qm, km, vm = _in_proj(xf, mod3, l, bsz, seq, row2(g_pre_mix[l]), wa, wb, wc,
                                      row2(mla_g_q[l]), row2(mla_g_kv[l]), wuq, wuqr, wuk, wuv,
                                      cq, sq, ck, sk)
        o_a = _na_attention(za.reshape(bsz, seq, 3 * D_A), _na_bias_table(na_rpb[l], rows))
        o_b = _dil_attention(zb.reshape(bsz, seq, 3 * D_B), dil_tbl)
        o_c = _mla_attention(qm.reshape(bsz, seq, MLA_QK_W), km.reshape(bsz, seq, MLA_QK_W),
                             vm.reshape(bsz, seq, D_C))
        xf = _out_proj(xf, o_a.reshape(n, D_A), o_b.reshape(n, D_B), o_c.reshape(n, D_C),
                       w1, w2, w3, row2(g_post_mix[l]), mod3, l, bsz, seq)
        xf = _ffn(xf, mod3, l, bsz, seq, row2(g_pre_ffn[l]), w_a, w_g, conv_w[l], row2(conv_b[l]), w_d,
                  row2(g_post_ffn[l]))
    return xf.reshape(bsz, seq, d)
```

```python
import functools
import math

import numpy as np
import jax
import jax.numpy as jnp
from jax import lax
from jax.experimental import pallas as pl
from jax.experimental.pallas import tpu as pltpu

F32 = jnp.float32
BF16 = jnp.bfloat16

D_MODEL = 1024
HEAD_DIM = 64
GRID_W = 64
EPS = 1e-6
NEG_INF = -1e30

NA_HEADS = 4
NA_KH_MAX = 8
NA_KW = 16

DIL_HEADS = 8
DIL_PATTERNS = ((128, 1), (512, 4), (2048, 16))
DIL_BLK = 128
DIL_HALF = 64

MLA_HEADS = 4
MLA_Q_LORA = 384
MLA_KV_LORA = 256
MLA_NOPE = 64
MLA_ROPE = 32
MLA_V = 64
ROPE_THETA = 10000.0

T5_BUCKETS = 32
T5_MAX_DIST = 1024

D_FF = 2816
CONV_W = 3

D_A = NA_HEADS * HEAD_DIM
D_B = DIL_HEADS * HEAD_DIM
D_C = MLA_HEADS * MLA_V
N_MOD = 6

LANES = 128
SUBLANES = 8
VMEM_LIMIT = 56 * 1024 * 1024

ZC_W = MLA_Q_LORA + MLA_KV_LORA + 2 * LANES
MLA_QK_W = MLA_HEADS * LANES


def _params(n_axes):
    return pltpu.CompilerParams(dimension_semantics=("arbitrary",) * n_axes,
                                vmem_limit_bytes=VMEM_LIMIT)


def _dot(a, b):
    return jnp.dot(a, b, preferred_element_type=F32)


def _dot_nt(a, b):
    return lax.dot_general(a, b, (((1,), (1,)), ((), ())), preferred_element_type=F32)


def _rms(x, g):
    return x * lax.rsqrt(jnp.mean(x * x, axis=-1, keepdims=True) + EPS) * g


def _mod_kernel(c_ref, w_ref, b_ref, o_ref):
    ca = jax.nn.silu(c_ref[...]).astype(BF16)
    o_ref[0] = _dot(ca, w_ref[0].astype(BF16)) + b_ref[0]


def _modulation(c, w_ada, b_ada):
    depth, d, n = w_ada.shape
    bsz = c.shape[0]
    tn = 1536
    return pl.pallas_call(
        _mod_kernel,
        grid=(depth, n // tn),
        in_specs=[pl.BlockSpec((bsz, d), lambda l, j: (0, 0)),
                  pl.BlockSpec((1, d, tn), lambda l, j: (l, 0, j)),
                  pl.BlockSpec((1, 1, tn), lambda l, j: (l, 0, j))],
        out_specs=pl.BlockSpec((1, bsz, tn), lambda l, j: (l, 0, j)),
        out_shape=jax.ShapeDtypeStruct((depth, bsz, n), F32),
        compiler_params=_params(2),
        name="adaln_mod",
    )(c, w_ada, b_ada.reshape(depth, 1, n))


def _in_kernel(x_ref, g_ref, sh_ref, sc_ref, wa_ref, wb_ref, wc_ref, gq_ref, gkv_ref,
               wuq_ref, wuqr_ref, wuk_ref, wuv_ref, cq_ref, sq_ref, ck_ref, sk_ref,
               za_ref, zb_ref, qm_ref, km_ref, vm_ref):
    x = x_ref[...]
    h = (_rms(x, g_ref[...]) * (1.0 + sc_ref[...]) + sh_ref[...]).astype(BF16)
    za_ref[...] = _dot(h, wa_ref[...]).astype(BF16)
    nb = zb_ref.shape[1]
    for j in range(0, nb, 512):
        zb_ref[:, j:j + 512] = _dot(h, wb_ref[:, j:j + 512])
    zc = _dot(h, wc_ref[...])
    c_q = zc[:, 0:MLA_Q_LORA]
    c_kv = zc[:, MLA_Q_LORA:MLA_Q_LORA + MLA_KV_LORA]
    kr_a = zc[:, MLA_Q_LORA + MLA_KV_LORA:MLA_Q_LORA + MLA_KV_LORA + LANES]
    kr_b = zc[:, MLA_Q_LORA + MLA_KV_LORA + LANES:ZC_W]
    cqn = _rms(c_q, gq_ref[...]).astype(BF16)
    q = _dot(cqn, wuq_ref[...]) * cq_ref[...] + _dot(cqn, wuqr_ref[...]) * sq_ref[...]
    qm_ref[...] = (q * MLA_Q_SCALE).astype(BF16)
    ckvn = _rms(c_kv, gkv_ref[...]).astype(BF16)
    kr = kr_a * ck_ref[...] + kr_b * sk_ref[...]
    km_ref[...] = (_dot(ckvn, wuk_ref[...]) + jnp.concatenate([kr] * MLA_HEADS, axis=1)).astype(BF16)
    vm_ref[...] = _dot(ckvn, wuv_ref[...]).astype(BF16)


def _in_proj(x, mod3, l, bsz, seq, g_pre, wa, wb, wc, gq, gkv, wuq, wuqr, wuk, wuv, cq, sq, ck, sk):
    n = x.shape[0]
    tm = 512
    tps = seq // tm
    row = lambda i: (i, 0)
    const = lambda i: (0, 0)
    pos = lambda i: (i % tps, 0)
    mrow = lambda k: (lambda i: ((l * bsz + i // tps) * N_MOD + k, 0, 0))
    full = lambda a: pl.BlockSpec(a.shape, const)
    return pl.pallas_call(
        _in_kernel,
        grid=(n // tm,),
        in_specs=[pl.BlockSpec((tm, D_MODEL), row),
                  full(g_pre),
                  pl.BlockSpec((None, 1, D_MODEL), mrow(0)),
                  pl.BlockSpec((None, 1, D_MODEL), mrow(1)),
                  full(wa), full(wb), full(wc), full(gq), full(gkv),
                  full(wuq), full(wuqr), full(wuk), full(wuv),
                  pl.BlockSpec((tm, MLA_QK_W), pos), pl.BlockSpec((tm, MLA_QK_W), pos),
                  pl.BlockSpec((tm, LANES), pos), pl.BlockSpec((tm, LANES), pos)],
        out_specs=[pl.BlockSpec((tm, 3 * D_A), row),
                   pl.BlockSpec((tm, 3 * D_B), row),
                   pl.BlockSpec((tm, MLA_QK_W), row),
                   pl.BlockSpec((tm, MLA_QK_W), row),
                   pl.BlockSpec((tm, MLA_QK_W), row)],
        out_shape=[jax.ShapeDtypeStruct((n, 3 * D_A), BF16),
                   jax.ShapeDtypeStruct((n, 3 * D_B), F32),
                   jax.ShapeDtypeStruct((n, MLA_QK_W), BF16),
                   jax.ShapeDtypeStruct((n, MLA_QK_W), BF16),
                   jax.ShapeDtypeStruct((n, MLA_QK_W), BF16)],
        compiler_params=_params(1),
        name="in_proj",
    )(x, g_pre, mod3, mod3, wa, wb, wc, gq, gkv, wuq, wuqr, wuk, wuv, cq, sq, ck, sk)


NA_ROWS_PER_STEP = 8
NA_UNROLL = 4


def _na_kernel(q_ref, k_ref, v_ref, tbl_ref, o_ref, *, rows):
    i = pl.program_id(1)
    lane = lax.broadcasted_iota(jnp.int32, (1, LANES), 1)
    low = lane < HEAD_DIM
    kh = NA_KH_MAX

    def one_row(j, carry):
        r = i * NA_ROWS_PER_STEP + j
        r0 = jnp.clip(r - kh // 2, 0, rows - kh)
        variant = r - r0
        qs = pl.multiple_of(j * GRID_W, GRID_W)
        ks = pl.multiple_of(r0 * GRID_W, GRID_W)
        for pair in range(NA_HEADS // 2):
            cols = slice(pair * LANES, (pair + 1) * LANES)
            q2 = q_ref[pl.ds(qs, GRID_W), cols] * 0.125
            k2 = k_ref[pl.ds(ks, kh * GRID_W), cols]
            v2 = v_ref[pl.ds(ks, kh * GRID_W), cols]
            out = jnp.zeros((GRID_W, LANES), F32)
            for w in range(2):
                sel = low if w == 0 else jnp.logical_not(low)
                qm = jnp.where(sel, q2, jnp.zeros_like(q2))
                s = _dot_nt(qm, k2) + tbl_ref[variant, 2 * pair + w]
                m = jnp.max(s, axis=-1, keepdims=True)
                p = jnp.exp(s - m)
                den = jnp.sum(p, axis=-1, keepdims=True)
                o = _dot(p.astype(BF16), v2) / den
                out = jnp.where(sel, o, out)
            o_ref[pl.ds(qs, GRID_W), cols] = out.astype(BF16)
        return carry

    lax.fori_loop(0, NA_ROWS_PER_STEP, one_row, 0, unroll=NA_UNROLL)


def _na_attention(za3, tbl):
    bsz, seq, _ = za3.shape
    rows = seq // GRID_W
    tq = NA_ROWS_PER_STEP * GRID_W
    return pl.pallas_call(
        functools.partial(_na_kernel, rows=rows),
        grid=(bsz, seq // tq),
        in_specs=[pl.BlockSpec((None, tq, D_A), lambda b, i: (b, i, 0)),
                  pl.BlockSpec((None, seq, D_A), lambda b, i: (b, 0, 1)),
                  pl.BlockSpec((None, seq, D_A), lambda b, i: (b, 0, 2)),
                  pl.BlockSpec(tbl.shape, lambda b, i: (0, 0, 0, 0))],
        out_specs=pl.BlockSpec((None, tq, D_A), lambda b, i: (b, i, 0)),
        out_shape=jax.ShapeDtypeStruct((bsz, seq, D_A), BF16),
        compiler_params=_params(2),
        name="na_attn",
    )(za3, za3, za3, tbl)


def _na_bias_tables(rpb, rows):
    kh = min(NA_KH_MAX, rows)
    col = np.arange(GRID_W)
    c_start = np.clip(col - NA_KW // 2, 0, GRID_W - NA_KW)
    valid = (col[None, :] >= c_start[:, None]) & (col[None, :] < c_start[:, None] + NA_KW)
    d_col = col[None, :] - col[:, None] + (NA_KW - 1)
    onehot = (d_col[:, :, None] == np.arange(2 * NA_KW - 1)[None, None, :]) & valid[:, :, None]
    by_row = jnp.stack([rpb[:, :, NA_KH_MAX - 1 - v:NA_KH_MAX - 1 - v + kh, :] for v in range(kh)], axis=1)
    t = jnp.einsum("lvhib,cdb->lvhcid", by_row.astype(F32), jnp.asarray(onehot, F32),
                   precision=lax.Precision.HIGHEST)
    t = jnp.where(valid[None, None, None, :, None, :], t, NEG_INF)
    return t.reshape(rpb.shape[0], kh, NA_HEADS, GRID_W, kh * GRID_W)


DIL_PAD = DIL_HALF * max(d for _, d in DIL_PATTERNS)
DIL_KB = DIL_BLK + 2 * DIL_HALF
DIL_UNROLL = 4
DIL_MERGE_ROWS = 512


def _dil_whole_class(seq, d):
    return seq // d <= 2 * DIL_BLK


def _dil_kernel(q_ref, k_ref, v_ref, *rest, seq):
    npat = len(DIL_PATTERNS)
    tbl_refs = rest[:npat]
    o_ref, kp_ref, vp_ref = rest[npat:npat + 3]
    stats = rest[npat + 3:]
    acc_refs, m_refs, l_refs = stats[:npat], stats[npat:2 * npat], stats[2 * npat:]

    zeros_pad = jnp.zeros((DIL_PAD, LANES), F32)
    kp_ref[0:DIL_PAD, :] = zeros_pad
    kp_ref[DIL_PAD + seq:DIL_PAD + seq + DIL_PAD, :] = zeros_pad
    vp_ref[0:DIL_PAD, :] = zeros_pad
    vp_ref[DIL_PAD + seq:DIL_PAD + seq + DIL_PAD, :] = zeros_pad
    kp_ref[DIL_PAD:DIL_PAD + seq, :] = k_ref[...]
    vp_ref[DIL_PAD:DIL_PAD + seq, :] = v_ref[...]

    lane = lax.broadcasted_iota(jnp.int32, (1, LANES), 1)
    low = lane < HEAD_DIM

    def attend(q2, k2, v2, bias):
        nq = q2.shape[0]
        m_b = jnp.zeros((nq, LANES), F32)
        l_b = jnp.zeros((nq, LANES), F32)
        o_b = jnp.zeros((nq, LANES), F32)
        for w in range(2):
            sel = low if w == 0 else jnp.logical_not(low)
            qm = jnp.where(sel, q2, jnp.zeros_like(q2))
            s = _dot_nt(qm, k2) + bias(w)
            m = jnp.max(s, axis=-1, keepdims=True)
            p = jnp.exp(s - m)
            den = jnp.sum(p, axis=-1, keepdims=True)
            o = _dot(p.astype(BF16), v2)
            m_b = jnp.where(sel, m, m_b)
            l_b = jnp.where(sel, den, l_b)
            o_b = jnp.where(sel, o, o_b)
        return o_b, m_b, l_b

    for pat, (_, d) in enumerate(DIL_PATTERNS):
        length = seq // d
        tbl_ref, acc_ref, m_ref, l_ref = tbl_refs[pat], acc_refs[pat], m_refs[pat], l_refs[pat]

        if _dil_whole_class(seq, d):
            def whole(r, carry, d=d, length=length, tbl_ref=tbl_ref, acc_ref=acc_ref, m_ref=m_ref, l_ref=l_ref):
                rows = pl.ds(r, length, stride=d) if d > 1 else pl.ds(r, length)
                q2 = (q_ref[rows, :] * 0.125).astype(BF16)
                o_b, m_b, l_b = attend(q2, k_ref[rows, :].astype(BF16), v_ref[rows, :].astype(BF16),
                                       lambda w: tbl_ref[w])
                acc_ref[rows, :] = o_b
                m_ref[rows, :] = m_b
                l_ref[rows, :] = l_b
                return carry

            lax.fori_loop(0, d, whole, 0, unroll=2)
        else:
            nblk = length // DIL_BLK

            def block(t, carry, d=d, nblk=nblk, tbl_ref=tbl_ref, acc_ref=acc_ref, m_ref=m_ref, l_ref=l_ref):
                r = t // nblk
                n = t % nblk
                variant = (n == 0).astype(jnp.int32) + 2 * (n == nblk - 1).astype(jnp.int32)
                q_start = r + d * DIL_BLK * n
                k_start = DIL_PAD + r + d * (DIL_BLK * n - DIL_HALF)
                if d == 1:
                    rows_q = pl.ds(q_start, DIL_BLK)
                    rows_k = pl.ds(k_start, DIL_KB)
                else:
                    rows_q = pl.ds(q_start, DIL_BLK, stride=d)
                    rows_k = pl.ds(k_start, DIL_KB, stride=d)
                q2 = (q_ref[rows_q, :] * 0.125).astype(BF16)
                o_b, m_b, l_b = attend(q2, kp_ref[rows_k, :].astype(BF16), vp_ref[rows_k, :].astype(BF16),
                                       lambda w: tbl_ref[variant, w])
                acc_ref[rows_q, :] = o_b
                m_ref[rows_q, :] = m_b
                l_ref[rows_q, :] = l_b
                return carry

            lax.fori_loop(0, d * nblk, block, 0, unroll=DIL_UNROLL)

    def merge(c, carry):
        rows = pl.ds(pl.multiple_of(c * DIL_MERGE_ROWS, DIL_MERGE_ROWS), DIL_MERGE_ROWS)
        ms = [m_ref[rows, :] for m_ref in m_refs]
        m_all = functools.reduce(jnp.maximum, ms)
        num = jnp.zeros((DIL_MERGE_ROWS, LANES), F32)
        den = jnp.zeros((DIL_MERGE_ROWS, LANES), F32)
        for m_i, acc_ref, l_ref in zip(ms, acc_refs, l_refs):
            e = jnp.exp(m_i - m_all)
            num = num + acc_ref[rows, :] * e
            den = den + l_ref[rows, :] * e
        o_ref[rows, :] = (num / den).astype(BF16)
        return carry

    lax.fori_loop(0, seq // DIL_MERGE_ROWS, merge, 0)


def _dil_attention(zb3, tbls):
    bsz, seq, _ = zb3.shape
    npair = DIL_HEADS // 2
    blk = lambda off: pl.BlockSpec((None, seq, LANES), lambda b, p: (b, 0, off + p))
    tbl_specs = []
    for t in tbls:
        if t.ndim == 3:
            tbl_specs.append(pl.BlockSpec((2,) + t.shape[1:], lambda b, p: (p, 0, 0)))
        else:
            tbl_specs.append(pl.BlockSpec((t.shape[0], 2) + t.shape[2:], lambda b, p: (0, p, 0, 0)))
    stat = pltpu.VMEM((seq, LANES), F32)
    return pl.pallas_call(
        functools.partial(_dil_kernel, seq=seq),
        grid=(bsz, npair),
        in_specs=[blk(0), blk(npair), blk(2 * npair)] + tbl_specs,
        out_specs=pl.BlockSpec((None, seq, LANES), lambda b, p: (b, 0, p)),
        out_shape=jax.ShapeDtypeStruct((bsz, seq, D_B), BF16),
        scratch_shapes=[pltpu.VMEM((seq + 2 * DIL_PAD, LANES), F32),
                        pltpu.VMEM((seq + 2 * DIL_PAD, LANES), F32)] + [stat] * (3 * len(DIL_PATTERNS)),
        compiler_params=_params(2),
        name="dil_attn",
    )(zb3, zb3, zb3, *tbls)


def _t5_bucket(rel):
    nb = T5_BUCKETS // 2
    max_exact = nb // 2
    n = np.abs(rel)
    large = max_exact + (np.log(np.maximum(n, 1) / max_exact)
                         / math.log(T5_MAX_DIST / max_exact) * (nb - max_exact)).astype(np.int64)
    large = np.minimum(large, nb - 1)
    return (np.where(rel > 0, nb, 0) + np.where(n < max_exact, n, large)).astype(np.int32)


def _dil_bias_tables(t5_table, seq):
    tables = []
    for _, d in DIL_PATTERNS:
        if _dil_whole_class(seq, d):
            pos = np.arange(seq // d)
            rel = pos[None, :] - pos[:, None]
            valid = np.abs(rel) <= DIL_HALF
            bias = jnp.transpose(t5_table[_t5_bucket(rel * d)], (2, 0, 1)).astype(F32)
            tables.append(jnp.where(valid[None], bias, NEG_INF))
        else:
            a = np.arange(DIL_BLK)[:, None]
            j = np.arange(DIL_KB)[None, :]
            rel = j - DIL_HALF - a
            band = np.abs(rel) <= DIL_HALF
            not_before = np.broadcast_to(j >= DIL_HALF, band.shape)
            not_after = np.broadcast_to(j < DIL_BLK + DIL_HALF, band.shape)
            valid = np.stack([band, band & not_before, band & not_after, band & not_before & not_after])
            bias = jnp.transpose(t5_table[_t5_bucket(rel * d)], (2, 0, 1)).astype(F32)
            tables.append(jnp.where(valid[:, None], bias[None], NEG_INF))
    return tables


MLA_TQ = 512
MLA_Q_SCALE = (MLA_NOPE + MLA_ROPE) ** -0.5 * math.log2(math.e)


def _mla_kernel(q_ref, k_ref, v_ref, o_ref):
    lane = lax.broadcasted_iota(jnp.int32, (1, LANES), 1)
    low = lane < MLA_V
    pairs = []
    for pair in range(MLA_HEADS // 2):
        out = jnp.zeros((MLA_TQ, LANES), F32)
        for w in range(2):
            cols = slice((2 * pair + w) * LANES, (2 * pair + w + 1) * LANES)
            s = _dot_nt(q_ref[:, cols], k_ref[:, cols])
            m = jnp.max(s, axis=-1, keepdims=True)
            p = jnp.exp2(s - m)
            den = jnp.sum(p, axis=-1, keepdims=True)
            o = _dot(p.astype(BF16), v_ref[:, cols]) / den
            out = jnp.where(low if w == 0 else jnp.logical_not(low), o, out)
        pairs.append(out)
    o_ref[...] = jnp.concatenate(pairs, axis=1).astype(BF16)


def _mla_attention(qm3, km3, vm3):
    bsz, seq, _ = qm3.shape
    return pl.pallas_call(
        _mla_kernel,
        grid=(bsz, seq // MLA_TQ),
        in_specs=[pl.BlockSpec((None, MLA_TQ, MLA_QK_W), lambda b, i: (b, i, 0)),
                  pl.BlockSpec((None, seq, MLA_QK_W), lambda b, i: (b, 0, 0)),
                  pl.BlockSpec((None, seq, MLA_QK_W), lambda b, i: (b, 0, 0))],
        out_specs=pl.BlockSpec((None, MLA_TQ, D_C), lambda b, i: (b, i, 0)),
        out_shape=jax.ShapeDtypeStruct((bsz, seq, D_C), BF16),
        compiler_params=_params(2),
        name="mla_attn",
    )(qm3, km3, vm3)


def _out_kernel(x_ref, oa_ref, ob_ref, oc_ref, w1_ref, w2_ref, w3_ref, g_ref, gate_ref, o_ref):
    y = _dot(oa_ref[...], w1_ref[...]) + _dot(ob_ref[...], w2_ref[...]) + _dot(oc_ref[...], w3_ref[...])
    o_ref[...] = x_ref[...] + gate_ref[...] * _rms(y, g_ref[...])


def _out_proj(x, oa, ob, oc, w1, w2, w3, g_post, mod3, l, bsz, seq):
    n = x.shape[0]
    tm = 512
    tps = seq // tm
    row = lambda i: (i, 0)
    const = lambda i: (0, 0)
    full = lambda a: pl.BlockSpec(a.shape, const)
    return pl.pallas_call(
        _out_kernel,
        grid=(n // tm,),
        in_specs=[pl.BlockSpec((tm, D_MODEL), row),
                  pl.BlockSpec((tm, D_A), row), pl.BlockSpec((tm, D_B), row), pl.BlockSpec((tm, D_C), row),
                  full(w1), full(w2), full(w3), full(g_post),
                  pl.BlockSpec((None, 1, D_MODEL), lambda i: ((l * bsz + i // tps) * N_MOD + 2, 0, 0))],
        out_specs=pl.BlockSpec((tm, D_MODEL), row),
        out_shape=jax.ShapeDtypeStruct((n, D_MODEL), F32),
        compiler_params=_params(1),
        name="out_proj",
    )(x, oa, ob, oc, w1, w2, w3, g_post, mod3)


FFN_TM = 512
FFN_CHUNK = 256


def _ffn_kernel(x_ref, xp_ref, xn_ref, g_ref, sh_ref, sc_ref, wa_ref, wg_ref, cw_ref, cb_ref, wd_ref,
                gpost_ref, gate_ref, o_ref, u_ref, *, tiles_per_seq):
    i = pl.program_id(0)
    has_prev = (i % tiles_per_seq) != 0
    has_next = (i % tiles_per_seq) != tiles_per_seq - 1
    g = g_ref[...]
    sc = 1.0 + sc_ref[...]
    sh = sh_ref[...]
    x = x_ref[...]
    pre = lambda t: _rms(t, g) * sc + sh
    h_prev = jnp.where(has_prev, pre(xp_ref[...]), 0.0)
    h_next = jnp.where(has_next, pre(xn_ref[...]), 0.0)
    h_mid = pre(x)
    h = h_mid.astype(BF16)
    h_ext = jnp.concatenate([h_prev, h_mid, h_next], axis=0).astype(BF16)
    tm = x.shape[0]
    ext = tm + 2 * SUBLANES
    mid = slice(SUBLANES, SUBLANES + tm)
    for c0 in range(0, D_FF, FFN_CHUNK):
        cols = slice(c0, c0 + FFN_CHUNK)
        a = _dot(h, wa_ref[:, cols])
        ge = _dot(h_ext, wg_ref[:, cols])
        g_prev = pltpu.roll(ge, 1, axis=0)[mid]
        g_next = pltpu.roll(ge, ext - 1, axis=0)[mid]
        gc = cb_ref[:, cols] + g_prev * cw_ref[0:1, cols]
        gc = gc + ge[mid] * cw_ref[1:2, cols]
        gc = gc + g_next * cw_ref[2:3, cols]
        u_ref[:, cols] = (jax.nn.gelu(gc) * a).astype(BF16)
    y = _dot(u_ref[...], wd_ref[...])
    o_ref[...] = x + gate_ref[...] * _rms(y, gpost_ref[...])


def _ffn(x, mod3, l, bsz, seq, g_pre, wa, wg, cw, cb, wd, g_post):
    n = x.shape[0]
    tm = FFN_TM
    tps = seq // tm
    r8 = tm // SUBLANES
    last8 = n // SUBLANES - 1
    const = lambda i: (0, 0)
    full = lambda a: pl.BlockSpec(a.shape, const)
    weight = lambda a: pl.BlockSpec(a.shape, const, pipeline_mode=pl.Buffered(1))
    mrow = lambda k: (lambda i: ((l * bsz + i // tps) * N_MOD + k, 0, 0))
    return pl.pallas_call(
        functools.partial(_ffn_kernel, tiles_per_seq=tps),
        grid=(n // tm,),
        in_specs=[pl.BlockSpec((tm, D_MODEL), lambda i: (i, 0)),
                  pl.BlockSpec((SUBLANES, D_MODEL), lambda i: (jnp.maximum(i * r8 - 1, 0), 0)),
                  pl.BlockSpec((SUBLANES, D_MODEL), lambda i: (jnp.minimum((i + 1) * r8, last8), 0)),
                  full(g_pre),
                  pl.BlockSpec((None, 1, D_MODEL), mrow(3)),
                  pl.BlockSpec((None, 1, D_MODEL), mrow(4)),
                  weight(wa), weight(wg), full(cw), full(cb), weight(wd), full(g_post),
                  pl.BlockSpec((None, 1, D_MODEL), mrow(5))],
        out_specs=pl.BlockSpec((tm, D_MODEL), lambda i: (i, 0)),
        out_shape=jax.ShapeDtypeStruct((n, D_MODEL), F32),
        scratch_shapes=[pltpu.VMEM((tm, D_FF), BF16)],
        compiler_params=_params(1),
        name="conv_ffn",
    )(x, x, x, g_pre, mod3, mod3, wa, wg, cw, cb, wd, g_post, mod3)


def _rope_tables(seq):
    inv_freq = jnp.asarray(ROPE_THETA ** (-np.arange(0, MLA_ROPE, 2, dtype=np.float32) / MLA_ROPE), F32)
    ang = jnp.arange(seq, dtype=F32)[:, None] * inv_freq[None, :]
    cos, sin = jnp.cos(ang), jnp.sin(ang)
    cos2 = jnp.concatenate([cos, cos], axis=1)
    sin2 = jnp.concatenate([sin, sin], axis=1)
    head_c = jnp.concatenate([cos2, jnp.ones((seq, MLA_NOPE), F32), jnp.zeros((seq, LANES - MLA_NOPE - MLA_ROPE), F32)], axis=1)
    head_s = jnp.concatenate([sin2, jnp.zeros((seq, LANES - MLA_ROPE), F32)], axis=1)
    cq = jnp.concatenate([head_c] * MLA_HEADS, axis=1)
    sq = jnp.concatenate([head_s] * MLA_HEADS, axis=1)
    ck = jnp.concatenate([cos2, jnp.zeros((seq, LANES - MLA_ROPE), F32)], axis=1)
    return cq, sq, ck, head_s


def _rot_half_cols(w):
    half = w.shape[1] // 2
    return jnp.concatenate([-w[:, half:], w[:, :half]], axis=1)


def _layer_weights(w_in, w_uq, w_ukv, w_out, w_up, w_down):
    d = w_in.shape[0]
    o_b = 3 * D_A
    o_c = o_b + 3 * D_B
    o_kr = o_c + MLA_Q_LORA + MLA_KV_LORA
    wa = w_in[:, :o_b].astype(BF16)
    wb = w_in[:, o_b:o_c].astype(BF16)
    w_kr = w_in[:, o_kr:o_kr + MLA_ROPE]
    zpad = jnp.zeros((d, LANES - MLA_ROPE), F32)
    wc = jnp.concatenate([w_in[:, o_c:o_kr], w_kr, zpad, _rot_half_cols(w_kr), zpad], axis=1).astype(BF16)

    hq = MLA_NOPE + MLA_ROPE
    q_cols, qr_cols, k_cols, v_cols = [], [], [], []
    zq = jnp.zeros((MLA_Q_LORA, LANES - hq), F32)
    for h in range(MLA_HEADS):
        nope = w_uq[:, h * hq:h * hq + MLA_NOPE]
        rope = w_uq[:, h * hq + MLA_NOPE:(h + 1) * hq]
        q_cols += [rope, nope, zq]
        qr_cols += [_rot_half_cols(rope), jnp.zeros((MLA_Q_LORA, LANES - MLA_ROPE), F32)]
        hk = MLA_NOPE + MLA_V
        k_cols += [jnp.zeros((MLA_KV_LORA, MLA_ROPE), F32), w_ukv[:, h * hk:h * hk + MLA_NOPE],
                   jnp.zeros((MLA_KV_LORA, LANES - hq), F32)]
        zv = jnp.zeros((MLA_KV_LORA, MLA_V), F32)
        wv = w_ukv[:, h * hk + MLA_NOPE:(h + 1) * hk]
        v_cols += [wv, zv] if h % 2 == 0 else [zv, wv]
    cat = lambda cols: jnp.concatenate(cols, axis=1).astype(BF16)
    w1 = w_out[:D_A].astype(BF16)
    w2 = w_out[D_A:D_A + D_B].astype(BF16)
    w3 = w_out[D_A + D_B:].astype(BF16)
    w_a = w_up[:, :D_FF].astype(BF16)
    w_g = w_up[:, D_FF:].astype(BF16)
    return (wa, wb, wc, cat(q_cols), cat(qr_cols), cat(k_cols), cat(v_cols), w1, w2, w3, w_a, w_g,
            w_down.astype(BF16))


def kernel(x, c, w_ada, b_ada, g_pre_mix, g_post_mix, g_pre_ffn, g_post_ffn, w_in, na_rpb, t5_table,
           mla_g_q, mla_g_kv, w_uq, w_ukv, w_out, w_up, conv_w, conv_b, w_down):
    bsz, seq, d = x.shape
    depth = w_in.shape[0]
    n = bsz * seq
    rows = seq // GRID_W
    assert d == D_MODEL and seq % (DIL_BLK * max(dd for _, dd in DIL_PATTERNS)) == 0 and rows >= NA_KH_MAX

    mod3 = _modulation(c, w_ada, b_ada).reshape(depth * bsz * N_MOD, 1, D_MODEL)
    cq, sq, ck, sk = _rope_tables(seq)
    dil_tbls = _dil_bias_tables(t5_table, seq)
    na_tbls = _na_bias_tables(na_rpb, rows)
    row2 = lambda v: v.reshape(1, -1)

    xf = x.reshape(n, d)
    for l in range(depth):
        (wa, wb, wc, wuq, wuqr, wuk, wuv, w1, w2, w3, w_a, w_g, w_d) = _layer_weights(
            w_in[l], w_uq[l], w_ukv[l], w_out[l], w_up[l], w_down[l])
        za, zb, qm, km, vm = _in_proj(xf, mod3, l, bsz, seq, row2(g_pre_mix[l]), wa, wb, wc,
                                      row2(mla_g_q[l]), row2(mla_g_kv[l]), wuq, wuqr, wuk, wuv,
                                      cq, sq, ck, sk)
        o_a = _na_attention(za.reshape(bsz, seq, 3 * D_A), na_tbls[l])
        o_b = _dil_attention(zb.reshape(bsz, seq, 3 * D_B), dil_tbls)
        o_c = _mla_attention(qm.reshape(bsz, seq, MLA_QK_W), km.reshape(bsz, seq, MLA_QK_W),
                             vm.reshape(bsz, seq, MLA_QK_W))
        xf = _out_proj(xf, o_a.reshape(n, D_A), o_b.reshape(n, D_B), o_c.reshape(n, D_C),
                       w1, w2, w3, row2(g_post_mix[l]), mod3, l, bsz, seq)
        xf = _ffn(xf, mod3, l, bsz, seq, row2(g_pre_ffn[l]), w_a, w_g, conv_w[l], row2(conv_b[l]), w_d,
                  row2(g_post_ffn[l]))
    return xf.reshape(bsz, seq, d)
```

```python
import functools
import math

import numpy as np
import jax
import jax.numpy as jnp
from jax import lax
from jax.experimental import pallas as pl
from jax.experimental.pallas import tpu as pltpu

F32 = jnp.float32
BF16 = jnp.bfloat16

D_MODEL = 1024
HEAD_DIM = 64
GRID_W = 64
EPS = 1e-6
NEG_INF = -1e30

NA_HEADS = 4
NA_KH_MAX = 8
NA_KW = 16

DIL_HEADS = 8
DIL_PATTERNS = ((128, 1), (512, 4), (2048, 16))
DIL_BLK = 128
DIL_HALF = 64

MLA_HEADS = 4
MLA_Q_LORA = 384
MLA_KV_LORA = 256
MLA_NOPE = 64
MLA_ROPE = 32
MLA_V = 64
ROPE_THETA = 10000.0

T5_BUCKETS = 32
T5_MAX_DIST = 1024

D_FF = 2816
CONV_W = 3

D_A = NA_HEADS * HEAD_DIM
D_B = DIL_HEADS * HEAD_DIM
D_C = MLA_HEADS * MLA_V
N_MOD = 6

LANES = 128
SUBLANES = 8
VMEM_LIMIT = 56 * 1024 * 1024

LOG2E = math.log2(math.e)
Q_SCALE = HEAD_DIM ** -0.5 * LOG2E
MLA_Q_SCALE = (MLA_NOPE + MLA_ROPE) ** -0.5 * LOG2E

ZC_W = MLA_Q_LORA + MLA_KV_LORA + 2 * LANES
MLA_QK_W = MLA_HEADS * LANES


def _params(n_axes):
    return pltpu.CompilerParams(dimension_semantics=("arbitrary",) * n_axes,
                                vmem_limit_bytes=VMEM_LIMIT)


def _dot(a, b):
    return jnp.dot(a, b, preferred_element_type=F32)


def _dot_nt(a, b):
    return lax.dot_general(a, b, (((1,), (1,)), ((), ())), preferred_element_type=F32)


def _rms(x, g):
    return x * lax.rsqrt(jnp.mean(x * x, axis=-1, keepdims=True) + EPS) * g


def _mod_kernel(c_ref, w_ref, b_ref, o_ref):
    ca = jax.nn.silu(c_ref[...]).astype(BF16)
    o_ref[0] = _dot(ca, w_ref[0].astype(BF16)) + b_ref[0]


def _modulation(c, w_ada, b_ada):
    depth, d, n = w_ada.shape
    bsz = c.shape[0]
    tn = 1536
    return pl.pallas_call(
        _mod_kernel,
        grid=(depth, n // tn),
        in_specs=[pl.BlockSpec((bsz, d), lambda l, j: (0, 0)),
                  pl.BlockSpec((1, d, tn), lambda l, j: (l, 0, j)),
                  pl.BlockSpec((1, 1, tn), lambda l, j: (l, 0, j))],
        out_specs=pl.BlockSpec((1, bsz, tn), lambda l, j: (l, 0, j)),
        out_shape=jax.ShapeDtypeStruct((depth, bsz, n), F32),
        compiler_params=_params(2),
        name="adaln_mod",
    )(c, w_ada, b_ada.reshape(depth, 1, n))


def _in_kernel(x_ref, g_ref, sh_ref, sc_ref, wa_ref, wb_ref, wc_ref, gq_ref, gkv_ref,
               wuq_ref, wuqr_ref, wuk_ref, wuvt_ref, cq_ref, sq_ref, ck_ref, sk_ref,
               za_ref, zb_ref, qm_ref, km_ref, vt_ref):
    x = x_ref[...]
    h = (_rms(x, g_ref[...]) * (1.0 + sc_ref[...]) + sh_ref[...]).astype(BF16)
    za_ref[:, 0:D_A] = (_dot(h, wa_ref[:, 0:D_A]) * Q_SCALE).astype(BF16)
    za_ref[:, D_A:3 * D_A] = _dot(h, wa_ref[:, D_A:3 * D_A]).astype(BF16)
    zb_ref[:, 0:D_B] = _dot(h, wb_ref[:, 0:D_B]) * Q_SCALE
    for j in range(D_B, 3 * D_B, D_B):
        zb_ref[:, j:j + D_B] = _dot(h, wb_ref[:, j:j + D_B])
    zc = _dot(h, wc_ref[...])
    c_q = zc[:, 0:MLA_Q_LORA]
    c_kv = zc[:, MLA_Q_LORA:MLA_Q_LORA + MLA_KV_LORA]
    kr_a = zc[:, MLA_Q_LORA + MLA_KV_LORA:MLA_Q_LORA + MLA_KV_LORA + LANES]
    kr_b = zc[:, MLA_Q_LORA + MLA_KV_LORA + LANES:ZC_W]
    cqn = _rms(c_q, gq_ref[...]).astype(BF16)
    q = _dot(cqn, wuq_ref[...]) * cq_ref[...] + _dot(cqn, wuqr_ref[...]) * sq_ref[...]
    qm_ref[...] = (q * MLA_Q_SCALE).astype(BF16)
    ckvn = _rms(c_kv, gkv_ref[...]).astype(BF16)
    kr = kr_a * ck_ref[...] + kr_b * sk_ref[...]
    km_ref[...] = (_dot(ckvn, wuk_ref[...]) + jnp.concatenate([kr] * MLA_HEADS, axis=1)).astype(BF16)
    vt_ref[...] = _dot_nt(wuvt_ref[...], ckvn).astype(BF16)


def _in_proj(x, mod3, l, bsz, seq, g_pre, wa, wb, wc, gq, gkv, wuq, wuqr, wuk, wuv, cq, sq, ck, sk):
    n = x.shape[0]
    tm = 512
    tps = seq // tm
    row = lambda i: (i, 0)
    const = lambda i: (0, 0)
    pos = lambda i: (i % tps, 0)
    mrow = lambda k: (lambda i: ((l * bsz + i // tps) * N_MOD + k, 0, 0))
    full = lambda a: pl.BlockSpec(a.shape, const)
    return pl.pallas_call(
        _in_kernel,
        grid=(n // tm,),
        in_specs=[pl.BlockSpec((tm, D_MODEL), row),
                  full(g_pre),
                  pl.BlockSpec((None, 1, D_MODEL), mrow(0)),
                  pl.BlockSpec((None, 1, D_MODEL), mrow(1)),
                  full(wa), full(wb), full(wc), full(gq), full(gkv),
                  full(wuq), full(wuqr), full(wuk), full(wuv),
                  pl.BlockSpec((tm, MLA_QK_W), pos), pl.BlockSpec((tm, MLA_QK_W), pos),
                  pl.BlockSpec((tm, LANES), pos), pl.BlockSpec((tm, LANES), pos)],
        out_specs=[pl.BlockSpec((tm, 3 * D_A), row),
                   pl.BlockSpec((tm, 3 * D_B), row),
                   pl.BlockSpec((tm, MLA_QK_W), row),
                   pl.BlockSpec((tm, MLA_QK_W), row),
                   pl.BlockSpec((None, D_C, tm), lambda i: (i // tps, 0, i % tps))],
        out_shape=[jax.ShapeDtypeStruct((n, 3 * D_A), BF16),
                   jax.ShapeDtypeStruct((n, 3 * D_B), F32),
                   jax.ShapeDtypeStruct((n, MLA_QK_W), BF16),
                   jax.ShapeDtypeStruct((n, MLA_QK_W), BF16),
                   jax.ShapeDtypeStruct((bsz, D_C, seq), BF16)],
        compiler_params=_params(1),
        name="in_proj",
    )(x, g_pre, mod3, mod3, wa, wb, wc, gq, gkv, wuq, wuqr, wuk, wuv, cq, sq, ck, sk)


NA_ROWS_PER_STEP = 8
NA_GROUP = 4


def _na_kernel(q_ref, k_ref, v_ref, tbl_ref, o_ref, *, rows):
    i = pl.program_id(1)
    lane = lax.broadcasted_iota(jnp.int32, (1, LANES), 1)
    low = lane < HEAD_DIM
    kh = NA_KH_MAX

    sels = (low, jnp.logical_not(low))

    def row_group(j0, carry):
        items = []
        scores = []
        for g in range(NA_GROUP):
            j = j0 * NA_GROUP + g
            r = i * NA_ROWS_PER_STEP + j
            r0 = jnp.clip(r - kh // 2, 0, rows - kh)
            variant = r - r0
            qs = pl.multiple_of(j * GRID_W, GRID_W)
            ks = pl.multiple_of(r0 * GRID_W, GRID_W)
            for pair in range(NA_HEADS // 2):
                cols = slice(pair * LANES, (pair + 1) * LANES)
                q2 = q_ref[pl.ds(qs, GRID_W), cols]
                k2 = k_ref[pl.ds(ks, kh * GRID_W), cols]
                items.append((qs, ks, cols))
                for w in range(2):
                    qm = jnp.where(sels[w], q2, jnp.zeros_like(q2))
                    scores.append(_dot_nt(qm, k2) + tbl_ref[variant, 2 * pair + w])
        probs = []
        for s in scores:
            m = jnp.max(s, axis=-1, keepdims=True)
            p = jnp.exp2(s - m)
            probs.append((p.astype(BF16), jnp.sum(p, axis=-1, keepdims=True)))
        for idx, (qs, ks, cols) in enumerate(items):
            v2 = v_ref[pl.ds(ks, kh * GRID_W), cols]
            out = jnp.zeros((GRID_W, LANES), F32)
            for w in range(2):
                p, den = probs[2 * idx + w]
                out = jnp.where(sels[w], _dot(p, v2) / den, out)
            o_ref[pl.ds(qs, GRID_W), cols] = out.astype(BF16)
        return carry

    lax.fori_loop(0, NA_ROWS_PER_STEP // NA_GROUP, row_group, 0)


def _na_attention(za3, tbl):
    bsz, seq, _ = za3.shape
    rows = seq // GRID_W
    tq = NA_ROWS_PER_STEP * GRID_W
    return pl.pallas_call(
        functools.partial(_na_kernel, rows=rows),
        grid=(bsz, seq // tq),
        in_specs=[pl.BlockSpec((None, tq, D_A), lambda b, i: (b, i, 0)),
                  pl.BlockSpec((None, seq, D_A), lambda b, i: (b, 0, 1)),
                  pl.BlockSpec((None, seq, D_A), lambda b, i: (b, 0, 2)),
                  pl.BlockSpec(tbl.shape, lambda b, i: (0, 0, 0, 0))],
        out_specs=pl.BlockSpec((None, tq, D_A), lambda b, i: (b, i, 0)),
        out_shape=jax.ShapeDtypeStruct((bsz, seq, D_A), BF16),
        compiler_params=_params(2),
        name="na_attn",
    )(za3, za3, za3, tbl)


def _na_bias_tables(rpb, rows):
    kh = min(NA_KH_MAX, rows)
    col = np.arange(GRID_W)
    c_start = np.clip(col - NA_KW // 2, 0, GRID_W - NA_KW)
    valid = (col[None, :] >= c_start[:, None]) & (col[None, :] < c_start[:, None] + NA_KW)
    d_col = col[None, :] - col[:, None] + (NA_KW - 1)
    onehot = (d_col[:, :, None] == np.arange(2 * NA_KW - 1)[None, None, :]) & valid[:, :, None]
    by_row = jnp.stack([rpb[:, :, NA_KH_MAX - 1 - v:NA_KH_MAX - 1 - v + kh, :] for v in range(kh)], axis=1)
    t = jnp.einsum("lvhib,cdb->lvhcid", by_row.astype(F32), jnp.asarray(onehot, F32),
                   precision=lax.Precision.HIGHEST)
    t = jnp.where(valid[None, None, None, :, None, :], t * LOG2E, NEG_INF)
    return t.reshape(rpb.shape[0], kh, NA_HEADS, GRID_W, kh * GRID_W)


DIL_PAD = DIL_HALF * max(d for _, d in DIL_PATTERNS)
DIL_KB = DIL_BLK + 2 * DIL_HALF
DIL_GROUP = 8
DIL_GROUP_WHOLE = 2
DIL_MERGE_ROWS = 512


def _dil_whole_class(seq, d):
    return seq // d <= 2 * DIL_BLK


def _dil_kernel(q_ref, k_ref, v_ref, *rest, seq):
    npat = len(DIL_PATTERNS)
    tbl_refs = rest[:npat]
    o_ref, kp_ref, vp_ref = rest[npat:npat + 3]
    stats = rest[npat + 3:]
    acc_refs, m_refs, l_refs = stats[:npat], stats[npat:2 * npat], stats[2 * npat:]

    zeros_pad = jnp.zeros((DIL_PAD, LANES), F32)
    kp_ref[0:DIL_PAD, :] = zeros_pad
    kp_ref[DIL_PAD + seq:DIL_PAD + seq + DIL_PAD, :] = zeros_pad
    vp_ref[0:DIL_PAD, :] = zeros_pad
    vp_ref[DIL_PAD + seq:DIL_PAD + seq + DIL_PAD, :] = zeros_pad
    kp_ref[DIL_PAD:DIL_PAD + seq, :] = k_ref[...]
    vp_ref[DIL_PAD:DIL_PAD + seq, :] = v_ref[...]

    lane = lax.broadcasted_iota(jnp.int32, (1, LANES), 1)
    low = lane < HEAD_DIM

    def attend_group(items, acc_ref, m_ref, l_ref):
        sels = (low, jnp.logical_not(low))
        scores = []
        for _, q2, k2, _, bias in items:
            for w in range(2):
                qm = jnp.where(sels[w], q2, jnp.zeros_like(q2))
                scores.append(_dot_nt(qm, k2) + bias(w))
        probs = []
        for s in scores:
            m = jnp.max(s, axis=-1, keepdims=True)
            p = jnp.exp2(s - m)
            probs.append((p.astype(BF16), m, jnp.sum(p, axis=-1, keepdims=True)))
        for idx, (rows, q2, _, v2, _) in enumerate(items):
            nq = q2.shape[0]
            m_b = jnp.zeros((nq, LANES), F32)
            l_b = jnp.zeros((nq, LANES), F32)
            o_b = jnp.zeros((nq, LANES), F32)
            for w in range(2):
                p, m, den = probs[2 * idx + w]
                o = _dot(p, v2)
                m_b = jnp.where(sels[w], m, m_b)
                l_b = jnp.where(sels[w], den, l_b)
                o_b = jnp.where(sels[w], o, o_b)
            acc_ref[rows, :] = o_b
            m_ref[rows, :] = m_b
            l_ref[rows, :] = l_b

    for pat, (_, d) in enumerate(DIL_PATTERNS):
        length = seq // d
        tbl_ref, acc_ref, m_ref, l_ref = tbl_refs[pat], acc_refs[pat], m_refs[pat], l_refs[pat]

        if _dil_whole_class(seq, d):
            group = min(DIL_GROUP_WHOLE, d)

            def whole(g0, carry, d=d, length=length, group=group, tbl_ref=tbl_ref,
                      acc_ref=acc_ref, m_ref=m_ref, l_ref=l_ref):
                items = []
                for g in range(group):
                    r = g0 * group + g
                    rows = pl.ds(r, length, stride=d) if d > 1 else pl.ds(r, length)
                    q2 = q_ref[rows, :].astype(BF16)
                    items.append((rows, q2, k_ref[rows, :].astype(BF16), v_ref[rows, :].astype(BF16),
                                  lambda w: tbl_ref[w]))
                attend_group(items, acc_ref, m_ref, l_ref)
                return carry

            lax.fori_loop(0, d // group, whole, 0)
        else:
            nblk = length // DIL_BLK
            group = min(DIL_GROUP, nblk)

            def block(t0, carry, d=d, nblk=nblk, group=group, tbl_ref=tbl_ref,
                      acc_ref=acc_ref, m_ref=m_ref, l_ref=l_ref):
                items = []
                for g in range(group):
                    t = t0 * group + g
                    r = t // nblk
                    n = t % nblk
                    variant = jnp.where(n == 0, 1, 0) + jnp.where(n == nblk - 1, 2, 0)
                    q_start = r + d * DIL_BLK * n
                    k_start = DIL_PAD + r + d * (DIL_BLK * n - DIL_HALF)
                    if d == 1:
                        rows_q = pl.ds(q_start, DIL_BLK)
                        rows_k = pl.ds(k_start, DIL_KB)
                    else:
                        rows_q = pl.ds(q_start, DIL_BLK, stride=d)
                        rows_k = pl.ds(k_start, DIL_KB, stride=d)
                    q2 = q_ref[rows_q, :].astype(BF16)
                    items.append((rows_q, q2, kp_ref[rows_k, :].astype(BF16), vp_ref[rows_k, :].astype(BF16),
                                  lambda w, variant=variant: tbl_ref[variant, w]))
                attend_group(items, acc_ref, m_ref, l_ref)
                return carry

            lax.fori_loop(0, d * nblk // group, block, 0)

    def merge(c, carry):
        rows = pl.ds(pl.multiple_of(c * DIL_MERGE_ROWS, DIL_MERGE_ROWS), DIL_MERGE_ROWS)
        ms = [m_ref[rows, :] for m_ref in m_refs]
        m_all = functools.reduce(jnp.maximum, ms)
        num = jnp.zeros((DIL_MERGE_ROWS, LANES), F32)
        den = jnp.zeros((DIL_MERGE_ROWS, LANES), F32)
        for m_i, acc_ref, l_ref in zip(ms, acc_refs, l_refs):
            e = jnp.exp2(m_i - m_all)
            num = num + acc_ref[rows, :] * e
            den = den + l_ref[rows, :] * e
        o_ref[rows, :] = (num / den).astype(BF16)
        return carry

    lax.fori_loop(0, seq // DIL_MERGE_ROWS, merge, 0)


def _dil_attention(zb3, tbls):
    bsz, seq, _ = zb3.shape
    npair = DIL_HEADS // 2
    blk = lambda off: pl.BlockSpec((None, seq, LANES), lambda b, p: (b, 0, off + p))
    tbl_specs = []
    for t in tbls:
        if t.ndim == 3:
            tbl_specs.append(pl.BlockSpec((2,) + t.shape[1:], lambda b, p: (p, 0, 0)))
        else:
            tbl_specs.append(pl.BlockSpec((t.shape[0], 2) + t.shape[2:], lambda b, p: (0, p, 0, 0)))
    stat = pltpu.VMEM((seq, LANES), F32)
    return pl.pallas_call(
        functools.partial(_dil_kernel, seq=seq),
        grid=(bsz, npair),
        in_specs=[blk(0), blk(npair), blk(2 * npair)] + tbl_specs,
        out_specs=pl.BlockSpec((None, seq, LANES), lambda b, p: (b, 0, p)),
        out_shape=jax.ShapeDtypeStruct((bsz, seq, D_B), BF16),
        scratch_shapes=[pltpu.VMEM((seq + 2 * DIL_PAD, LANES), F32),
                        pltpu.VMEM((seq + 2 * DIL_PAD, LANES), F32)] + [stat] * (3 * len(DIL_PATTERNS)),
        compiler_params=_params(2),
        name="dil_attn",
    )(zb3, zb3, zb3, *tbls)


def _t5_bucket(rel):
    nb = T5_BUCKETS // 2
    max_exact = nb // 2
    n = np.abs(rel)
    large = max_exact + (np.log(np.maximum(n, 1) / max_exact)
                         / math.log(T5_MAX_DIST / max_exact) * (nb - max_exact)).astype(np.int64)
    large = np.minimum(large, nb - 1)
    return (np.where(rel > 0, nb, 0) + np.where(n < max_exact, n, large)).astype(np.int32)


def _dil_bias_tables(t5_table, seq):
    tables = []
    for _, d in DIL_PATTERNS:
        if _dil_whole_class(seq, d):
            pos = np.arange(seq // d)
            rel = pos[None, :] - pos[:, None]
            valid = np.abs(rel) <= DIL_HALF
            bias = jnp.transpose(t5_table[_t5_bucket(rel * d)], (2, 0, 1)).astype(F32)
            tables.append(jnp.where(valid[None], bias * LOG2E, NEG_INF))
        else:
            a = np.arange(DIL_BLK)[:, None]
            j = np.arange(DIL_KB)[None, :]
            rel = j - DIL_HALF - a
            band = np.abs(rel) <= DIL_HALF
            not_before = np.broadcast_to(j >= DIL_HALF, band.shape)
            not_after = np.broadcast_to(j < DIL_BLK + DIL_HALF, band.shape)
            valid = np.stack([band, band & not_before, band & not_after, band & not_before & not_after])
            bias = jnp.transpose(t5_table[_t5_bucket(rel * d)], (2, 0, 1)).astype(F32)
            tables.append(jnp.where(valid[:, None], bias[None] * LOG2E, NEG_INF))
    return tables


MLA_TQ = 512


def _mla_kernel(q_ref, k_ref, vt_ref, o_ref):
    head_cols = [slice(h * LANES, (h + 1) * LANES) for h in range(MLA_HEADS)]
    scores = lambda h: _dot_nt(k_ref[:, head_cols[h]], q_ref[:, head_cols[h]])
    outs = []
    s_next = scores(0)
    for h in range(MLA_HEADS):
        s = s_next
        if h + 1 < MLA_HEADS:
            s_next = scores(h + 1)
        m = jnp.max(s, axis=0, keepdims=True)
        p = jnp.exp2(s - m)
        den = jnp.sum(p, axis=0, keepdims=True)
        outs.append(_dot(vt_ref[h * MLA_V:(h + 1) * MLA_V, :], p.astype(BF16)) / den)
    o_ref[...] = jnp.concatenate(outs, axis=0).T.astype(BF16)


def _mla_attention(qm3, km3, vt3):
    bsz, seq, _ = qm3.shape
    return pl.pallas_call(
        _mla_kernel,
        grid=(bsz, seq // MLA_TQ),
        in_specs=[pl.BlockSpec((None, MLA_TQ, MLA_QK_W), lambda b, i: (b, i, 0)),
                  pl.BlockSpec((None, seq, MLA_QK_W), lambda b, i: (b, 0, 0), pipeline_mode=pl.Buffered(1)),
                  pl.BlockSpec((None, D_C, seq), lambda b, i: (b, 0, 0), pipeline_mode=pl.Buffered(1))],
        out_specs=pl.BlockSpec((None, MLA_TQ, D_C), lambda b, i: (b, i, 0)),
        out_shape=jax.ShapeDtypeStruct((bsz, seq, D_C), BF16),
        compiler_params=_params(2),
        name="mla_attn",
    )(qm3, km3, vt3)


def _out_kernel(x_ref, oa_ref, ob_ref, oc_ref, w1_ref, w2_ref, w3_ref, g_ref, gate_ref, o_ref):
    y = _dot(oa_ref[...], w1_ref[...]) + _dot(ob_ref[...], w2_ref[...]) + _dot(oc_ref[...], w3_ref[...])
    o_ref[...] = x_ref[...] + gate_ref[...] * _rms(y, g_ref[...])


def _out_proj(x, oa, ob, oc, w1, w2, w3, g_post, mod3, l, bsz, seq):
    n = x.shape[0]
    tm = 512
    tps = seq // tm
    row = lambda i: (i, 0)
    const = lambda i: (0, 0)
    full = lambda a: pl.BlockSpec(a.shape, const)
    return pl.pallas_call(
        _out_kernel,
        grid=(n // tm,),
        in_specs=[pl.BlockSpec((tm, D_MODEL), row),
                  pl.BlockSpec((tm, D_A), row), pl.BlockSpec((tm, D_B), row), pl.BlockSpec((tm, D_C), row),
                  full(w1), full(w2), full(w3), full(g_post),
                  pl.BlockSpec((None, 1, D_MODEL), lambda i: ((l * bsz + i // tps) * N_MOD + 2, 0, 0))],
        out_specs=pl.BlockSpec((tm, D_MODEL), row),
        out_shape=jax.ShapeDtypeStruct((n, D_MODEL), F32),
        compiler_params=_params(1),
        name="out_proj",
    )(x, oa, ob, oc, w1, w2, w3, g_post, mod3)


FFN_TM = 512
FFN_CHUNK = 256


def _ffn_kernel(x_ref, xp_ref, xn_ref, g_ref, sh_ref, sc_ref, wa_ref, wg_ref, cw_ref, cb_ref, wd_ref,
                gpost_ref, gate_ref, o_ref, u_ref, *, tiles_per_seq):
    i = pl.program_id(0)
    has_prev = (i % tiles_per_seq) != 0
    has_next = (i % tiles_per_seq) != tiles_per_seq - 1
    g = g_ref[...]
    sc = 1.0 + sc_ref[...]
    sh = sh_ref[...]
    x = x_ref[...]
    pre = lambda t: _rms(t, g) * sc + sh
    h_prev = jnp.where(has_prev, pre(xp_ref[...]), 0.0)
    h_next = jnp.where(has_next, pre(xn_ref[...]), 0.0)
    h_mid = pre(x)
    h = h_mid.astype(BF16)
    h_ext = jnp.concatenate([h_prev, h_mid, h_next], axis=0).astype(BF16)
    tm = x.shape[0]
    ext = tm + 2 * SUBLANES
    mid = slice(SUBLANES, SUBLANES + tm)
    for c0 in range(0, D_FF, FFN_CHUNK):
        cols = slice(c0, c0 + FFN_CHUNK)
        a = _dot(h, wa_ref[:, cols])
        ge = _dot(h_ext, wg_ref[:, cols])
        g_prev = pltpu.roll(ge, 1, axis=0)[mid]
        g_next = pltpu.roll(ge, ext - 1, axis=0)[mid]
        gc = cb_ref[:, cols] + g_prev * cw_ref[0:1, cols]
        gc = gc + ge[mid] * cw_ref[1:2, cols]
        gc = gc + g_next * cw_ref[2:3, cols]
        u_ref[:, cols] = (jax.nn.gelu(gc) * a).astype(BF16)
    y = _dot(u_ref[...], wd_ref[...])
    o_ref[...] = x + gate_ref[...] * _rms(y, gpost_ref[...])


def _ffn(x, mod3, l, bsz, seq, g_pre, wa, wg, cw, cb, wd, g_post):
    n = x.shape[0]
    tm = FFN_TM
    tps = seq // tm
    r8 = tm // SUBLANES
    last8 = n // SUBLANES - 1
    const = lambda i: (0, 0)
    full = lambda a: pl.BlockSpec(a.shape, const)
    weight = lambda a: pl.BlockSpec(a.shape, const, pipeline_mode=pl.Buffered(1))
    mrow = lambda k: (lambda i: ((l * bsz + i // tps) * N_MOD + k, 0, 0))
    return pl.pallas_call(
        functools.partial(_ffn_kernel, tiles_per_seq=tps),
        grid=(n // tm,),
        in_specs=[pl.BlockSpec((tm, D_MODEL), lambda i: (i, 0)),
                  pl.BlockSpec((SUBLANES, D_MODEL), lambda i: (jnp.maximum(i * r8 - 1, 0), 0)),
                  pl.BlockSpec((SUBLANES, D_MODEL), lambda i: (jnp.minimum((i + 1) * r8, last8), 0)),
                  full(g_pre),
                  pl.BlockSpec((None, 1, D_MODEL), mrow(3)),
                  pl.BlockSpec((None, 1, D_MODEL), mrow(4)),
                  weight(wa), weight(wg), full(cw), full(cb), weight(wd), full(g_post),
                  pl.BlockSpec((None, 1, D_MODEL), mrow(5))],
        out_specs=pl.BlockSpec((tm, D_MODEL), lambda i: (i, 0)),
        out_shape=jax.ShapeDtypeStruct((n, D_MODEL), F32),
        scratch_shapes=[pltpu.VMEM((tm, D_FF), BF16)],
        compiler_params=_params(1),
        name="conv_ffn",
    )(x, x, x, g_pre, mod3, mod3, wa, wg, cw, cb, wd, g_post, mod3)


def _rope_tables(seq):
    inv_freq = jnp.asarray(ROPE_THETA ** (-np.arange(0, MLA_ROPE, 2, dtype=np.float32) / MLA_ROPE), F32)
    ang = jnp.arange(seq, dtype=F32)[:, None] * inv_freq[None, :]
    cos, sin = jnp.cos(ang), jnp.sin(ang)
    cos2 = jnp.concatenate([cos, cos], axis=1)
    sin2 = jnp.concatenate([sin, sin], axis=1)
    head_c = jnp.concatenate([cos2, jnp.ones((seq, MLA_NOPE), F32), jnp.zeros((seq, LANES - MLA_NOPE - MLA_ROPE), F32)], axis=1)
    head_s = jnp.concatenate([sin2, jnp.zeros((seq, LANES - MLA_ROPE), F32)], axis=1)
    cq = jnp.concatenate([head_c] * MLA_HEADS, axis=1)
    sq = jnp.concatenate([head_s] * MLA_HEADS, axis=1)
    ck = jnp.concatenate([cos2, jnp.zeros((seq, LANES - MLA_ROPE), F32)], axis=1)
    return cq, sq, ck, head_s


def _rot_half_cols(w):
    half = w.shape[1] // 2
    return jnp.concatenate([-w[:, half:], w[:, :half]], axis=1)


def _layer_weights(w_in, w_uq, w_ukv, w_out, w_up, w_down):
    d = w_in.shape[0]
    o_b = 3 * D_A
    o_c = o_b + 3 * D_B
    o_kr = o_c + MLA_Q_LORA + MLA_KV_LORA
    wa = w_in[:, :o_b].astype(BF16)
    wb = w_in[:, o_b:o_c].astype(BF16)
    w_kr = w_in[:, o_kr:o_kr + MLA_ROPE]
    zpad = jnp.zeros((d, LANES - MLA_ROPE), F32)
    wc = jnp.concatenate([w_in[:, o_c:o_kr], w_kr, zpad, _rot_half_cols(w_kr), zpad], axis=1).astype(BF16)

    hq = MLA_NOPE + MLA_ROPE
    q_cols, qr_cols, k_cols, v_cols = [], [], [], []
    zq = jnp.zeros((MLA_Q_LORA, LANES - hq), F32)
    for h in range(MLA_HEADS):
        nope = w_uq[:, h * hq:h * hq + MLA_NOPE]
        rope = w_uq[:, h * hq + MLA_NOPE:(h + 1) * hq]
        q_cols += [rope, nope, zq]
        qr_cols += [_rot_half_cols(rope), jnp.zeros((MLA_Q_LORA, LANES - MLA_ROPE), F32)]
        hk = MLA_NOPE + MLA_V
        k_cols += [jnp.zeros((MLA_KV_LORA, MLA_ROPE), F32), w_ukv[:, h * hk:h * hk + MLA_NOPE],
                   jnp.zeros((MLA_KV_LORA, LANES - hq), F32)]
        v_cols += [w_ukv[:, h * hk + MLA_NOPE:(h + 1) * hk]]
    cat = lambda cols: jnp.concatenate(cols, axis=1).astype(BF16)
    w1 = w_out[:D_A].astype(BF16)
    w2 = w_out[D_A:D_A + D_B].astype(BF16)
    w3 = w_out[D_A + D_B:].astype(BF16)
    w_a = w_up[:, :D_FF].astype(BF16)
    w_g = w_up[:, D_FF:].astype(BF16)
    return (wa, wb, wc, cat(q_cols), cat(qr_cols), cat(k_cols), cat(v_cols).T, w1, w2, w3, w_a, w_g,
            w_down.astype(BF16))


def kernel(x, c, w_ada, b_ada, g_pre_mix, g_post_mix, g_pre_ffn, g_post_ffn, w_in, na_rpb, t5_table,
           mla_g_q, mla_g_kv, w_uq, w_ukv, w_out, w_up, conv_w, conv_b, w_down):
    bsz, seq, d = x.shape
    depth = w_in.shape[0]
    n = bsz * seq
    rows = seq // GRID_W
    assert d == D_MODEL and seq % (DIL_BLK * max(dd for _, dd in DIL_PATTERNS)) == 0 and rows >= NA_KH_MAX

    mod3 = _modulation(c, w_ada, b_ada).reshape(depth * bsz * N_MOD, 1, D_MODEL)
    cq, sq, ck, sk = _rope_tables(seq)
    dil_tbls = _dil_bias_tables(t5_table, seq)
    na_tbls = _na_bias_tables(na_rpb, rows)
    row2 = lambda v: v.reshape(1, -1)

    xf = x.reshape(n, d)
    for l in range(depth):
        (wa, wb, wc, wuq, wuqr, wuk, wuv, w1, w2, w3, w_a, w_g, w_d) = _layer_weights(
            w_in[l], w_uq[l], w_ukv[l], w_out[l], w_up[l], w_down[l])
        za, zb, qm, km, vt = _in_proj(xf, mod3, l, bsz, seq, row2(g_pre_mix[l]), wa, wb, wc,
                                      row2(mla_g_q[l]), row2(mla_g_kv[l]), wuq, wuqr, wuk, wuv,
                                      cq, sq, ck, sk)
        o_a = _na_attention(za.reshape(bsz, seq, 3 * D_A), na_tbls[l])
        o_b = _dil_attention(zb.reshape(bsz, seq, 3 * D_B), dil_tbls)
        o_c = _mla_attention(qm.reshape(bsz, seq, MLA_QK_W), km.reshape(bsz, seq, MLA_QK_W), vt)
        xf = _out_proj(xf, o_a.reshape(n, D_A), o_b.reshape(n, D_B), o_c.reshape(n, D_C),
                       w1, w2, w3, row2(g_post_mix[l]), mod3, l, bsz, seq)
        xf = _ffn(xf, mod3, l, bsz, seq, row2(g_pre_ffn[l]), w_a, w_g, conv_w[l], row2(conv_b[l]), w_d,
                  row2(g_post_ffn[l]))
    return xf.reshape(bsz, seq, d)
```

```python
import functools
import math

import numpy as np
import jax
import jax.numpy as jnp
from jax import lax
from jax.experimental import pallas as pl
from jax.experimental.pallas import tpu as pltpu

F32 = jnp.float32
BF16 = jnp.bfloat16

D_MODEL = 1024
HEAD_DIM = 64
GRID_W = 64
EPS = 1e-6
NEG_INF = -1e30

NA_HEADS = 4
NA_KH_MAX = 8
NA_KW = 16

DIL_HEADS = 8
DIL_PATTERNS = ((128, 1), (512, 4), (2048, 16))
DIL_BLK = 128
DIL_HALF = 64

MLA_HEADS = 4
MLA_Q_LORA = 384
MLA_KV_LORA = 256
MLA_NOPE = 64
MLA_ROPE = 32
MLA_V = 64
ROPE_THETA = 10000.0

T5_BUCKETS = 32
T5_MAX_DIST = 1024

D_FF = 2816
CONV_W = 3

D_A = NA_HEADS * HEAD_DIM
D_B = DIL_HEADS * HEAD_DIM
D_C = MLA_HEADS * MLA_V
N_MOD = 6

LANES = 128
SUBLANES = 8
VMEM_LIMIT = 56 * 1024 * 1024

LOG2E = math.log2(math.e)
Q_SCALE = HEAD_DIM ** -0.5 * LOG2E
MLA_Q_SCALE = (MLA_NOPE + MLA_ROPE) ** -0.5 * LOG2E

ZC_W = MLA_Q_LORA + MLA_KV_LORA + 2 * LANES
MLA_QK_W = MLA_HEADS * LANES


def _params(n_axes):
    return pltpu.CompilerParams(dimension_semantics=("arbitrary",) * n_axes,
                                vmem_limit_bytes=VMEM_LIMIT)


def _dot(a, b):
    return jnp.dot(a, b, preferred_element_type=F32)


def _dot_nt(a, b):
    return lax.dot_general(a, b, (((1,), (1,)), ((), ())), preferred_element_type=F32)


def _rms(x, g):
    return x * lax.rsqrt(jnp.mean(x * x, axis=-1, keepdims=True) + EPS) * g


def _mod_kernel(c_ref, w_ref, b_ref, o_ref):
    ca = jax.nn.silu(c_ref[...]).astype(BF16)
    o_ref[0] = _dot(ca, w_ref[0].astype(BF16)) + b_ref[0]


def _modulation(c, w_ada, b_ada):
    depth, d, n = w_ada.shape
    bsz = c.shape[0]
    tn = 1536
    return pl.pallas_call(
        _mod_kernel,
        grid=(depth, n // tn),
        in_specs=[pl.BlockSpec((bsz, d), lambda l, j: (0, 0)),
                  pl.BlockSpec((1, d, tn), lambda l, j: (l, 0, j)),
                  pl.BlockSpec((1, 1, tn), lambda l, j: (l, 0, j))],
        out_specs=pl.BlockSpec((1, bsz, tn), lambda l, j: (l, 0, j)),
        out_shape=jax.ShapeDtypeStruct((depth, bsz, n), F32),
        compiler_params=_params(2),
        name="adaln_mod",
    )(c, w_ada, b_ada.reshape(depth, 1, n))


def _in_kernel(x_ref, g_ref, sh_ref, sc_ref, wa_ref, wb_ref, wc_ref, gq_ref, gkv_ref,
               wuq_ref, wuqr_ref, wuk_ref, wuvt_ref, cq_ref, sq_ref, ck_ref, sk_ref,
               za_ref, zb_ref, qm_ref, km_ref, vt_ref):
    x = x_ref[...]
    h = (_rms(x, g_ref[...]) * (1.0 + sc_ref[...]) + sh_ref[...]).astype(BF16)
    za_ref[:, 0:D_A] = (_dot(h, wa_ref[:, 0:D_A]) * Q_SCALE).astype(BF16)
    za_ref[:, D_A:3 * D_A] = _dot(h, wa_ref[:, D_A:3 * D_A]).astype(BF16)
    zb_ref[:, 0:D_B] = _dot(h, wb_ref[:, 0:D_B]) * Q_SCALE
    for j in range(D_B, 3 * D_B, D_B):
        zb_ref[:, j:j + D_B] = _dot(h, wb_ref[:, j:j + D_B])
    zc = _dot(h, wc_ref[...])
    c_q = zc[:, 0:MLA_Q_LORA]
    c_kv = zc[:, MLA_Q_LORA:MLA_Q_LORA + MLA_KV_LORA]
    kr_a = zc[:, MLA_Q_LORA + MLA_KV_LORA:MLA_Q_LORA + MLA_KV_LORA + LANES]
    kr_b = zc[:, MLA_Q_LORA + MLA_KV_LORA + LANES:ZC_W]
    cqn = _rms(c_q, gq_ref[...]).astype(BF16)
    q = _dot(cqn, wuq_ref[...]) * cq_ref[...] + _dot(cqn, wuqr_ref[...]) * sq_ref[...]
    qm_ref[...] = (q * MLA_Q_SCALE).astype(BF16)
    ckvn = _rms(c_kv, gkv_ref[...]).astype(BF16)
    kr = kr_a * ck_ref[...] + kr_b * sk_ref[...]
    km_ref[...] = (_dot(ckvn, wuk_ref[...]) + jnp.concatenate([kr] * MLA_HEADS, axis=1)).astype(BF16)
    vrow = lax.broadcasted_iota(jnp.int32, (MLA_HEADS * MLA_VT_ROWS, 1), 0)
    ones_rows = functools.reduce(jnp.logical_or, [vrow == h * MLA_VT_ROWS + MLA_V for h in range(MLA_HEADS)])
    vt = _dot_nt(wuvt_ref[...], ckvn) + jnp.where(ones_rows, 1.0, 0.0)
    vt_ref[...] = vt.astype(BF16)


def _in_proj(x, mod3, l, bsz, seq, g_pre, wa, wb, wc, gq, gkv, wuq, wuqr, wuk, wuv, cq, sq, ck, sk):
    n = x.shape[0]
    tm = 512
    tps = seq // tm
    row = lambda i: (i, 0)
    const = lambda i: (0, 0)
    pos = lambda i: (i % tps, 0)
    mrow = lambda k: (lambda i: ((l * bsz + i // tps) * N_MOD + k, 0, 0))
    full = lambda a: pl.BlockSpec(a.shape, const)
    return pl.pallas_call(
        _in_kernel,
        grid=(n // tm,),
        in_specs=[pl.BlockSpec((tm, D_MODEL), row),
                  full(g_pre),
                  pl.BlockSpec((None, 1, D_MODEL), mrow(0)),
                  pl.BlockSpec((None, 1, D_MODEL), mrow(1)),
                  full(wa), full(wb), full(wc), full(gq), full(gkv),
                  full(wuq), full(wuqr), full(wuk), full(wuv),
                  pl.BlockSpec((tm, MLA_QK_W), pos), pl.BlockSpec((tm, MLA_QK_W), pos),
                  pl.BlockSpec((tm, LANES), pos), pl.BlockSpec((tm, LANES), pos)],
        out_specs=[pl.BlockSpec((tm, 3 * D_A), row),
                   pl.BlockSpec((tm, 3 * D_B), row),
                   pl.BlockSpec((tm, MLA_QK_W), row),
                   pl.BlockSpec((tm, MLA_QK_W), row),
                   pl.BlockSpec((None, MLA_HEADS * MLA_VT_ROWS, tm), lambda i: (i // tps, 0, i % tps))],
        out_shape=[jax.ShapeDtypeStruct((n, 3 * D_A), BF16),
                   jax.ShapeDtypeStruct((n, 3 * D_B), F32),
                   jax.ShapeDtypeStruct((n, MLA_QK_W), BF16),
                   jax.ShapeDtypeStruct((n, MLA_QK_W), BF16),
                   jax.ShapeDtypeStruct((bsz, MLA_HEADS * MLA_VT_ROWS, seq), BF16)],
        compiler_params=_params(1),
        name="in_proj",
    )(x, g_pre, mod3, mod3, wa, wb, wc, gq, gkv, wuq, wuqr, wuk, wuv, cq, sq, ck, sk)


NA_ROWS_PER_STEP = 8
NA_GROUP = 4


def _na_kernel(q_ref, k_ref, v_ref, tbl_ref, o_ref, *, rows):
    i = pl.program_id(1)
    lane = lax.broadcasted_iota(jnp.int32, (1, LANES), 1)
    low = lane < HEAD_DIM
    kh = NA_KH_MAX

    sels = (low, jnp.logical_not(low))

    def row_group(j0, carry):
        items = []
        scores = []
        for g in range(NA_GROUP):
            j = j0 * NA_GROUP + g
            r = i * NA_ROWS_PER_STEP + j
            r0 = jnp.clip(r - kh // 2, 0, rows - kh)
            variant = r - r0
            qs = pl.multiple_of(j * GRID_W, GRID_W)
            ks = pl.multiple_of(r0 * GRID_W, GRID_W)
            for pair in range(NA_HEADS // 2):
                cols = slice(pair * LANES, (pair + 1) * LANES)
                q2 = q_ref[pl.ds(qs, GRID_W), cols]
                k2 = k_ref[pl.ds(ks, kh * GRID_W), cols]
                items.append((qs, ks, cols))
                for w in range(2):
                    qm = jnp.where(sels[w], q2, jnp.zeros_like(q2))
                    scores.append(_dot_nt(qm, k2) + tbl_ref[variant, 2 * pair + w])
        probs = []
        for s in scores:
            m = jnp.max(s, axis=-1, keepdims=True)
            p = jnp.exp2(s - m)
            probs.append((p.astype(BF16), jnp.sum(p, axis=-1, keepdims=True)))
        for idx, (qs, ks, cols) in enumerate(items):
            v2 = v_ref[pl.ds(ks, kh * GRID_W), cols]
            out = jnp.zeros((GRID_W, LANES), F32)
            for w in range(2):
                p, den = probs[2 * idx + w]
                out = jnp.where(sels[w], _dot(p, v2) / den, out)
            o_ref[pl.ds(qs, GRID_W), cols] = out.astype(BF16)
        return carry

    lax.fori_loop(0, NA_ROWS_PER_STEP // NA_GROUP, row_group, 0)


def _na_attention(za3, tbl):
    bsz, seq, _ = za3.shape
    rows = seq // GRID_W
    tq = NA_ROWS_PER_STEP * GRID_W
    return pl.pallas_call(
        functools.partial(_na_kernel, rows=rows),
        grid=(bsz, seq // tq),
        in_specs=[pl.BlockSpec((None, tq, D_A), lambda b, i: (b, i, 0)),
                  pl.BlockSpec((None, seq, D_A), lambda b, i: (b, 0, 1)),
                  pl.BlockSpec((None, seq, D_A), lambda b, i: (b, 0, 2)),
                  pl.BlockSpec(tbl.shape, lambda b, i: (0, 0, 0, 0))],
        out_specs=pl.BlockSpec((None, tq, D_A), lambda b, i: (b, i, 0)),
        out_shape=jax.ShapeDtypeStruct((bsz, seq, D_A), BF16),
        compiler_params=_params(2),
        name="na_attn",
    )(za3, za3, za3, tbl)


def _na_bias_tables(rpb, rows):
    kh = min(NA_KH_MAX, rows)
    col = np.arange(GRID_W)
    c_start = np.clip(col - NA_KW // 2, 0, GRID_W - NA_KW)
    valid = (col[None, :] >= c_start[:, None]) & (col[None, :] < c_start[:, None] + NA_KW)
    d_col = col[None, :] - col[:, None] + (NA_KW - 1)
    onehot = (d_col[:, :, None] == np.arange(2 * NA_KW - 1)[None, None, :]) & valid[:, :, None]
    by_row = jnp.stack([rpb[:, :, NA_KH_MAX - 1 - v:NA_KH_MAX - 1 - v + kh, :] for v in range(kh)], axis=1)
    t = jnp.einsum("lvhib,cdb->lvhcid", by_row.astype(F32), jnp.asarray(onehot, F32),
                   precision=lax.Precision.HIGHEST)
    t = jnp.where(valid[None, None, None, :, None, :], t * LOG2E, NEG_INF)
    return t.reshape(rpb.shape[0], kh, NA_HEADS, GRID_W, kh * GRID_W)


DIL_PAD = DIL_HALF * max(d for _, d in DIL_PATTERNS)
DIL_KB = DIL_BLK + 2 * DIL_HALF
DIL_GROUP = 8
DIL_GROUP_WHOLE = 2
DIL_MERGE_ROWS = 512


def _dil_whole_class(seq, d):
    return seq // d <= 2 * DIL_BLK


def _dil_kernel(q_ref, k_ref, v_ref, *rest, seq):
    npat = len(DIL_PATTERNS)
    tbl_refs = rest[:npat]
    o_ref, kp_ref, vp_ref = rest[npat:npat + 3]
    stats = rest[npat + 3:]
    acc_refs, m_refs, l_refs = stats[:npat], stats[npat:2 * npat], stats[2 * npat:]

    zeros_pad = jnp.zeros((DIL_PAD, LANES), F32)
    kp_ref[0:DIL_PAD, :] = zeros_pad
    kp_ref[DIL_PAD + seq:DIL_PAD + seq + DIL_PAD, :] = zeros_pad
    vp_ref[0:DIL_PAD, :] = zeros_pad
    vp_ref[DIL_PAD + seq:DIL_PAD + seq + DIL_PAD, :] = zeros_pad
    kp_ref[DIL_PAD:DIL_PAD + seq, :] = k_ref[...]
    vp_ref[DIL_PAD:DIL_PAD + seq, :] = v_ref[...]

    lane = lax.broadcasted_iota(jnp.int32, (1, LANES), 1)
    low = lane < HEAD_DIM

    def attend_group(items, acc_ref, m_ref, l_ref):
        sels = (low, jnp.logical_not(low))
        scores = []
        for _, q2, k2, _, bias in items:
            for w in range(2):
                qm = jnp.where(sels[w], q2, jnp.zeros_like(q2))
                scores.append(_dot_nt(qm, k2) + bias(w))
        probs = []
        for s in scores:
            m = jnp.max(s, axis=-1, keepdims=True)
            p = jnp.exp2(s - m)
            probs.append((p.astype(BF16), m, jnp.sum(p, axis=-1, keepdims=True)))
        for idx, (rows, q2, _, v2, _) in enumerate(items):
            nq = q2.shape[0]
            m_b = jnp.zeros((nq, LANES), F32)
            l_b = jnp.zeros((nq, LANES), F32)
            o_b = jnp.zeros((nq, LANES), F32)
            for w in range(2):
                p, m, den = probs[2 * idx + w]
                o = _dot(p, v2)
                m_b = jnp.where(sels[w], m, m_b)
                l_b = jnp.where(sels[w], den, l_b)
                o_b = jnp.where(sels[w], o, o_b)
            acc_ref[rows, :] = o_b
            m_ref[rows, :] = m_b
            l_ref[rows, :] = l_b

    for pat, (_, d) in enumerate(DIL_PATTERNS):
        length = seq // d
        tbl_ref, acc_ref, m_ref, l_ref = tbl_refs[pat], acc_refs[pat], m_refs[pat], l_refs[pat]

        if _dil_whole_class(seq, d):
            group = min(DIL_GROUP_WHOLE, d)

            def whole(g0, carry, d=d, length=length, group=group, tbl_ref=tbl_ref,
                      acc_ref=acc_ref, m_ref=m_ref, l_ref=l_ref):
                items = []
                for g in range(group):
                    r = g0 * group + g
                    rows = pl.ds(r, length, stride=d) if d > 1 else pl.ds(r, length)
                    q2 = q_ref[rows, :].astype(BF16)
                    items.append((rows, q2, k_ref[rows, :].astype(BF16), v_ref[rows, :].astype(BF16),
                                  lambda w: tbl_ref[w]))
                attend_group(items, acc_ref, m_ref, l_ref)
                return carry

            lax.fori_loop(0, d // group, whole, 0)
        else:
            nblk = length // DIL_BLK
            group = min(DIL_GROUP, nblk)

            def block(t0, carry, d=d, nblk=nblk, group=group, tbl_ref=tbl_ref,
                      acc_ref=acc_ref, m_ref=m_ref, l_ref=l_ref):
                items = []
                for g in range(group):
                    t = t0 * group + g
                    r = t // nblk
                    n = t % nblk
                    variant = jnp.where(n == 0, 1, 0) + jnp.where(n == nblk - 1, 2, 0)
                    q_start = r + d * DIL_BLK * n
                    k_start = DIL_PAD + r + d * (DIL_BLK * n - DIL_HALF)
                    if d == 1:
                        rows_q = pl.ds(q_start, DIL_BLK)
                        rows_k = pl.ds(k_start, DIL_KB)
                    else:
                        rows_q = pl.ds(q_start, DIL_BLK, stride=d)
                        rows_k = pl.ds(k_start, DIL_KB, stride=d)
                    q2 = q_ref[rows_q, :].astype(BF16)
                    items.append((rows_q, q2, kp_ref[rows_k, :].astype(BF16), vp_ref[rows_k, :].astype(BF16),
                                  lambda w, variant=variant: tbl_ref[variant, w]))
                attend_group(items, acc_ref, m_ref, l_ref)
                return carry

            lax.fori_loop(0, d * nblk // group, block, 0)

    def merge(c, carry):
        rows = pl.ds(pl.multiple_of(c * DIL_MERGE_ROWS, DIL_MERGE_ROWS), DIL_MERGE_ROWS)
        ms = [m_ref[rows, :] for m_ref in m_refs]
        m_all = functools.reduce(jnp.maximum, ms)
        num = jnp.zeros((DIL_MERGE_ROWS, LANES), F32)
        den = jnp.zeros((DIL_MERGE_ROWS, LANES), F32)
        for m_i, acc_ref, l_ref in zip(ms, acc_refs, l_refs):
            e = jnp.exp2(m_i - m_all)
            num = num + acc_ref[rows, :] * e
            den = den + l_ref[rows, :] * e
        o_ref[rows, :] = (num / den).astype(BF16)
        return carry

    lax.fori_loop(0, seq // DIL_MERGE_ROWS, merge, 0)


def _dil_attention(zb3, tbls):
    bsz, seq, _ = zb3.shape
    npair = DIL_HEADS // 2
    blk = lambda off: pl.BlockSpec((None, seq, LANES), lambda b, p: (b, 0, off + p))
    tbl_specs = []
    for t in tbls:
        if t.ndim == 3:
            tbl_specs.append(pl.BlockSpec((2,) + t.shape[1:], lambda b, p: (p, 0, 0)))
        else:
            tbl_specs.append(pl.BlockSpec((t.shape[0], 2) + t.shape[2:], lambda b, p: (0, p, 0, 0)))
    stat = pltpu.VMEM((seq, LANES), F32)
    return pl.pallas_call(
        functools.partial(_dil_kernel, seq=seq),
        grid=(bsz, npair),
        in_specs=[blk(0), blk(npair), blk(2 * npair)] + tbl_specs,
        out_specs=pl.BlockSpec((None, seq, LANES), lambda b, p: (b, 0, p)),
        out_shape=jax.ShapeDtypeStruct((bsz, seq, D_B), BF16),
        scratch_shapes=[pltpu.VMEM((seq + 2 * DIL_PAD, LANES), F32),
                        pltpu.VMEM((seq + 2 * DIL_PAD, LANES), F32)] + [stat] * (3 * len(DIL_PATTERNS)),
        compiler_params=_params(2),
        name="dil_attn",
    )(zb3, zb3, zb3, *tbls)


def _t5_bucket(rel):
    nb = T5_BUCKETS // 2
    max_exact = nb // 2
    n = np.abs(rel)
    large = max_exact + (np.log(np.maximum(n, 1) / max_exact)
                         / math.log(T5_MAX_DIST / max_exact) * (nb - max_exact)).astype(np.int64)
    large = np.minimum(large, nb - 1)
    return (np.where(rel > 0, nb, 0) + np.where(n < max_exact, n, large)).astype(np.int32)


def _t5_bias(t5_table, rel):
    onehot = _t5_bucket(rel)[:, :, None] == np.arange(T5_BUCKETS)[None, None, :]
    return jnp.einsum("qkb,bh->hqk", jnp.asarray(onehot, F32), t5_table.astype(F32),
                      precision=lax.Precision.HIGHEST)


def _dil_bias_tables(t5_table, seq):
    tables = []
    for _, d in DIL_PATTERNS:
        if _dil_whole_class(seq, d):
            pos = np.arange(seq // d)
            rel = pos[None, :] - pos[:, None]
            valid = np.abs(rel) <= DIL_HALF
            bias = _t5_bias(t5_table, rel * d)
            tables.append(jnp.where(valid[None], bias * LOG2E, NEG_INF))
        else:
            a = np.arange(DIL_BLK)[:, None]
            j = np.arange(DIL_KB)[None, :]
            rel = j - DIL_HALF - a
            band = np.abs(rel) <= DIL_HALF
            not_before = np.broadcast_to(j >= DIL_HALF, band.shape)
            not_after = np.broadcast_to(j < DIL_BLK + DIL_HALF, band.shape)
            valid = np.stack([band, band & not_before, band & not_after, band & not_before & not_after])
            bias = _t5_bias(t5_table, rel * d)
            tables.append(jnp.where(valid[:, None], bias[None] * LOG2E, NEG_INF))
    return tables


MLA_TQ = 1024
MLA_CQ = 512
MLA_VT_ROWS = 80


def _mla_kernel(q_ref, k_ref, vt_ref, o_ref):
    chains = [(h, c) for c in range(MLA_TQ // MLA_CQ) for h in range(MLA_HEADS)]

    def scores(h, c):
        cols = slice(h * LANES, (h + 1) * LANES)
        return _dot_nt(k_ref[:, cols], q_ref[c * MLA_CQ:(c + 1) * MLA_CQ, cols])

    outs = []
    s_next = scores(*chains[0])
    for idx, (h, c) in enumerate(chains):
        s = s_next
        if idx + 1 < len(chains):
            s_next = scores(*chains[idx + 1])
        p = jnp.exp2(s - jnp.max(s, axis=0, keepdims=True)).astype(BF16)
        ov = _dot(vt_ref[h * MLA_VT_ROWS:(h + 1) * MLA_VT_ROWS, :], p)
        outs.append(ov[0:MLA_V] / ov[MLA_V:MLA_V + 1])
        if h == MLA_HEADS - 1:
            o_ref[c * MLA_CQ:(c + 1) * MLA_CQ, :] = jnp.concatenate(outs, axis=0).T.astype(BF16)
            outs = []


def _mla_attention(qm3, km3, vt3):
    bsz, seq, _ = qm3.shape
    return pl.pallas_call(
        _mla_kernel,
        grid=(bsz, seq // MLA_TQ),
        in_specs=[pl.BlockSpec((None, MLA_TQ, MLA_QK_W), lambda b, i: (b, i, 0)),
                  pl.BlockSpec((None, seq, MLA_QK_W), lambda b, i: (b, 0, 0), pipeline_mode=pl.Buffered(1)),
                  pl.BlockSpec((None, MLA_HEADS * MLA_VT_ROWS, seq), lambda b, i: (b, 0, 0),
                               pipeline_mode=pl.Buffered(1))],
        out_specs=pl.BlockSpec((None, MLA_TQ, D_C), lambda b, i: (b, i, 0)),
        out_shape=jax.ShapeDtypeStruct((bsz, seq, D_C), BF16),
        compiler_params=_params(2),
        name="mla_attn",
    )(qm3, km3, vt3)


def _out_kernel(x_ref, oa_ref, ob_ref, oc_ref, w1_ref, w2_ref, w3_ref, g_ref, gate_ref, o_ref):
    y = _dot(oa_ref[...], w1_ref[...]) + _dot(ob_ref[...], w2_ref[...]) + _dot(oc_ref[...], w3_ref[...])
    o_ref[...] = x_ref[...] + gate_ref[...] * _rms(y, g_ref[...])


def _out_proj(x, oa, ob, oc, w1, w2, w3, g_post, mod3, l, bsz, seq):
    n = x.shape[0]
    tm = 512
    tps = seq // tm
    row = lambda i: (i, 0)
    const = lambda i: (0, 0)
    full = lambda a: pl.BlockSpec(a.shape, const)
    return pl.pallas_call(
        _out_kernel,
        grid=(n // tm,),
        in_specs=[pl.BlockSpec((tm, D_MODEL), row),
                  pl.BlockSpec((tm, D_A), row), pl.BlockSpec((tm, D_B), row), pl.BlockSpec((tm, D_C), row),
                  full(w1), full(w2), full(w3), full(g_post),
                  pl.BlockSpec((None, 1, D_MODEL), lambda i: ((l * bsz + i // tps) * N_MOD + 2, 0, 0))],
        out_specs=pl.BlockSpec((tm, D_MODEL), row),
        out_shape=jax.ShapeDtypeStruct((n, D_MODEL), F32),
        compiler_params=_params(1),
        name="out_proj",
    )(x, oa, ob, oc, w1, w2, w3, g_post, mod3)


FFN_TM = 512
FFN_CHUNK = 256


def _ffn_kernel(x_ref, xp_ref, xn_ref, g_ref, sh_ref, sc_ref, wa_ref, wg_ref, cw_ref, cb_ref, wd_ref,
                gpost_ref, gate_ref, o_ref, u_ref, *, tiles_per_seq):
    i = pl.program_id(0)
    has_prev = (i % tiles_per_seq) != 0
    has_next = (i % tiles_per_seq) != tiles_per_seq - 1
    g = g_ref[...]
    sc = 1.0 + sc_ref[...]
    sh = sh_ref[...]
    x = x_ref[...]
    pre = lambda t: _rms(t, g) * sc + sh
    h_prev = jnp.where(has_prev, pre(xp_ref[...]), 0.0)
    h_next = jnp.where(has_next, pre(xn_ref[...]), 0.0)
    h_mid = pre(x)
    h = h_mid.astype(BF16)
    h_ext = jnp.concatenate([h_prev, h_mid, h_next], axis=0).astype(BF16)
    tm = x.shape[0]
    ext = tm + 2 * SUBLANES
    mid = slice(SUBLANES, SUBLANES + tm)
    for c0 in range(0, D_FF, FFN_CHUNK):
        cols = slice(c0, c0 + FFN_CHUNK)
        a = _dot(h, wa_ref[:, cols])
        ge = _dot(h_ext, wg_ref[:, cols])
        g_prev = pltpu.roll(ge, 1, axis=0)[mid]
        g_next = pltpu.roll(ge, ext - 1, axis=0)[mid]
        gc = cb_ref[:, cols] + g_prev * cw_ref[0:1, cols]
        gc = gc + ge[mid] * cw_ref[1:2, cols]
        gc = gc + g_next * cw_ref[2:3, cols]
        u_ref[:, cols] = (jax.nn.gelu(gc) * a).astype(BF16)
    y = _dot(u_ref[...], wd_ref[...])
    o_ref[...] = x + gate_ref[...] * _rms(y, gpost_ref[...])


def _ffn(x, mod3, l, bsz, seq, g_pre, wa, wg, cw, cb, wd, g_post):
    n = x.shape[0]
    tm = FFN_TM
    tps = seq // tm
    r8 = tm // SUBLANES
    last8 = n // SUBLANES - 1
    const = lambda i: (0, 0)
    full = lambda a: pl.BlockSpec(a.shape, const)
    weight = lambda a: pl.BlockSpec(a.shape, const, pipeline_mode=pl.Buffered(1))
    mrow = lambda k: (lambda i: ((l * bsz + i // tps) * N_MOD + k, 0, 0))
    return pl.pallas_call(
        functools.partial(_ffn_kernel, tiles_per_seq=tps),
        grid=(n // tm,),
        in_specs=[pl.BlockSpec((tm, D_MODEL), lambda i: (i, 0)),
                  pl.BlockSpec((SUBLANES, D_MODEL), lambda i: (jnp.maximum(i * r8 - 1, 0), 0)),
                  pl.BlockSpec((SUBLANES, D_MODEL), lambda i: (jnp.minimum((i + 1) * r8, last8), 0)),
                  full(g_pre),
                  pl.BlockSpec((None, 1, D_MODEL), mrow(3)),
                  pl.BlockSpec((None, 1, D_MODEL), mrow(4)),
                  weight(wa), weight(wg), full(cw), full(cb), weight(wd), full(g_post),
                  pl.BlockSpec((None, 1, D_MODEL), mrow(5))],
        out_specs=pl.BlockSpec((tm, D_MODEL), lambda i: (i, 0)),
        out_shape=jax.ShapeDtypeStruct((n, D_MODEL), F32),
        scratch_shapes=[pltpu.VMEM((tm, D_FF), BF16)],
        compiler_params=_params(1),
        name="conv_ffn",
    )(x, x, x, g_pre, mod3, mod3, wa, wg, cw, cb, wd, g_post, mod3)


def _rope_tables(seq):
    inv_freq = jnp.asarray(ROPE_THETA ** (-np.arange(0, MLA_ROPE, 2, dtype=np.float32) / MLA_ROPE), F32)
    ang = jnp.arange(seq, dtype=F32)[:, None] * inv_freq[None, :]
    cos, sin = jnp.cos(ang), jnp.sin(ang)
    cos2 = jnp.concatenate([cos, cos], axis=1)
    sin2 = jnp.concatenate([sin, sin], axis=1)
    head_c = jnp.concatenate([cos2, jnp.ones((seq, MLA_NOPE), F32), jnp.zeros((seq, LANES - MLA_NOPE - MLA_ROPE), F32)], axis=1)
    head_s = jnp.concatenate([sin2, jnp.zeros((seq, LANES - MLA_ROPE), F32)], axis=1)
    cq = jnp.concatenate([head_c] * MLA_HEADS, axis=1)
    sq = jnp.concatenate([head_s] * MLA_HEADS, axis=1)
    ck = jnp.concatenate([cos2, jnp.zeros((seq, LANES - MLA_ROPE), F32)], axis=1)
    return cq, sq, ck, head_s


def _rot_half_cols(w):
    half = w.shape[1] // 2
    return jnp.concatenate([-w[:, half:], w[:, :half]], axis=1)


def _layer_weights(w_in, w_uq, w_ukv, w_out, w_up, w_down):
    d = w_in.shape[0]
    o_b = 3 * D_A
    o_c = o_b + 3 * D_B
    o_kr = o_c + MLA_Q_LORA + MLA_KV_LORA
    wa = w_in[:, :o_b].astype(BF16)
    wb = w_in[:, o_b:o_c].astype(BF16)
    w_kr = w_in[:, o_kr:o_kr + MLA_ROPE]
    zpad = jnp.zeros((d, LANES - MLA_ROPE), F32)
    wc = jnp.concatenate([w_in[:, o_c:o_kr], w_kr, zpad, _rot_half_cols(w_kr), zpad], axis=1).astype(BF16)

    hq = MLA_NOPE + MLA_ROPE
    q_cols, qr_cols, k_cols, v_cols = [], [], [], []
    zq = jnp.zeros((MLA_Q_LORA, LANES - hq), F32)
    for h in range(MLA_HEADS):
        nope = w_uq[:, h * hq:h * hq + MLA_NOPE]
        rope = w_uq[:, h * hq + MLA_NOPE:(h + 1) * hq]
        q_cols += [rope, nope, zq]
        qr_cols += [_rot_half_cols(rope), jnp.zeros((MLA_Q_LORA, LANES - MLA_ROPE), F32)]
        hk = MLA_NOPE + MLA_V
        k_cols += [jnp.zeros((MLA_KV_LORA, MLA_ROPE), F32), w_ukv[:, h * hk:h * hk + MLA_NOPE],
                   jnp.zeros((MLA_KV_LORA, LANES - hq), F32)]
        v_cols += [w_ukv[:, h * hk + MLA_NOPE:(h + 1) * hk], jnp.zeros((MLA_KV_LORA, MLA_VT_ROWS - MLA_V), F32)]
    cat = lambda cols: jnp.concatenate(cols, axis=1).astype(BF16)
    w1 = w_out[:D_A].astype(BF16)
    w2 = w_out[D_A:D_A + D_B].astype(BF16)
    w3 = w_out[D_A + D_B:].astype(BF16)
    w_a = w_up[:, :D_FF].astype(BF16)
    w_g = w_up[:, D_FF:].astype(BF16)
    return (wa, wb, wc, cat(q_cols), cat(qr_cols), cat(k_cols), cat(v_cols).T, w1, w2, w3, w_a, w_g,
            w_down.astype(BF16))


def kernel(x, c, w_ada, b_ada, g_pre_mix, g_post_mix, g_pre_ffn, g_post_ffn, w_in, na_rpb, t5_table,
           mla_g_q, mla_g_kv, w_uq, w_ukv, w_out, w_up, conv_w, conv_b, w_down):
    bsz, seq, d = x.shape
    depth = w_in.shape[0]
    n = bsz * seq
    rows = seq // GRID_W
    assert d == D_MODEL and seq % (DIL_BLK * max(dd for _, dd in DIL_PATTERNS)) == 0 and rows >= NA_KH_MAX

    mod3 = _modulation(c, w_ada, b_ada).reshape(depth * bsz * N_MOD, 1, D_MODEL)
    cq, sq, ck, sk = _rope_tables(seq)
    dil_tbls = _dil_bias_tables(t5_table, seq)
    na_tbls = _na_bias_tables(na_rpb, rows)
    row2 = lambda v: v.reshape(1, -1)

    xf = x.reshape(n, d)
    for l in range(depth):
        (wa, wb, wc, wuq, wuqr, wuk, wuv, w1, w2, w3, w_a, w_g, w_d) = _layer_weights(
            w_in[l], w_uq[l], w_ukv[l], w_out[l], w_up[l], w_down[l])
        za, zb, qm, km, vt = _in_proj(xf, mod3, l, bsz, seq, row2(g_pre_mix[l]), wa, wb, wc,
                                      row2(mla_g_q[l]), row2(mla_g_kv[l]), wuq, wuqr, wuk, wuv,
                                      cq, sq, ck, sk)
        o_a = _na_attention(za.reshape(bsz, seq, 3 * D_A), na_tbls[l])
        o_b = _dil_attention(zb.reshape(bsz, seq, 3 * D_B), dil_tbls)
        o_c = _mla_attention(qm.reshape(bsz, seq, MLA_QK_W), km.reshape(bsz, seq, MLA_QK_W), vt)
        xf = _out_proj(xf, o_a.reshape(n, D_A), o_b.reshape(n, D_B), o_c.reshape(n, D_C),
                       w1, w2, w3, row2(g_post_mix[l]), mod3, l, bsz, seq)
        xf = _ffn(xf, mod3, l, bsz, seq, row2(g_pre_ffn[l]), w_a, w_g, conv_w[l], row2(conv_b[l]), w_d,
                  row2(g_post_ffn[l]))
    return xf.reshape(bsz, seq, d)
```

```python
import functools
import math

import numpy as np
import jax
import jax.numpy as jnp
from jax import lax
from jax.experimental import pallas as pl
from jax.experimental.pallas import tpu as pltpu

F32 = jnp.float32
BF16 = jnp.bfloat16

D_MODEL = 1024
HEAD_DIM = 64
GRID_W = 64
EPS = 1e-6
NEG_INF = -1e30

NA_HEADS = 4
NA_KH_MAX = 8
NA_KW = 16

DIL_HEADS = 8
DIL_PATTERNS = ((128, 1), (512, 4), (2048, 16))
DIL_BLK = 128
DIL_HALF = 64

MLA_HEADS = 4
MLA_Q_LORA = 384
MLA_KV_LORA = 256
MLA_NOPE = 64
MLA_ROPE = 32
MLA_V = 64
ROPE_THETA = 10000.0

T5_BUCKETS = 32
T5_MAX_DIST = 1024

D_FF = 2816
CONV_W = 3

D_A = NA_HEADS * HEAD_DIM
D_B = DIL_HEADS * HEAD_DIM
D_C = MLA_HEADS * MLA_V
N_MOD = 6

LANES = 128
SUBLANES = 8
VMEM_LIMIT = 56 * 1024 * 1024

LOG2E = math.log2(math.e)
Q_SCALE = HEAD_DIM ** -0.5 * LOG2E
MLA_Q_SCALE = (MLA_NOPE + MLA_ROPE) ** -0.5 * LOG2E

ZC_W = MLA_Q_LORA + MLA_KV_LORA + 2 * LANES
MLA_QK_W = MLA_HEADS * LANES


def _params(n_axes):
    return pltpu.CompilerParams(dimension_semantics=("arbitrary",) * n_axes,
                                vmem_limit_bytes=VMEM_LIMIT)


def _dot(a, b):
    return jnp.dot(a, b, preferred_element_type=F32)


def _dot_nt(a, b):
    return lax.dot_general(a, b, (((1,), (1,)), ((), ())), preferred_element_type=F32)


def _rms(x, g):
    return x * lax.rsqrt(jnp.mean(x * x, axis=-1, keepdims=True) + EPS) * g


def _mod_kernel(c_ref, w_ref, b_ref, o_ref):
    ca = jax.nn.silu(c_ref[...]).astype(BF16)
    o_ref[0] = _dot(ca, w_ref[0].astype(BF16)) + b_ref[0]


def _modulation(c, w_ada, b_ada):
    depth, d, n = w_ada.shape
    bsz = c.shape[0]
    tn = 1536
    return pl.pallas_call(
        _mod_kernel,
        grid=(depth, n // tn),
        in_specs=[pl.BlockSpec((bsz, d), lambda l, j: (0, 0)),
                  pl.BlockSpec((1, d, tn), lambda l, j: (l, 0, j)),
                  pl.BlockSpec((1, 1, tn), lambda l, j: (l, 0, j))],
        out_specs=pl.BlockSpec((1, bsz, tn), lambda l, j: (l, 0, j)),
        out_shape=jax.ShapeDtypeStruct((depth, bsz, n), F32),
        compiler_params=_params(2),
        name="adaln_mod",
    )(c, w_ada, b_ada.reshape(depth, 1, n))


def _in_kernel(x_ref, g_ref, sh_ref, sc_ref, wa_ref, wb_ref, wc_ref, gq_ref, gkv_ref,
               wuq_ref, wuqr_ref, wuk_ref, wuvt_ref, cq_ref, sq_ref, ck_ref, sk_ref,
               za_ref, zb_ref, qm_ref, km_ref, vt_ref):
    x = x_ref[...]
    h = (_rms(x, g_ref[...]) * (1.0 + sc_ref[...]) + sh_ref[...]).astype(BF16)
    za_ref[:, 0:D_A] = (_dot(h, wa_ref[:, 0:D_A]) * Q_SCALE).astype(BF16)
    za_ref[:, D_A:3 * D_A] = _dot(h, wa_ref[:, D_A:3 * D_A]).astype(BF16)
    zb_ref[:, 0:D_B] = _dot(h, wb_ref[:, 0:D_B]) * Q_SCALE
    for j in range(D_B, 3 * D_B, D_B):
        zb_ref[:, j:j + D_B] = _dot(h, wb_ref[:, j:j + D_B])
    zc = _dot(h, wc_ref[...])
    c_q = zc[:, 0:MLA_Q_LORA]
    c_kv = zc[:, MLA_Q_LORA:MLA_Q_LORA + MLA_KV_LORA]
    kr_a = zc[:, MLA_Q_LORA + MLA_KV_LORA:MLA_Q_LORA + MLA_KV_LORA + LANES]
    kr_b = zc[:, MLA_Q_LORA + MLA_KV_LORA + LANES:ZC_W]
    cqn = _rms(c_q, gq_ref[...]).astype(BF16)
    q = _dot(cqn, wuq_ref[...]) * cq_ref[...] + _dot(cqn, wuqr_ref[...]) * sq_ref[...]
    qm_ref[...] = (q * MLA_Q_SCALE).astype(BF16)
    ckvn = _rms(c_kv, gkv_ref[...]).astype(BF16)
    kr = kr_a * ck_ref[...] + kr_b * sk_ref[...]
    km_ref[...] = (_dot(ckvn, wuk_ref[...]) + jnp.concatenate([kr] * MLA_HEADS, axis=1)).astype(BF16)
    vrow = lax.broadcasted_iota(jnp.int32, (MLA_HEADS * MLA_VT_ROWS, 1), 0)
    ones_rows = functools.reduce(jnp.logical_or, [vrow == h * MLA_VT_ROWS + MLA_V for h in range(MLA_HEADS)])
    vt = _dot_nt(wuvt_ref[...], ckvn) + jnp.where(ones_rows, 1.0, 0.0)
    vt_ref[...] = vt.astype(BF16)


def _in_proj(x, mod3, l, bsz, seq, g_pre, wa, wb, wc, gq, gkv, wuq, wuqr, wuk, wuv, cq, sq, ck, sk):
    n = x.shape[0]
    tm = 512
    tps = seq // tm
    row = lambda i: (i, 0)
    const = lambda i: (0, 0)
    pos = lambda i: (i % tps, 0)
    mrow = lambda k: (lambda i: ((l * bsz + i // tps) * N_MOD + k, 0, 0))
    full = lambda a: pl.BlockSpec(a.shape, const)
    return pl.pallas_call(
        _in_kernel,
        grid=(n // tm,),
        in_specs=[pl.BlockSpec((tm, D_MODEL), row),
                  full(g_pre),
                  pl.BlockSpec((None, 1, D_MODEL), mrow(0)),
                  pl.BlockSpec((None, 1, D_MODEL), mrow(1)),
                  full(wa), full(wb), full(wc), full(gq), full(gkv),
                  full(wuq), full(wuqr), full(wuk), full(wuv),
                  pl.BlockSpec((tm, MLA_QK_W), pos), pl.BlockSpec((tm, MLA_QK_W), pos),
                  pl.BlockSpec((tm, LANES), pos), pl.BlockSpec((tm, LANES), pos)],
        out_specs=[pl.BlockSpec((tm, 3 * D_A), row),
                   pl.BlockSpec((tm, 3 * D_B), row),
                   pl.BlockSpec((tm, MLA_QK_W), row),
                   pl.BlockSpec((tm, MLA_QK_W), row),
                   pl.BlockSpec((None, MLA_HEADS * MLA_VT_ROWS, tm), lambda i: (i // tps, 0, i % tps))],
        out_shape=[jax.ShapeDtypeStruct((n, 3 * D_A), BF16),
                   jax.ShapeDtypeStruct((n, 3 * D_B), F32),
                   jax.ShapeDtypeStruct((n, MLA_QK_W), BF16),
                   jax.ShapeDtypeStruct((n, MLA_QK_W), BF16),
                   jax.ShapeDtypeStruct((bsz, MLA_HEADS * MLA_VT_ROWS, seq), BF16)],
        compiler_params=_params(1),
        name="in_proj",
    )(x, g_pre, mod3, mod3, wa, wb, wc, gq, gkv, wuq, wuqr, wuk, wuv, cq, sq, ck, sk)


NA_ROWS_PER_STEP = 8
NA_GROUP = 4


def _na_kernel(q_ref, k_ref, v_ref, tbl_ref, o_ref, *, rows):
    i = pl.program_id(1)
    lane = lax.broadcasted_iota(jnp.int32, (1, LANES), 1)
    low = lane < HEAD_DIM
    kh = NA_KH_MAX

    sels = (low, jnp.logical_not(low))

    def row_group(j0, carry):
        items = []
        scores = []
        for g in range(NA_GROUP):
            j = j0 * NA_GROUP + g
            r = i * NA_ROWS_PER_STEP + j
            r0 = jnp.clip(r - kh // 2, 0, rows - kh)
            variant = r - r0
            qs = pl.multiple_of(j * GRID_W, GRID_W)
            ks = pl.multiple_of(r0 * GRID_W, GRID_W)
            for pair in range(NA_HEADS // 2):
                cols = slice(pair * LANES, (pair + 1) * LANES)
                q2 = q_ref[pl.ds(qs, GRID_W), cols]
                k2 = k_ref[pl.ds(ks, kh * GRID_W), cols]
                items.append((qs, ks, cols))
                for w in range(2):
                    qm = jnp.where(sels[w], q2, jnp.zeros_like(q2))
                    scores.append(_dot_nt(qm, k2) + tbl_ref[variant, 2 * pair + w])
        probs = []
        for s in scores:
            m = jnp.max(s, axis=-1, keepdims=True)
            p = jnp.exp2(s - m)
            probs.append((p.astype(BF16), jnp.sum(p, axis=-1, keepdims=True)))
        for idx, (qs, ks, cols) in enumerate(items):
            v2 = v_ref[pl.ds(ks, kh * GRID_W), cols]
            out = jnp.zeros((GRID_W, LANES), F32)
            for w in range(2):
                p, den = probs[2 * idx + w]
                out = jnp.where(sels[w], _dot(p, v2) / den, out)
            o_ref[pl.ds(qs, GRID_W), cols] = out.astype(BF16)
        return carry

    lax.fori_loop(0, NA_ROWS_PER_STEP // NA_GROUP, row_group, 0)


def _na_attention(za3, tbl):
    bsz, seq, _ = za3.shape
    rows = seq // GRID_W
    tq = NA_ROWS_PER_STEP * GRID_W
    return pl.pallas_call(
        functools.partial(_na_kernel, rows=rows),
        grid=(bsz, seq // tq),
        in_specs=[pl.BlockSpec((None, tq, D_A), lambda b, i: (b, i, 0)),
                  pl.BlockSpec((None, seq, D_A), lambda b, i: (b, 0, 1)),
                  pl.BlockSpec((None, seq, D_A), lambda b, i: (b, 0, 2)),
                  pl.BlockSpec(tbl.shape, lambda b, i: (0, 0, 0, 0))],
        out_specs=pl.BlockSpec((None, tq, D_A), lambda b, i: (b, i, 0)),
        out_shape=jax.ShapeDtypeStruct((bsz, seq, D_A), BF16),
        compiler_params=_params(2),
        name="na_attn",
    )(za3, za3, za3, tbl)


def _na_bias_tables(rpb, rows):
    kh = min(NA_KH_MAX, rows)
    col = np.arange(GRID_W)
    c_start = np.clip(col - NA_KW // 2, 0, GRID_W - NA_KW)
    valid = (col[None, :] >= c_start[:, None]) & (col[None, :] < c_start[:, None] + NA_KW)
    d_col = col[None, :] - col[:, None] + (NA_KW - 1)
    onehot = (d_col[:, :, None] == np.arange(2 * NA_KW - 1)[None, None, :]) & valid[:, :, None]
    by_row = jnp.stack([rpb[:, :, NA_KH_MAX - 1 - v:NA_KH_MAX - 1 - v + kh, :] for v in range(kh)], axis=1)
    t = jnp.einsum("lvhib,cdb->lvhcid", by_row.astype(F32), jnp.asarray(onehot, F32),
                   precision=lax.Precision.HIGHEST)
    t = jnp.where(valid[None, None, None, :, None, :], t * LOG2E, NEG_INF)
    return t.reshape(rpb.shape[0], kh, NA_HEADS, GRID_W, kh * GRID_W)


DIL_PAD = DIL_HALF * max(d for _, d in DIL_PATTERNS)
DIL_KB = DIL_BLK + 2 * DIL_HALF
DIL_GROUP = 8
DIL_GROUP_WHOLE = 2
DIL_MERGE_ROWS = 512


def _dil_whole_class(seq, d):
    return seq // d <= 2 * DIL_BLK


def _dil_kernel(q_ref, k_ref, v_ref, *rest, seq):
    npat = len(DIL_PATTERNS)
    tbl_refs = rest[:npat]
    o_ref, kp_ref, vp_ref = rest[npat:npat + 3]
    stats = rest[npat + 3:]
    acc_refs, m_refs, l_refs = stats[:npat], stats[npat:2 * npat], stats[2 * npat:]

    zeros_pad = jnp.zeros((DIL_PAD, LANES), F32)
    kp_ref[0:DIL_PAD, :] = zeros_pad
    kp_ref[DIL_PAD + seq:DIL_PAD + seq + DIL_PAD, :] = zeros_pad
    vp_ref[0:DIL_PAD, :] = zeros_pad
    vp_ref[DIL_PAD + seq:DIL_PAD + seq + DIL_PAD, :] = zeros_pad
    kp_ref[DIL_PAD:DIL_PAD + seq, :] = k_ref[...]
    vp_ref[DIL_PAD:DIL_PAD + seq, :] = v_ref[...]

    lane = lax.broadcasted_iota(jnp.int32, (1, LANES), 1)
    low = lane < HEAD_DIM

    def attend_group(items, acc_ref, m_ref, l_ref):
        sels = (low, jnp.logical_not(low))
        scores = []
        for _, q2, k2, _, bias in items:
            for w in range(2):
                qm = jnp.where(sels[w], q2, jnp.zeros_like(q2))
                scores.append(_dot_nt(qm, k2) + bias(w))
        probs = []
        for s in scores:
            m = jnp.max(s, axis=-1, keepdims=True)
            p = jnp.exp2(s - m)
            probs.append((p.astype(BF16), m, jnp.sum(p, axis=-1, keepdims=True)))
        for idx, (rows, q2, _, v2, _) in enumerate(items):
            nq = q2.shape[0]
            m_b = jnp.zeros((nq, LANES), F32)
            l_b = jnp.zeros((nq, LANES), F32)
            o_b = jnp.zeros((nq, LANES), F32)
            for w in range(2):
                p, m, den = probs[2 * idx + w]
                o = _dot(p, v2)
                m_b = jnp.where(sels[w], m, m_b)
                l_b = jnp.where(sels[w], den, l_b)
                o_b = jnp.where(sels[w], o, o_b)
            acc_ref[rows, :] = o_b
            m_ref[rows, :] = m_b
            l_ref[rows, :] = l_b

    for pat, (_, d) in enumerate(DIL_PATTERNS):
        length = seq // d
        tbl_ref, acc_ref, m_ref, l_ref = tbl_refs[pat], acc_refs[pat], m_refs[pat], l_refs[pat]

        if _dil_whole_class(seq, d):
            group = min(DIL_GROUP_WHOLE, d)

            def whole(g0, carry, d=d, length=length, group=group, tbl_ref=tbl_ref,
                      acc_ref=acc_ref, m_ref=m_ref, l_ref=l_ref):
                items = []
                for g in range(group):
                    r = g0 * group + g
                    rows = pl.ds(r, length, stride=d) if d > 1 else pl.ds(r, length)
                    q2 = q_ref[rows, :].astype(BF16)
                    items.append((rows, q2, k_ref[rows, :].astype(BF16), v_ref[rows, :].astype(BF16),
                                  lambda w: tbl_ref[w]))
                attend_group(items, acc_ref, m_ref, l_ref)
                return carry

            lax.fori_loop(0, d // group, whole, 0)
        else:
            nblk = length // DIL_BLK
            group = min(DIL_GROUP, nblk)

            def block(t0, carry, d=d, nblk=nblk, group=group, tbl_ref=tbl_ref,
                      acc_ref=acc_ref, m_ref=m_ref, l_ref=l_ref):
                items = []
                for g in range(group):
                    t = t0 * group + g
                    r = t // nblk
                    n = t % nblk
                    variant = jnp.where(n == 0, 1, 0) + jnp.where(n == nblk - 1, 2, 0)
                    q_start = r + d * DIL_BLK * n
                    k_start = DIL_PAD + r + d * (DIL_BLK * n - DIL_HALF)
                    if d == 1:
                        rows_q = pl.ds(q_start, DIL_BLK)
                        rows_k = pl.ds(k_start, DIL_KB)
                    else:
                        rows_q = pl.ds(q_start, DIL_BLK, stride=d)
                        rows_k = pl.ds(k_start, DIL_KB, stride=d)
                    q2 = q_ref[rows_q, :].astype(BF16)
                    items.append((rows_q, q2, kp_ref[rows_k, :].astype(BF16), vp_ref[rows_k, :].astype(BF16),
                                  lambda w, variant=variant: tbl_ref[variant, w]))
                attend_group(items, acc_ref, m_ref, l_ref)
                return carry

            lax.fori_loop(0, d * nblk // group, block, 0)

    def merge(c, carry):
        rows = pl.ds(pl.multiple_of(c * DIL_MERGE_ROWS, DIL_MERGE_ROWS), DIL_MERGE_ROWS)
        ms = [m_ref[rows, :] for m_ref in m_refs]
        m_all = functools.reduce(jnp.maximum, ms)
        num = jnp.zeros((DIL_MERGE_ROWS, LANES), F32)
        den = jnp.zeros((DIL_MERGE_ROWS, LANES), F32)
        for m_i, acc_ref, l_ref in zip(ms, acc_refs, l_refs):
            e = jnp.exp2(m_i - m_all)
            num = num + acc_ref[rows, :] * e
            den = den + l_ref[rows, :] * e
        o_ref[rows, :] = (num / den).astype(BF16)
        return carry

    lax.fori_loop(0, seq // DIL_MERGE_ROWS, merge, 0)


def _dil_attention(zb3, tbls):
    bsz, seq, _ = zb3.shape
    npair = DIL_HEADS // 2
    blk = lambda off: pl.BlockSpec((None, seq, LANES), lambda b, p: (b, 0, off + p))
    tbl_specs = []
    for t in tbls:
        if t.ndim == 3:
            tbl_specs.append(pl.BlockSpec((2,) + t.shape[1:], lambda b, p: (p, 0, 0)))
        else:
            tbl_specs.append(pl.BlockSpec((t.shape[0], 2) + t.shape[2:], lambda b, p: (0, p, 0, 0)))
    stat = pltpu.VMEM((seq, LANES), F32)
    return pl.pallas_call(
        functools.partial(_dil_kernel, seq=seq),
        grid=(bsz, npair),
        in_specs=[blk(0), blk(npair), blk(2 * npair)] + tbl_specs,
        out_specs=pl.BlockSpec((None, seq, LANES), lambda b, p: (b, 0, p)),
        out_shape=jax.ShapeDtypeStruct((bsz, seq, D_B), BF16),
        scratch_shapes=[pltpu.VMEM((seq + 2 * DIL_PAD, LANES), F32),
                        pltpu.VMEM((seq + 2 * DIL_PAD, LANES), F32)] + [stat] * (3 * len(DIL_PATTERNS)),
        compiler_params=_params(2),
        name="dil_attn",
    )(zb3, zb3, zb3, *tbls)


def _t5_bucket(rel):
    nb = T5_BUCKETS // 2
    max_exact = nb // 2
    n = np.abs(rel)
    large = max_exact + (np.log(np.maximum(n, 1) / max_exact)
                         / math.log(T5_MAX_DIST / max_exact) * (nb - max_exact)).astype(np.int64)
    large = np.minimum(large, nb - 1)
    return (np.where(rel > 0, nb, 0) + np.where(n < max_exact, n, large)).astype(np.int32)


def _t5_bias(t5_table, rel):
    onehot = _t5_bucket(rel)[:, :, None] == np.arange(T5_BUCKETS)[None, None, :]
    return jnp.einsum("qkb,bh->hqk", jnp.asarray(onehot, F32), t5_table.astype(F32),
                      precision=lax.Precision.HIGHEST)


def _dil_bias_tables(t5_table, seq):
    tables = []
    for _, d in DIL_PATTERNS:
        if _dil_whole_class(seq, d):
            pos = np.arange(seq // d)
            rel = pos[None, :] - pos[:, None]
            valid = np.abs(rel) <= DIL_HALF
            bias = _t5_bias(t5_table, rel * d)
            tables.append(jnp.where(valid[None], bias * LOG2E, NEG_INF))
        else:
            a = np.arange(DIL_BLK)[:, None]
            j = np.arange(DIL_KB)[None, :]
            rel = j - DIL_HALF - a
            band = np.abs(rel) <= DIL_HALF
            not_before = np.broadcast_to(j >= DIL_HALF, band.shape)
            not_after = np.broadcast_to(j < DIL_BLK + DIL_HALF, band.shape)
            valid = np.stack([band, band & not_before, band & not_after, band & not_before & not_after])
            bias = _t5_bias(t5_table, rel * d)
            tables.append(jnp.where(valid[:, None], bias[None] * LOG2E, NEG_INF))
    return tables


MLA_TQ = 1024
MLA_CQ = 512
MLA_VT_ROWS = 80


def _mla_kernel(q_ref, k_ref, vt_ref, o_ref):
    chains = [(h, c) for c in range(MLA_TQ // MLA_CQ) for h in range(MLA_HEADS)]

    def scores(h, c):
        cols = slice(h * LANES, (h + 1) * LANES)
        return _dot_nt(k_ref[:, cols], q_ref[c * MLA_CQ:(c + 1) * MLA_CQ, cols])

    outs = []
    s_next = scores(*chains[0])
    for idx, (h, c) in enumerate(chains):
        s = s_next
        if idx + 1 < len(chains):
            s_next = scores(*chains[idx + 1])
        sb = s.astype(BF16)
        p = jnp.exp2(sb - jnp.max(sb, axis=0, keepdims=True))
        ov = _dot(vt_ref[h * MLA_VT_ROWS:(h + 1) * MLA_VT_ROWS, :], p)
        outs.append(ov[0:MLA_V] / ov[MLA_V:MLA_V + 1])
        if h == MLA_HEADS - 1:
            o_ref[c * MLA_CQ:(c + 1) * MLA_CQ, :] = jnp.concatenate(outs, axis=0).T.astype(BF16)
            outs = []


def _mla_attention(qm3, km3, vt3):
    bsz, seq, _ = qm3.shape
    return pl.pallas_call(
        _mla_kernel,
        grid=(bsz, seq // MLA_TQ),
        in_specs=[pl.BlockSpec((None, MLA_TQ, MLA_QK_W), lambda b, i: (b, i, 0)),
                  pl.BlockSpec((None, seq, MLA_QK_W), lambda b, i: (b, 0, 0), pipeline_mode=pl.Buffered(1)),
                  pl.BlockSpec((None, MLA_HEADS * MLA_VT_ROWS, seq), lambda b, i: (b, 0, 0),
                               pipeline_mode=pl.Buffered(1))],
        out_specs=pl.BlockSpec((None, MLA_TQ, D_C), lambda b, i: (b, i, 0)),
        out_shape=jax.ShapeDtypeStruct((bsz, seq, D_C), BF16),
        compiler_params=_params(2),
        name="mla_attn",
    )(qm3, km3, vt3)


def _out_kernel(x_ref, oa_ref, ob_ref, oc_ref, w1_ref, w2_ref, w3_ref, g_ref, gate_ref, o_ref):
    y = _dot(oa_ref[...], w1_ref[...]) + _dot(ob_ref[...], w2_ref[...]) + _dot(oc_ref[...], w3_ref[...])
    o_ref[...] = x_ref[...] + gate_ref[...] * _rms(y, g_ref[...])


def _out_proj(x, oa, ob, oc, w1, w2, w3, g_post, mod3, l, bsz, seq):
    n = x.shape[0]
    tm = 512
    tps = seq // tm
    row = lambda i: (i, 0)
    const = lambda i: (0, 0)
    full = lambda a: pl.BlockSpec(a.shape, const)
    return pl.pallas_call(
        _out_kernel,
        grid=(n // tm,),
        in_specs=[pl.BlockSpec((tm, D_MODEL), row),
                  pl.BlockSpec((tm, D_A), row), pl.BlockSpec((tm, D_B), row), pl.BlockSpec((tm, D_C), row),
                  full(w1), full(w2), full(w3), full(g_post),
                  pl.BlockSpec((None, 1, D_MODEL), lambda i: ((l * bsz + i // tps) * N_MOD + 2, 0, 0))],
        out_specs=pl.BlockSpec((tm, D_MODEL), row),
        out_shape=jax.ShapeDtypeStruct((n, D_MODEL), F32),
        compiler_params=_params(1),
        name="out_proj",
    )(x, oa, ob, oc, w1, w2, w3, g_post, mod3)


FFN_TM = 512
FFN_CHUNK = 256


def _ffn_kernel(x_ref, xp_ref, xn_ref, g_ref, sh_ref, sc_ref, wa_ref, wg_ref, cw_ref, cb_ref, wd_ref,
                gpost_ref, gate_ref, o_ref, u_ref, *, tiles_per_seq):
    i = pl.program_id(0)
    has_prev = (i % tiles_per_seq) != 0
    has_next = (i % tiles_per_seq) != tiles_per_seq - 1
    g = g_ref[...]
    sc = 1.0 + sc_ref[...]
    sh = sh_ref[...]
    x = x_ref[...]
    pre = lambda t: _rms(t, g) * sc + sh
    h_prev = jnp.where(has_prev, pre(xp_ref[...]), 0.0)
    h_next = jnp.where(has_next, pre(xn_ref[...]), 0.0)
    h_mid = pre(x)
    h = h_mid.astype(BF16)
    h_ext = jnp.concatenate([h_prev, h_mid, h_next], axis=0).astype(BF16)
    tm = x.shape[0]
    ext = tm + 2 * SUBLANES
    mid = slice(SUBLANES, SUBLANES + tm)
    for c0 in range(0, D_FF, FFN_CHUNK):
        cols = slice(c0, c0 + FFN_CHUNK)
        a = _dot(h, wa_ref[:, cols])
        ge = _dot(h_ext, wg_ref[:, cols])
        g_prev = pltpu.roll(ge, 1, axis=0)[mid]
        g_next = pltpu.roll(ge, ext - 1, axis=0)[mid]
        gc = cb_ref[:, cols] + g_prev * cw_ref[0:1, cols]
        gc = gc + ge[mid] * cw_ref[1:2, cols]
        gc = gc + g_next * cw_ref[2:3, cols]
        u_ref[:, cols] = (jax.nn.gelu(gc) * a).astype(BF16)
    y = _dot(u_ref[...], wd_ref[...])
    o_ref[...] = x + gate_ref[...] * _rms(y, gpost_ref[...])


def _ffn(x, mod3, l, bsz, seq, g_pre, wa, wg, cw, cb, wd, g_post):
    n = x.shape[0]
    tm = FFN_TM
    tps = seq // tm
    r8 = tm // SUBLANES
    last8 = n // SUBLANES - 1
    const = lambda i: (0, 0)
    full = lambda a: pl.BlockSpec(a.shape, const)
    weight = lambda a: pl.BlockSpec(a.shape, const, pipeline_mode=pl.Buffered(1))
    mrow = lambda k: (lambda i: ((l * bsz + i // tps) * N_MOD + k, 0, 0))
    return pl.pallas_call(
        functools.partial(_ffn_kernel, tiles_per_seq=tps),
        grid=(n // tm,),
        in_specs=[pl.BlockSpec((tm, D_MODEL), lambda i: (i, 0)),
                  pl.BlockSpec((SUBLANES, D_MODEL), lambda i: (jnp.maximum(i * r8 - 1, 0), 0)),
                  pl.BlockSpec((SUBLANES, D_MODEL), lambda i: (jnp.minimum((i + 1) * r8, last8), 0)),
                  full(g_pre),
                  pl.BlockSpec((None, 1, D_MODEL), mrow(3)),
                  pl.BlockSpec((None, 1, D_MODEL), mrow(4)),
                  weight(wa), weight(wg), full(cw), full(cb), weight(wd), full(g_post),
                  pl.BlockSpec((None, 1, D_MODEL), mrow(5))],
        out_specs=pl.BlockSpec((tm, D_MODEL), lambda i: (i, 0)),
        out_shape=jax.ShapeDtypeStruct((n, D_MODEL), F32),
        scratch_shapes=[pltpu.VMEM((tm, D_FF), BF16)],
        compiler_params=_params(1),
        name="conv_ffn",
    )(x, x, x, g_pre, mod3, mod3, wa, wg, cw, cb, wd, g_post, mod3)


def _rope_tables(seq):
    inv_freq = jnp.asarray(ROPE_THETA ** (-np.arange(0, MLA_ROPE, 2, dtype=np.float32) / MLA_ROPE), F32)
    ang = jnp.arange(seq, dtype=F32)[:, None] * inv_freq[None, :]
    cos, sin = jnp.cos(ang), jnp.sin(ang)
    cos2 = jnp.concatenate([cos, cos], axis=1)
    sin2 = jnp.concatenate([sin, sin], axis=1)
    head_c = jnp.concatenate([cos2, jnp.ones((seq, MLA_NOPE), F32), jnp.zeros((seq, LANES - MLA_NOPE - MLA_ROPE), F32)], axis=1)
    head_s = jnp.concatenate([sin2, jnp.zeros((seq, LANES - MLA_ROPE), F32)], axis=1)
    cq = jnp.concatenate([head_c] * MLA_HEADS, axis=1)
    sq = jnp.concatenate([head_s] * MLA_HEADS, axis=1)
    ck = jnp.concatenate([cos2, jnp.zeros((seq, LANES - MLA_ROPE), F32)], axis=1)
    return cq, sq, ck, head_s


def _rot_half_cols(w):
    half = w.shape[1] // 2
    return jnp.concatenate([-w[:, half:], w[:, :half]], axis=1)


def _layer_weights(w_in, w_uq, w_ukv, w_out, w_up, w_down):
    d = w_in.shape[0]
    o_b = 3 * D_A
    o_c = o_b + 3 * D_B
    o_kr = o_c + MLA_Q_LORA + MLA_KV_LORA
    wa = w_in[:, :o_b].astype(BF16)
    wb = w_in[:, o_b:o_c].astype(BF16)
    w_kr = w_in[:, o_kr:o_kr + MLA_ROPE]
    zpad = jnp.zeros((d, LANES - MLA_ROPE), F32)
    wc = jnp.concatenate([w_in[:, o_c:o_kr], w_kr, zpad, _rot_half_cols(w_kr), zpad], axis=1).astype(BF16)

    hq = MLA_NOPE + MLA_ROPE
    q_cols, qr_cols, k_cols, v_cols = [], [], [], []
    zq = jnp.zeros((MLA_Q_LORA, LANES - hq), F32)
    for h in range(MLA_HEADS):
        nope = w_uq[:, h * hq:h * hq + MLA_NOPE]
        rope = w_uq[:, h * hq + MLA_NOPE:(h + 1) * hq]
        q_cols += [rope, nope, zq]
        qr_cols += [_rot_half_cols(rope), jnp.zeros((MLA_Q_LORA, LANES - MLA_ROPE), F32)]
        hk = MLA_NOPE + MLA_V
        k_cols += [jnp.zeros((MLA_KV_LORA, MLA_ROPE), F32), w_ukv[:, h * hk:h * hk + MLA_NOPE],
                   jnp.zeros((MLA_KV_LORA, LANES - hq), F32)]
        v_cols += [w_ukv[:, h * hk + MLA_NOPE:(h + 1) * hk], jnp.zeros((MLA_KV_LORA, MLA_VT_ROWS - MLA_V), F32)]
    cat = lambda cols: jnp.concatenate(cols, axis=1).astype(BF16)
    w1 = w_out[:D_A].astype(BF16)
    w2 = w_out[D_A:D_A + D_B].astype(BF16)
    w3 = w_out[D_A + D_B:].astype(BF16)
    w_a = w_up[:, :D_FF].astype(BF16)
    w_g = w_up[:, D_FF:].astype(BF16)
    return (wa, wb, wc, cat(q_cols), cat(qr_cols), cat(k_cols), cat(v_cols).T, w1, w2, w3, w_a, w_g,
            w_down.astype(BF16))


def kernel(x, c, w_ada, b_ada, g_pre_mix, g_post_mix, g_pre_ffn, g_post_ffn, w_in, na_rpb, t5_table,
           mla_g_q, mla_g_kv, w_uq, w_ukv, w_out, w_up, conv_w, conv_b, w_down):
    bsz, seq, d = x.shape
    depth = w_in.shape[0]
    n = bsz * seq
    rows = seq // GRID_W
    assert d == D_MODEL and seq % (DIL_BLK * max(dd for _, dd in DIL_PATTERNS)) == 0 and rows >= NA_KH_MAX

    mod3 = _modulation(c, w_ada, b_ada).reshape(depth * bsz * N_MOD, 1, D_MODEL)
    cq, sq, ck, sk = _rope_tables(seq)
    dil_tbls = _dil_bias_tables(t5_table, seq)
    na_tbls = _na_bias_tables(na_rpb, rows)
    row2 = lambda v: v.reshape(1, -1)

    xf = x.reshape(n, d)
    for l in range(depth):
        (wa, wb, wc, wuq, wuqr, wuk, wuv, w1, w2, w3, w_a, w_g, w_d) = _layer_weights(
            w_in[l], w_uq[l], w_ukv[l], w_out[l], w_up[l], w_down[l])
        za, zb, qm, km, vt = _in_proj(xf, mod3, l, bsz, seq, row2(g_pre_mix[l]), wa, wb, wc,
                                      row2(mla_g_q[l]), row2(mla_g_kv[l]), wuq, wuqr, wuk, wuv,
                                      cq, sq, ck, sk)
        o_a = _na_attention(za.reshape(bsz, seq, 3 * D_A), na_tbls[l])
        o_b = _dil_attention(zb.reshape(bsz, seq, 3 * D_B), dil_tbls)
        o_c = _mla_attention(qm.reshape(bsz, seq, MLA_QK_W), km.reshape(bsz, seq, MLA_QK_W), vt)
        xf = _out_proj(xf, o_a.reshape(n, D_A), o_b.reshape(n, D_B), o_c.reshape(n, D_C),
                       w1, w2, w3, row2(g_post_mix[l]), mod3, l, bsz, seq)
        xf = _ffn(xf, mod3, l, bsz, seq, row2(g_pre_ffn[l]), w_a, w_g, conv_w[l], row2(conv_b[l]), w_d,
                  row2(g_post_ffn[l]))
    return xf.reshape(bsz, seq, d)
```

```python
import functools
import math

import numpy as np
import jax
import jax.numpy as jnp
from jax import lax
from jax.experimental import pallas as pl
from jax.experimental.pallas import tpu as pltpu

F32 = jnp.float32
BF16 = jnp.bfloat16

D_MODEL = 1024
HEAD_DIM = 64
GRID_W = 64
EPS = 1e-6
NEG_INF = -1e30

NA_HEADS = 4
NA_KH_MAX = 8
NA_KW = 16

DIL_HEADS = 8
DIL_PATTERNS = ((128, 1), (512, 4), (2048, 16))
DIL_BLK = 128
DIL_HALF = 64

MLA_HEADS = 4
MLA_Q_LORA = 384
MLA_KV_LORA = 256
MLA_NOPE = 64
MLA_ROPE = 32
MLA_V = 64
ROPE_THETA = 10000.0

T5_BUCKETS = 32
T5_MAX_DIST = 1024

D_FF = 2816
CONV_W = 3

D_A = NA_HEADS * HEAD_DIM
D_B = DIL_HEADS * HEAD_DIM
D_C = MLA_HEADS * MLA_V
N_MOD = 6

LANES = 128
SUBLANES = 8
VMEM_LIMIT = 56 * 1024 * 1024

LOG2E = math.log2(math.e)
Q_SCALE = HEAD_DIM ** -0.5 * LOG2E
MLA_Q_SCALE = (MLA_NOPE + MLA_ROPE) ** -0.5 * LOG2E

ZC_W = MLA_Q_LORA + MLA_KV_LORA + 2 * LANES
MLA_QK_W = MLA_HEADS * LANES


def _params(n_axes):
    return pltpu.CompilerParams(dimension_semantics=("arbitrary",) * n_axes,
                                vmem_limit_bytes=VMEM_LIMIT)


def _dot(a, b):
    return jnp.dot(a, b, preferred_element_type=F32)


def _dot_nt(a, b):
    return lax.dot_general(a, b, (((1,), (1,)), ((), ())), preferred_element_type=F32)


def _rms(x, g):
    return x * lax.rsqrt(jnp.mean(x * x, axis=-1, keepdims=True) + EPS) * g


def _mod_kernel(c_ref, w_ref, b_ref, o_ref):
    ca = jax.nn.silu(c_ref[...]).astype(BF16)
    o_ref[0] = _dot(ca, w_ref[0].astype(BF16)) + b_ref[0]


def _modulation(c, w_ada, b_ada):
    depth, d, n = w_ada.shape
    bsz = c.shape[0]
    tn = 1536
    return pl.pallas_call(
        _mod_kernel,
        grid=(depth, n // tn),
        in_specs=[pl.BlockSpec((bsz, d), lambda l, j: (0, 0)),
                  pl.BlockSpec((1, d, tn), lambda l, j: (l, 0, j)),
                  pl.BlockSpec((1, 1, tn), lambda l, j: (l, 0, j))],
        out_specs=pl.BlockSpec((1, bsz, tn), lambda l, j: (l, 0, j)),
        out_shape=jax.ShapeDtypeStruct((depth, bsz, n), F32),
        compiler_params=_params(2),
        name="adaln_mod",
    )(c, w_ada, b_ada.reshape(depth, 1, n))


def _in_kernel(x_ref, g_ref, sh_ref, sc_ref, wa_ref, wb_ref, wc_ref, gq_ref, gkv_ref,
               wuq_ref, wuqr_ref, wuk_ref, wuvt_ref, cq_ref, sq_ref, ck_ref, sk_ref,
               za_ref, zb_ref, qm_ref, km_ref, vt_ref):
    x = x_ref[...]
    h = (_rms(x, g_ref[...]) * (1.0 + sc_ref[...]) + sh_ref[...]).astype(BF16)
    za_ref[:, 0:D_A] = (_dot(h, wa_ref[:, 0:D_A]) * Q_SCALE).astype(BF16)
    za_ref[:, D_A:3 * D_A] = _dot(h, wa_ref[:, D_A:3 * D_A]).astype(BF16)
    zb_ref[:, 0:D_B] = _dot(h, wb_ref[:, 0:D_B]) * Q_SCALE
    for j in range(D_B, 3 * D_B, D_B):
        zb_ref[:, j:j + D_B] = _dot(h, wb_ref[:, j:j + D_B])
    zc = _dot(h, wc_ref[...])
    c_q = zc[:, 0:MLA_Q_LORA]
    c_kv = zc[:, MLA_Q_LORA:MLA_Q_LORA + MLA_KV_LORA]
    kr_a = zc[:, MLA_Q_LORA + MLA_KV_LORA:MLA_Q_LORA + MLA_KV_LORA + LANES]
    kr_b = zc[:, MLA_Q_LORA + MLA_KV_LORA + LANES:ZC_W]
    cqn = _rms(c_q, gq_ref[...]).astype(BF16)
    q = _dot(cqn, wuq_ref[...]) * cq_ref[...] + _dot(cqn, wuqr_ref[...]) * sq_ref[...]
    qm_ref[...] = (q * MLA_Q_SCALE).astype(BF16)
    ckvn = _rms(c_kv, gkv_ref[...]).astype(BF16)
    kr = kr_a * ck_ref[...] + kr_b * sk_ref[...]
    km_ref[...] = (_dot(ckvn, wuk_ref[...]) + jnp.concatenate([kr] * MLA_HEADS, axis=1)).astype(BF16)
    vrow = lax.broadcasted_iota(jnp.int32, (MLA_HEADS * MLA_VT_ROWS, 1), 0)
    ones_rows = functools.reduce(jnp.logical_or, [vrow == h * MLA_VT_ROWS + MLA_V for h in range(MLA_HEADS)])
    vt = _dot_nt(wuvt_ref[...], ckvn) + jnp.where(ones_rows, 1.0, 0.0)
    vt_ref[...] = vt.astype(BF16)


def _in_proj(x, mod3, l, bsz, seq, g_pre, wa, wb, wc, gq, gkv, wuq, wuqr, wuk, wuv, cq, sq, ck, sk):
    n = x.shape[0]
    tm = 512
    tps = seq // tm
    row = lambda i: (i, 0)
    const = lambda i: (0, 0)
    pos = lambda i: (i % tps, 0)
    mrow = lambda k: (lambda i: ((l * bsz + i // tps) * N_MOD + k, 0, 0))
    full = lambda a: pl.BlockSpec(a.shape, const)
    return pl.pallas_call(
        _in_kernel,
        grid=(n // tm,),
        in_specs=[pl.BlockSpec((tm, D_MODEL), row),
                  full(g_pre),
                  pl.BlockSpec((None, 1, D_MODEL), mrow(0)),
                  pl.BlockSpec((None, 1, D_MODEL), mrow(1)),
                  full(wa), full(wb), full(wc), full(gq), full(gkv),
                  full(wuq), full(wuqr), full(wuk), full(wuv),
                  pl.BlockSpec((tm, MLA_QK_W), pos), pl.BlockSpec((tm, MLA_QK_W), pos),
                  pl.BlockSpec((tm, LANES), pos), pl.BlockSpec((tm, LANES), pos)],
        out_specs=[pl.BlockSpec((tm, 3 * D_A), row),
                   pl.BlockSpec((tm, 3 * D_B), row),
                   pl.BlockSpec((tm, MLA_QK_W), row),
                   pl.BlockSpec((tm, MLA_QK_W), row),
                   pl.BlockSpec((None, MLA_HEADS * MLA_VT_ROWS, tm), lambda i: (i // tps, 0, i % tps))],
        out_shape=[jax.ShapeDtypeStruct((n, 3 * D_A), BF16),
                   jax.ShapeDtypeStruct((n, 3 * D_B), F32),
                   jax.ShapeDtypeStruct((n, MLA_QK_W), BF16),
                   jax.ShapeDtypeStruct((n, MLA_QK_W), BF16),
                   jax.ShapeDtypeStruct((bsz, MLA_HEADS * MLA_VT_ROWS, seq), BF16)],
        compiler_params=_params(1),
        name="in_proj",
    )(x, g_pre, mod3, mod3, wa, wb, wc, gq, gkv, wuq, wuqr, wuk, wuv, cq, sq, ck, sk)


NA_ROWS_PER_STEP = 8
NA_GROUP = 4


def _na_kernel(q_ref, k_ref, v_ref, tbl_ref, o_ref, *, rows):
    i = pl.program_id(1)
    lane = lax.broadcasted_iota(jnp.int32, (1, LANES), 1)
    low = lane < HEAD_DIM
    kh = NA_KH_MAX

    def row_group(j0, carry):
        items = []
        scores = []
        for g in range(NA_GROUP):
            j = j0 * NA_GROUP + g
            r = i * NA_ROWS_PER_STEP + j
            r0 = jnp.clip(r - kh // 2, 0, rows - kh)
            variant = r - r0
            qs = pl.multiple_of(j * GRID_W, GRID_W)
            ks = pl.multiple_of(r0 * GRID_W, GRID_W)
            for pair in range(NA_HEADS // 2):
                cols = slice(pair * LANES, (pair + 1) * LANES)
                q2 = q_ref[pl.ds(qs, GRID_W), cols]
                k2 = k_ref[pl.ds(ks, kh * GRID_W), cols]
                items.append((qs, ks, cols))
                zero = jnp.zeros_like(q2)
                q_st = jnp.concatenate([jnp.where(low, q2, zero), jnp.where(low, zero, q2)], axis=0)
                bias = tbl_ref[variant, 2 * pair:2 * pair + 2].reshape(2 * GRID_W, kh * GRID_W)
                scores.append(_dot_nt(q_st, k2) + bias)
        probs = []
        for s in scores:
            m = jnp.max(s, axis=-1, keepdims=True)
            p = jnp.exp2(s - m)
            probs.append((p.astype(BF16), jnp.sum(p, axis=-1, keepdims=True)))
        for (qs, ks, cols), (p, den) in zip(items, probs):
            o = _dot(p, v_ref[pl.ds(ks, kh * GRID_W), cols]) / den
            o_ref[pl.ds(qs, GRID_W), cols] = jnp.where(low, o[:GRID_W], o[GRID_W:]).astype(BF16)
        return carry

    lax.fori_loop(0, NA_ROWS_PER_STEP // NA_GROUP, row_group, 0)


def _na_attention(za3, tbl):
    bsz, seq, _ = za3.shape
    rows = seq // GRID_W
    tq = NA_ROWS_PER_STEP * GRID_W
    return pl.pallas_call(
        functools.partial(_na_kernel, rows=rows),
        grid=(bsz, seq // tq),
        in_specs=[pl.BlockSpec((None, tq, D_A), lambda b, i: (b, i, 0)),
                  pl.BlockSpec((None, seq, D_A), lambda b, i: (b, 0, 1)),
                  pl.BlockSpec((None, seq, D_A), lambda b, i: (b, 0, 2)),
                  pl.BlockSpec(tbl.shape, lambda b, i: (0, 0, 0, 0))],
        out_specs=pl.BlockSpec((None, tq, D_A), lambda b, i: (b, i, 0)),
        out_shape=jax.ShapeDtypeStruct((bsz, seq, D_A), BF16),
        compiler_params=_params(2),
        name="na_attn",
    )(za3, za3, za3, tbl)


def _na_bias_tables(rpb, rows):
    kh = min(NA_KH_MAX, rows)
    col = np.arange(GRID_W)
    c_start = np.clip(col - NA_KW // 2, 0, GRID_W - NA_KW)
    valid = (col[None, :] >= c_start[:, None]) & (col[None, :] < c_start[:, None] + NA_KW)
    d_col = col[None, :] - col[:, None] + (NA_KW - 1)
    onehot = (d_col[:, :, None] == np.arange(2 * NA_KW - 1)[None, None, :]) & valid[:, :, None]
    by_row = jnp.stack([rpb[:, :, NA_KH_MAX - 1 - v:NA_KH_MAX - 1 - v + kh, :] for v in range(kh)], axis=1)
    t = jnp.einsum("lvhib,cdb->lvhcid", by_row.astype(F32), jnp.asarray(onehot, F32),
                   precision=lax.Precision.HIGHEST)
    t = jnp.where(valid[None, None, None, :, None, :], t * LOG2E, NEG_INF)
    return t.reshape(rpb.shape[0], kh, NA_HEADS, GRID_W, kh * GRID_W)


DIL_PAD = DIL_HALF * max(d for _, d in DIL_PATTERNS)
DIL_KB = DIL_BLK + 2 * DIL_HALF
DIL_GROUP = 8
DIL_GROUP_WHOLE = 2
DIL_MERGE_ROWS = 512


def _dil_whole_class(seq, d):
    return seq // d <= 2 * DIL_BLK


def _dil_kernel(q_ref, k_ref, v_ref, *rest, seq):
    npat = len(DIL_PATTERNS)
    tbl_refs = rest[:npat]
    o_ref, kp_ref, vp_ref = rest[npat:npat + 3]
    stats = rest[npat + 3:]
    acc_refs, m_refs, l_refs = stats[:npat], stats[npat:2 * npat], stats[2 * npat:]

    zeros_pad = jnp.zeros((DIL_PAD, LANES), F32)
    kp_ref[0:DIL_PAD, :] = zeros_pad
    kp_ref[DIL_PAD + seq:DIL_PAD + seq + DIL_PAD, :] = zeros_pad
    vp_ref[0:DIL_PAD, :] = zeros_pad
    vp_ref[DIL_PAD + seq:DIL_PAD + seq + DIL_PAD, :] = zeros_pad
    kp_ref[DIL_PAD:DIL_PAD + seq, :] = k_ref[...]
    vp_ref[DIL_PAD:DIL_PAD + seq, :] = v_ref[...]

    lane = lax.broadcasted_iota(jnp.int32, (1, LANES), 1)
    low = lane < HEAD_DIM

    def attend_group(items, acc_ref, m_ref, l_ref):
        scores = []
        for _, q2, k2, _, bias in items:
            zero = jnp.zeros_like(q2)
            qs = jnp.concatenate([jnp.where(low, q2, zero), jnp.where(low, zero, q2)], axis=0)
            scores.append(_dot_nt(qs, k2) + bias())
        probs = []
        for s in scores:
            m = jnp.max(s, axis=-1, keepdims=True)
            p = jnp.exp2(s - m)
            probs.append((p.astype(BF16), m, jnp.sum(p, axis=-1, keepdims=True)))
        for (rows, q2, _, v2, _), (p, m, den) in zip(items, probs):
            nq = q2.shape[0]
            o = _dot(p, v2)
            acc_ref[rows, :] = jnp.where(low, o[:nq], o[nq:])
            m_ref[rows, :] = jnp.where(low, m[:nq], m[nq:])
            l_ref[rows, :] = jnp.where(low, den[:nq], den[nq:])

    for pat, (_, d) in enumerate(DIL_PATTERNS):
        length = seq // d
        tbl_ref, acc_ref, m_ref, l_ref = tbl_refs[pat], acc_refs[pat], m_refs[pat], l_refs[pat]

        if _dil_whole_class(seq, d):
            group = min(DIL_GROUP_WHOLE, d)

            def whole(g0, carry, d=d, length=length, group=group, tbl_ref=tbl_ref,
                      acc_ref=acc_ref, m_ref=m_ref, l_ref=l_ref):
                items = []
                for g in range(group):
                    r = g0 * group + g
                    rows = pl.ds(r, length, stride=d) if d > 1 else pl.ds(r, length)
                    q2 = q_ref[rows, :].astype(BF16)
                    items.append((rows, q2, k_ref[rows, :].astype(BF16), v_ref[rows, :].astype(BF16),
                                  lambda: tbl_ref[...].reshape(2 * length, length)))
                attend_group(items, acc_ref, m_ref, l_ref)
                return carry

            lax.fori_loop(0, d // group, whole, 0)
        else:
            nblk = length // DIL_BLK
            group = min(DIL_GROUP, nblk)

            def block(t0, carry, d=d, nblk=nblk, group=group, tbl_ref=tbl_ref,
                      acc_ref=acc_ref, m_ref=m_ref, l_ref=l_ref):
                items = []
                for g in range(group):
                    t = t0 * group + g
                    r = t // nblk
                    n = t % nblk
                    variant = jnp.where(n == 0, 1, 0) + jnp.where(n == nblk - 1, 2, 0)
                    q_start = r + d * DIL_BLK * n
                    k_start = DIL_PAD + r + d * (DIL_BLK * n - DIL_HALF)
                    if d == 1:
                        rows_q = pl.ds(q_start, DIL_BLK)
                        rows_k = pl.ds(k_start, DIL_KB)
                    else:
                        rows_q = pl.ds(q_start, DIL_BLK, stride=d)
                        rows_k = pl.ds(k_start, DIL_KB, stride=d)
                    q2 = q_ref[rows_q, :].astype(BF16)
                    items.append((rows_q, q2, kp_ref[rows_k, :].astype(BF16), vp_ref[rows_k, :].astype(BF16),
                                  lambda variant=variant: tbl_ref[variant].reshape(2 * DIL_BLK, DIL_KB)))
                attend_group(items, acc_ref, m_ref, l_ref)
                return carry

            lax.fori_loop(0, d * nblk // group, block, 0)

    def merge(c, carry):
        rows = pl.ds(pl.multiple_of(c * DIL_MERGE_ROWS, DIL_MERGE_ROWS), DIL_MERGE_ROWS)
        ms = [m_ref[rows, :] for m_ref in m_refs]
        m_all = functools.reduce(jnp.maximum, ms)
        num = jnp.zeros((DIL_MERGE_ROWS, LANES), F32)
        den = jnp.zeros((DIL_MERGE_ROWS, LANES), F32)
        for m_i, acc_ref, l_ref in zip(ms, acc_refs, l_refs):
            e = jnp.exp2(m_i - m_all)
            num = num + acc_ref[rows, :] * e
            den = den + l_ref[rows, :] * e
        o_ref[rows, :] = (num / den).astype(BF16)
        return carry

    lax.fori_loop(0, seq // DIL_MERGE_ROWS, merge, 0)


def _dil_attention(zb3, tbls):
    bsz, seq, _ = zb3.shape
    npair = DIL_HEADS // 2
    blk = lambda off: pl.BlockSpec((None, seq, LANES), lambda b, p: (b, 0, off + p))
    tbl_specs = []
    for t in tbls:
        if t.ndim == 3:
            tbl_specs.append(pl.BlockSpec((2,) + t.shape[1:], lambda b, p: (p, 0, 0)))
        else:
            tbl_specs.append(pl.BlockSpec((t.shape[0], 2) + t.shape[2:], lambda b, p: (0, p, 0, 0)))
    stat = pltpu.VMEM((seq, LANES), F32)
    return pl.pallas_call(
        functools.partial(_dil_kernel, seq=seq),
        grid=(bsz, npair),
        in_specs=[blk(0), blk(npair), blk(2 * npair)] + tbl_specs,
        out_specs=pl.BlockSpec((None, seq, LANES), lambda b, p: (b, 0, p)),
        out_shape=jax.ShapeDtypeStruct((bsz, seq, D_B), BF16),
        scratch_shapes=[pltpu.VMEM((seq + 2 * DIL_PAD, LANES), F32),
                        pltpu.VMEM((seq + 2 * DIL_PAD, LANES), F32)] + [stat] * (3 * len(DIL_PATTERNS)),
        compiler_params=_params(2),
        name="dil_attn",
    )(zb3, zb3, zb3, *tbls)


def _t5_bucket(rel):
    nb = T5_BUCKETS // 2
    max_exact = nb // 2
    n = np.abs(rel)
    large = max_exact + (np.log(np.maximum(n, 1) / max_exact)
                         / math.log(T5_MAX_DIST / max_exact) * (nb - max_exact)).astype(np.int64)
    large = np.minimum(large, nb - 1)
    return (np.where(rel > 0, nb, 0) + np.where(n < max_exact, n, large)).astype(np.int32)


def _t5_bias(t5_table, rel):
    onehot = _t5_bucket(rel)[:, :, None] == np.arange(T5_BUCKETS)[None, None, :]
    return jnp.einsum("qkb,bh->hqk", jnp.asarray(onehot, F32), t5_table.astype(F32),
                      precision=lax.Precision.HIGHEST)


def _dil_bias_tables(t5_table, seq):
    tables = []
    for _, d in DIL_PATTERNS:
        if _dil_whole_class(seq, d):
            pos = np.arange(seq // d)
            rel = pos[None, :] - pos[:, None]
            valid = np.abs(rel) <= DIL_HALF
            bias = _t5_bias(t5_table, rel * d)
            tables.append(jnp.where(valid[None], bias * LOG2E, NEG_INF))
        else:
            a = np.arange(DIL_BLK)[:, None]
            j = np.arange(DIL_KB)[None, :]
            rel = j - DIL_HALF - a
            band = np.abs(rel) <= DIL_HALF
            not_before = np.broadcast_to(j >= DIL_HALF, band.shape)
            not_after = np.broadcast_to(j < DIL_BLK + DIL_HALF, band.shape)
            valid = np.stack([band, band & not_before, band & not_after, band & not_before & not_after])
            bias = _t5_bias(t5_table, rel * d)
            tables.append(jnp.where(valid[:, None], bias[None] * LOG2E, NEG_INF))
    return tables


MLA_TQ = 1024
MLA_CQ = 512
MLA_VT_ROWS = 80


def _mla_kernel(q_ref, k_ref, vt_ref, o_ref):
    chains = [(h, c) for c in range(MLA_TQ // MLA_CQ) for h in range(MLA_HEADS)]

    def scores(h, c):
        cols = slice(h * LANES, (h + 1) * LANES)
        return _dot_nt(k_ref[:, cols], q_ref[c * MLA_CQ:(c + 1) * MLA_CQ, cols])

    outs = []
    s_next = scores(*chains[0])
    for idx, (h, c) in enumerate(chains):
        s = s_next
        if idx + 1 < len(chains):
            s_next = scores(*chains[idx + 1])
        sb = s.astype(BF16)
        p = jnp.exp2(sb - jnp.max(sb, axis=0, keepdims=True))
        ov = _dot(vt_ref[h * MLA_VT_ROWS:(h + 1) * MLA_VT_ROWS, :], p)
        outs.append(ov[0:MLA_V] / ov[MLA_V:MLA_V + 1])
        if h == MLA_HEADS - 1:
            o_ref[c * MLA_CQ:(c + 1) * MLA_CQ, :] = jnp.concatenate(outs, axis=0).T.astype(BF16)
            outs = []


def _mla_attention(qm3, km3, vt3):
    bsz, seq, _ = qm3.shape
    return pl.pallas_call(
        _mla_kernel,
        grid=(bsz, seq // MLA_TQ),
        in_specs=[pl.BlockSpec((None, MLA_TQ, MLA_QK_W), lambda b, i: (b, i, 0)),
                  pl.BlockSpec((None, seq, MLA_QK_W), lambda b, i: (b, 0, 0), pipeline_mode=pl.Buffered(1)),
                  pl.BlockSpec((None, MLA_HEADS * MLA_VT_ROWS, seq), lambda b, i: (b, 0, 0),
                               pipeline_mode=pl.Buffered(1))],
        out_specs=pl.BlockSpec((None, MLA_TQ, D_C), lambda b, i: (b, i, 0)),
        out_shape=jax.ShapeDtypeStruct((bsz, seq, D_C), BF16),
        compiler_params=_params(2),
        name="mla_attn",
    )(qm3, km3, vt3)


def _out_kernel(x_ref, oa_ref, ob_ref, oc_ref, w1_ref, w2_ref, w3_ref, g_ref, gate_ref, o_ref):
    y = _dot(oa_ref[...], w1_ref[...]) + _dot(ob_ref[...], w2_ref[...]) + _dot(oc_ref[...], w3_ref[...])
    o_ref[...] = x_ref[...] + gate_ref[...] * _rms(y, g_ref[...])


def _out_proj(x, oa, ob, oc, w1, w2, w3, g_post, mod3, l, bsz, seq):
    n = x.shape[0]
    tm = 1024
    tps = seq // tm
    row = lambda i: (i, 0)
    const = lambda i: (0, 0)
    full = lambda a: pl.BlockSpec(a.shape, const)
    return pl.pallas_call(
        _out_kernel,
        grid=(n // tm,),
        in_specs=[pl.BlockSpec((tm, D_MODEL), row),
                  pl.BlockSpec((tm, D_A), row), pl.BlockSpec((tm, D_B), row), pl.BlockSpec((tm, D_C), row),
                  full(w1), full(w2), full(w3), full(g_post),
                  pl.BlockSpec((None, 1, D_MODEL), lambda i: ((l * bsz + i // tps) * N_MOD + 2, 0, 0))],
        out_specs=pl.BlockSpec((tm, D_MODEL), row),
        out_shape=jax.ShapeDtypeStruct((n, D_MODEL), F32),
        compiler_params=_params(1),
        name="out_proj",
    )(x, oa, ob, oc, w1, w2, w3, g_post, mod3)


FFN_TM = 512
FFN_CHUNK = 256


def _ffn_kernel(x_ref, xp_ref, xn_ref, g_ref, sh_ref, sc_ref, wa_ref, wg_ref, cw_ref, cb_ref, wd_ref,
                gpost_ref, gate_ref, o_ref, u_ref, *, tiles_per_seq):
    i = pl.program_id(0)
    has_prev = (i % tiles_per_seq) != 0
    has_next = (i % tiles_per_seq) != tiles_per_seq - 1
    g = g_ref[...]
    sc = 1.0 + sc_ref[...]
    sh = sh_ref[...]
    x = x_ref[...]
    pre = lambda t: _rms(t, g) * sc + sh
    h_prev = jnp.where(has_prev, pre(xp_ref[...]), 0.0)
    h_next = jnp.where(has_next, pre(xn_ref[...]), 0.0)
    h_mid = pre(x)
    h = h_mid.astype(BF16)
    h_ext = jnp.concatenate([h_prev, h_mid, h_next], axis=0).astype(BF16)
    tm = x.shape[0]
    ext = tm + 2 * SUBLANES
    mid = slice(SUBLANES, SUBLANES + tm)
    for c0 in range(0, D_FF, FFN_CHUNK):
        cols = slice(c0, c0 + FFN_CHUNK)
        a = _dot(h, wa_ref[:, cols])
        ge = _dot(h_ext, wg_ref[:, cols])
        g_prev = pltpu.roll(ge, 1, axis=0)[mid]
        g_next = pltpu.roll(ge, ext - 1, axis=0)[mid]
        gc = cb_ref[:, cols] + g_prev * cw_ref[0:1, cols]
        gc = gc + ge[mid] * cw_ref[1:2, cols]
        gc = gc + g_next * cw_ref[2:3, cols]
        u_ref[:, cols] = (jax.nn.gelu(gc) * a).astype(BF16)
    y = _dot(u_ref[...], wd_ref[...])
    o_ref[...] = x + gate_ref[...] * _rms(y, gpost_ref[...])


def _ffn(x, mod3, l, bsz, seq, g_pre, wa, wg, cw, cb, wd, g_post):
    n = x.shape[0]
    tm = FFN_TM
    tps = seq // tm
    r8 = tm // SUBLANES
    last8 = n // SUBLANES - 1
    const = lambda i: (0, 0)
    full = lambda a: pl.BlockSpec(a.shape, const)
    weight = lambda a: pl.BlockSpec(a.shape, const, pipeline_mode=pl.Buffered(1))
    mrow = lambda k: (lambda i: ((l * bsz + i // tps) * N_MOD + k, 0, 0))
    return pl.pallas_call(
        functools.partial(_ffn_kernel, tiles_per_seq=tps),
        grid=(n // tm,),
        in_specs=[pl.BlockSpec((tm, D_MODEL), lambda i: (i, 0)),
                  pl.BlockSpec((SUBLANES, D_MODEL), lambda i: (jnp.maximum(i * r8 - 1, 0), 0)),
                  pl.BlockSpec((SUBLANES, D_MODEL), lambda i: (jnp.minimum((i + 1) * r8, last8), 0)),
                  full(g_pre),
                  pl.BlockSpec((None, 1, D_MODEL), mrow(3)),
                  pl.BlockSpec((None, 1, D_MODEL), mrow(4)),
                  weight(wa), weight(wg), full(cw), full(cb), weight(wd), full(g_post),
                  pl.BlockSpec((None, 1, D_MODEL), mrow(5))],
        out_specs=pl.BlockSpec((tm, D_MODEL), lambda i: (i, 0)),
        out_shape=jax.ShapeDtypeStruct((n, D_MODEL), F32),
        scratch_shapes=[pltpu.VMEM((tm, D_FF), BF16)],
        compiler_params=_params(1),
        name="conv_ffn",
    )(x, x, x, g_pre, mod3, mod3, wa, wg, cw, cb, wd, g_post, mod3)


def _rope_tables(seq):
    inv_freq = jnp.asarray(ROPE_THETA ** (-np.arange(0, MLA_ROPE, 2, dtype=np.float32) / MLA_ROPE), F32)
    ang = jnp.arange(seq, dtype=F32)[:, None] * inv_freq[None, :]
    cos, sin = jnp.cos(ang), jnp.sin(ang)
    cos2 = jnp.concatenate([cos, cos], axis=1)
    sin2 = jnp.concatenate([sin, sin], axis=1)
    head_c = jnp.concatenate([cos2, jnp.ones((seq, MLA_NOPE), F32), jnp.zeros((seq, LANES - MLA_NOPE - MLA_ROPE), F32)], axis=1)
    head_s = jnp.concatenate([sin2, jnp.zeros((seq, LANES - MLA_ROPE), F32)], axis=1)
    cq = jnp.concatenate([head_c] * MLA_HEADS, axis=1)
    sq = jnp.concatenate([head_s] * MLA_HEADS, axis=1)
    ck = jnp.concatenate([cos2, jnp.zeros((seq, LANES - MLA_ROPE), F32)], axis=1)
    return cq, sq, ck, head_s


def _rot_half_cols(w):
    half = w.shape[1] // 2
    return jnp.concatenate([-w[:, half:], w[:, :half]], axis=1)


def _layer_weights(w_in, w_uq, w_ukv, w_out, w_up, w_down):
    d = w_in.shape[0]
    o_b = 3 * D_A
    o_c = o_b + 3 * D_B
    o_kr = o_c + MLA_Q_LORA + MLA_KV_LORA
    wa = w_in[:, :o_b].astype(BF16)
    wb = w_in[:, o_b:o_c].astype(BF16)
    w_kr = w_in[:, o_kr:o_kr + MLA_ROPE]
    zpad = jnp.zeros((d, LANES - MLA_ROPE), F32)
    wc = jnp.concatenate([w_in[:, o_c:o_kr], w_kr, zpad, _rot_half_cols(w_kr), zpad], axis=1).astype(BF16)

    hq = MLA_NOPE + MLA_ROPE
    q_cols, qr_cols, k_cols, v_cols = [], [], [], []
    zq = jnp.zeros((MLA_Q_LORA, LANES - hq), F32)
    for h in range(MLA_HEADS):
        nope = w_uq[:, h * hq:h * hq + MLA_NOPE]
        rope = w_uq[:, h * hq + MLA_NOPE:(h + 1) * hq]
        q_cols += [rope, nope, zq]
        qr_cols += [_rot_half_cols(rope), jnp.zeros((MLA_Q_LORA, LANES - MLA_ROPE), F32)]
        hk = MLA_NOPE + MLA_V
        k_cols += [jnp.zeros((MLA_KV_LORA, MLA_ROPE), F32), w_ukv[:, h * hk:h * hk + MLA_NOPE],
                   jnp.zeros((MLA_KV_LORA, LANES - hq), F32)]
        v_cols += [w_ukv[:, h * hk + MLA_NOPE:(h + 1) * hk], jnp.zeros((MLA_KV_LORA, MLA_VT_ROWS - MLA_V), F32)]
    cat = lambda cols: jnp.concatenate(cols, axis=1).astype(BF16)
    w1 = w_out[:D_A].astype(BF16)
    w2 = w_out[D_A:D_A + D_B].astype(BF16)
    w3 = w_out[D_A + D_B:].astype(BF16)
    w_a = w_up[:, :D_FF].astype(BF16)
    w_g = w_up[:, D_FF:].astype(BF16)
    return (wa, wb, wc, cat(q_cols), cat(qr_cols), cat(k_cols), cat(v_cols).T, w1, w2, w3, w_a, w_g,
            w_down.astype(BF16))


def kernel(x, c, w_ada, b_ada, g_pre_mix, g_post_mix, g_pre_ffn, g_post_ffn, w_in, na_rpb, t5_table,
           mla_g_q, mla_g_kv, w_uq, w_ukv, w_out, w_up, conv_w, conv_b, w_down):
    bsz, seq, d = x.shape
    depth = w_in.shape[0]
    n = bsz * seq
    rows = seq // GRID_W
    assert d == D_MODEL and seq % (DIL_BLK * max(dd for _, dd in DIL_PATTERNS)) == 0 and rows >= NA_KH_MAX

    mod3 = _modulation(c, w_ada, b_ada).reshape(depth * bsz * N_MOD, 1, D_MODEL)
    cq, sq, ck, sk = _rope_tables(seq)
    dil_tbls = _dil_bias_tables(t5_table, seq)
    na_tbls = _na_bias_tables(na_rpb, rows)
    row2 = lambda v: v.reshape(1, -1)

    xf = x.reshape(n, d)
    for l in range(depth):
        (wa, wb, wc, wuq, wuqr, wuk, wuv, w1, w2, w3, w_a, w_g, w_d) = _layer_weights(
            w_in[l], w_uq[l], w_ukv[l], w_out[l], w_up[l], w_down[l])
        za, zb, qm, km, vt = _in_proj(xf, mod3, l, bsz, seq, row2(g_pre_mix[l]), wa, wb, wc,
                                      row2(mla_g_q[l]), row2(mla_g_kv[l]), wuq, wuqr, wuk, wuv,
                                      cq, sq, ck, sk)
        o_a = _na_attention(za.reshape(bsz, seq, 3 * D_A), na_tbls[l])
        o_b = _dil_attention(zb.reshape(bsz, seq, 3 * D_B), dil_tbls)
        o_c = _mla_attention(qm.reshape(bsz, seq, MLA_QK_W), km.reshape(bsz, seq, MLA_QK_W), vt)
        xf = _out_proj(xf, o_a.reshape(n, D_A), o_b.reshape(n, D_B), o_c.reshape(n, D_C),
                       w1, w2, w3, row2(g_post_mix[l]), mod3, l, bsz, seq)
        xf = _ffn(xf, mod3, l, bsz, seq, row2(g_pre_ffn[l]), w_a, w_g, conv_w[l], row2(conv_b[l]), w_d,
                  row2(g_post_ffn[l]))
    return xf.reshape(bsz, seq, d)
```

```python
import functools
import math

import numpy as np
import jax
import jax.numpy as jnp
from jax import lax
from jax.experimental import pallas as pl
from jax.experimental.pallas import tpu as pltpu

F32 = jnp.float32
BF16 = jnp.bfloat16

D_MODEL = 1024
HEAD_DIM = 64
GRID_W = 64
EPS = 1e-6
NEG_INF = -1e30

NA_HEADS = 4
NA_KH_MAX = 8
NA_KW = 16

DIL_HEADS = 8
DIL_PATTERNS = ((128, 1), (512, 4), (2048, 16))
DIL_BLK = 128
DIL_HALF = 64

MLA_HEADS = 4
MLA_Q_LORA = 384
MLA_KV_LORA = 256
MLA_NOPE = 64
MLA_ROPE = 32
MLA_V = 64
ROPE_THETA = 10000.0

T5_BUCKETS = 32
T5_MAX_DIST = 1024

D_FF = 2816
CONV_W = 3

D_A = NA_HEADS * HEAD_DIM
D_B = DIL_HEADS * HEAD_DIM
D_C = MLA_HEADS * MLA_V
N_MOD = 6

LANES = 128
SUBLANES = 8
VMEM_LIMIT = 56 * 1024 * 1024

LOG2E = math.log2(math.e)
Q_SCALE = HEAD_DIM ** -0.5 * LOG2E
MLA_Q_SCALE = (MLA_NOPE + MLA_ROPE) ** -0.5 * LOG2E

ZC_W = MLA_Q_LORA + MLA_KV_LORA + 2 * LANES
MLA_QK_W = MLA_HEADS * LANES


def _params(n_axes):
    return pltpu.CompilerParams(dimension_semantics=("arbitrary",) * n_axes,
                                vmem_limit_bytes=VMEM_LIMIT)


def _dot(a, b):
    return jnp.dot(a, b, preferred_element_type=F32)


def _dot_nt(a, b):
    return lax.dot_general(a, b, (((1,), (1,)), ((), ())), preferred_element_type=F32)


def _rms(x, g):
    return x * lax.rsqrt(jnp.mean(x * x, axis=-1, keepdims=True) + EPS) * g


def _mod_kernel(c_ref, w_ref, b_ref, o_ref):
    ca = jax.nn.silu(c_ref[...]).astype(BF16)
    o_ref[0] = _dot(ca, w_ref[0].astype(BF16)) + b_ref[0]


def _modulation(c, w_ada, b_ada):
    depth, d, n = w_ada.shape
    bsz = c.shape[0]
    tn = 1536
    return pl.pallas_call(
        _mod_kernel,
        grid=(depth, n // tn),
        in_specs=[pl.BlockSpec((bsz, d), lambda l, j: (0, 0)),
                  pl.BlockSpec((1, d, tn), lambda l, j: (l, 0, j)),
                  pl.BlockSpec((1, 1, tn), lambda l, j: (l, 0, j))],
        out_specs=pl.BlockSpec((1, bsz, tn), lambda l, j: (l, 0, j)),
        out_shape=jax.ShapeDtypeStruct((depth, bsz, n), F32),
        compiler_params=_params(2),
        name="adaln_mod",
    )(c, w_ada, b_ada.reshape(depth, 1, n))


def _in_kernel(x_ref, g_ref, sh_ref, sc_ref, wa_ref, wb_ref, wc_ref, gq_ref, gkv_ref,
               wuq_ref, wuqr_ref, wuk_ref, wuvt_ref, cq_ref, sq_ref, ck_ref, sk_ref,
               za_ref, zb_ref, qm_ref, km_ref, vt_ref):
    x = x_ref[...]
    h = (_rms(x, g_ref[...]) * (1.0 + sc_ref[...]) + sh_ref[...]).astype(BF16)
    za_ref[:, 0:D_A] = (_dot(h, wa_ref[:, 0:D_A]) * Q_SCALE).astype(BF16)
    za_ref[:, D_A:3 * D_A] = _dot(h, wa_ref[:, D_A:3 * D_A]).astype(BF16)
    zb_ref[:, 0:D_B] = _dot(h, wb_ref[:, 0:D_B]) * Q_SCALE
    for j in range(D_B, 3 * D_B, D_B):
        zb_ref[:, j:j + D_B] = _dot(h, wb_ref[:, j:j + D_B])
    zc = _dot(h, wc_ref[...])
    c_q = zc[:, 0:MLA_Q_LORA]
    c_kv = zc[:, MLA_Q_LORA:MLA_Q_LORA + MLA_KV_LORA]
    kr_a = zc[:, MLA_Q_LORA + MLA_KV_LORA:MLA_Q_LORA + MLA_KV_LORA + LANES]
    kr_b = zc[:, MLA_Q_LORA + MLA_KV_LORA + LANES:ZC_W]
    cqn = _rms(c_q, gq_ref[...]).astype(BF16)
    q = _dot(cqn, wuq_ref[...]) * cq_ref[...] + _dot(cqn, wuqr_ref[...]) * sq_ref[...]
    qm_ref[...] = (q * MLA_Q_SCALE).astype(BF16)
    ckvn = _rms(c_kv, gkv_ref[...]).astype(BF16)
    kr = kr_a * ck_ref[...] + kr_b * sk_ref[...]
    km_ref[...] = (_dot(ckvn, wuk_ref[...]) + jnp.concatenate([kr] * MLA_HEADS, axis=1)).astype(BF16)
    vrow = lax.broadcasted_iota(jnp.int32, (MLA_HEADS * MLA_VT_ROWS, 1), 0)
    ones_rows = functools.reduce(jnp.logical_or, [vrow == h * MLA_VT_ROWS + MLA_V for h in range(MLA_HEADS)])
    vt = _dot_nt(wuvt_ref[...], ckvn) + jnp.where(ones_rows, 1.0, 0.0)
    vt_ref[...] = vt.astype(BF16)


def _in_proj(x, mod3, l, bsz, seq, g_pre, wa, wb, wc, gq, gkv, wuq, wuqr, wuk, wuv, cq, sq, ck, sk):
    n = x.shape[0]
    tm = 512
    tps = seq // tm
    row = lambda i: (i, 0)
    const = lambda i: (0, 0)
    pos = lambda i: (i % tps, 0)
    mrow = lambda k: (lambda i: ((l * bsz + i // tps) * N_MOD + k, 0, 0))
    full = lambda a: pl.BlockSpec(a.shape, const)
    return pl.pallas_call(
        _in_kernel,
        grid=(n // tm,),
        in_specs=[pl.BlockSpec((tm, D_MODEL), row),
                  full(g_pre),
                  pl.BlockSpec((None, 1, D_MODEL), mrow(0)),
                  pl.BlockSpec((None, 1, D_MODEL), mrow(1)),
                  full(wa), full(wb), full(wc), full(gq), full(gkv),
                  full(wuq), full(wuqr), full(wuk), full(wuv),
                  pl.BlockSpec((tm, MLA_QK_W), pos), pl.BlockSpec((tm, MLA_QK_W), pos),
                  pl.BlockSpec((tm, LANES), pos), pl.BlockSpec((tm, LANES), pos)],
        out_specs=[pl.BlockSpec((tm, 3 * D_A), row),
                   pl.BlockSpec((tm, 3 * D_B), row),
                   pl.BlockSpec((tm, MLA_QK_W), row),
                   pl.BlockSpec((tm, MLA_QK_W), row),
                   pl.BlockSpec((None, MLA_HEADS * MLA_VT_ROWS, tm), lambda i: (i // tps, 0, i % tps))],
        out_shape=[jax.ShapeDtypeStruct((n, 3 * D_A), BF16),
                   jax.ShapeDtypeStruct((n, 3 * D_B), F32),
                   jax.ShapeDtypeStruct((n, MLA_QK_W), BF16),
                   jax.ShapeDtypeStruct((n, MLA_QK_W), BF16),
                   jax.ShapeDtypeStruct((bsz, MLA_HEADS * MLA_VT_ROWS, seq), BF16)],
        compiler_params=_params(1),
        name="in_proj",
    )(x, g_pre, mod3, mod3, wa, wb, wc, gq, gkv, wuq, wuqr, wuk, wuv, cq, sq, ck, sk)


NA_ROWS_PER_STEP = 8
NA_GROUP = 4


def _na_kernel(q_ref, k_ref, v_ref, tbl_ref, o_ref, *, rows):
    i = pl.program_id(1)
    lane = lax.broadcasted_iota(jnp.int32, (1, LANES), 1)
    low = lane < HEAD_DIM
    kh = NA_KH_MAX

    def row_group(j0, carry):
        items = []
        scores = []
        for g in range(NA_GROUP):
            j = j0 * NA_GROUP + g
            r = i * NA_ROWS_PER_STEP + j
            r0 = jnp.clip(r - kh // 2, 0, rows - kh)
            variant = r - r0
            qs = pl.multiple_of(j * GRID_W, GRID_W)
            ks = pl.multiple_of(r0 * GRID_W, GRID_W)
            for pair in range(NA_HEADS // 2):
                cols = slice(pair * LANES, (pair + 1) * LANES)
                q2 = q_ref[pl.ds(qs, GRID_W), cols]
                k2 = k_ref[pl.ds(ks, kh * GRID_W), cols]
                items.append((qs, ks, cols))
                zero = jnp.zeros_like(q2)
                q_st = jnp.concatenate([jnp.where(low, q2, zero), jnp.where(low, zero, q2)], axis=0)
                bias = tbl_ref[variant, 2 * pair:2 * pair + 2].reshape(2 * GRID_W, kh * GRID_W)
                scores.append(_dot_nt(q_st, k2) + bias)
        probs = []
        for s in scores:
            m = jnp.max(s, axis=-1, keepdims=True)
            p = jnp.exp2(s - m)
            probs.append((p.astype(BF16), jnp.sum(p, axis=-1, keepdims=True)))
        for (qs, ks, cols), (p, den) in zip(items, probs):
            o = _dot(p, v_ref[pl.ds(ks, kh * GRID_W), cols]) / den
            o_ref[pl.ds(qs, GRID_W), cols] = jnp.where(low, o[:GRID_W], o[GRID_W:]).astype(BF16)
        return carry

    lax.fori_loop(0, NA_ROWS_PER_STEP // NA_GROUP, row_group, 0)


def _na_attention(za3, tbl):
    bsz, seq, _ = za3.shape
    rows = seq // GRID_W
    tq = NA_ROWS_PER_STEP * GRID_W
    return pl.pallas_call(
        functools.partial(_na_kernel, rows=rows),
        grid=(bsz, seq // tq),
        in_specs=[pl.BlockSpec((None, tq, D_A), lambda b, i: (b, i, 0)),
                  pl.BlockSpec((None, seq, D_A), lambda b, i: (b, 0, 1)),
                  pl.BlockSpec((None, seq, D_A), lambda b, i: (b, 0, 2)),
                  pl.BlockSpec(tbl.shape, lambda b, i: (0, 0, 0, 0))],
        out_specs=pl.BlockSpec((None, tq, D_A), lambda b, i: (b, i, 0)),
        out_shape=jax.ShapeDtypeStruct((bsz, seq, D_A), BF16),
        compiler_params=_params(2),
        name="na_attn",
    )(za3, za3, za3, tbl)


def _na_bias_tables(rpb, rows):
    kh = min(NA_KH_MAX, rows)
    col = np.arange(GRID_W)
    c_start = np.clip(col - NA_KW // 2, 0, GRID_W - NA_KW)
    valid = (col[None, :] >= c_start[:, None]) & (col[None, :] < c_start[:, None] + NA_KW)
    d_col = col[None, :] - col[:, None] + (NA_KW - 1)
    onehot = (d_col[:, :, None] == np.arange(2 * NA_KW - 1)[None, None, :]) & valid[:, :, None]
    by_row = jnp.stack([rpb[:, :, NA_KH_MAX - 1 - v:NA_KH_MAX - 1 - v + kh, :] for v in range(kh)], axis=1)
    t = jnp.einsum("lvhib,cdb->lvhcid", by_row.astype(F32), jnp.asarray(onehot, F32),
                   precision=lax.Precision.HIGHEST)
    t = jnp.where(valid[None, None, None, :, None, :], t * LOG2E, NEG_INF)
    return t.reshape(rpb.shape[0], kh, NA_HEADS, GRID_W, kh * GRID_W)


DIL_KB = DIL_BLK + 2 * DIL_HALF
DIL_GROUP = 8
DIL_GROUP_WHOLE = 2
DIL_MERGE_ROWS = 512
DIL_WINDOW_SHIFT = (DIL_HALF, 0, 2 * DIL_HALF)


def _dil_whole_class(seq, d):
    return seq // d <= 2 * DIL_BLK


def _dil_kernel(q_ref, k_ref, v_ref, *rest, seq):
    npat = len(DIL_PATTERNS)
    tbl_refs = rest[:npat]
    o_ref = rest[npat]
    stats = rest[npat + 1:]
    acc_refs, m_refs, l_refs = stats[:npat], stats[npat:2 * npat], stats[2 * npat:]

    lane = lax.broadcasted_iota(jnp.int32, (1, LANES), 1)
    low = lane < HEAD_DIM

    def attend_group(items, acc_ref, m_ref, l_ref):
        scores = []
        for _, q2, k2, _, bias in items:
            zero = jnp.zeros_like(q2)
            qs = jnp.concatenate([jnp.where(low, q2, zero), jnp.where(low, zero, q2)], axis=0)
            scores.append(_dot_nt(qs, k2) + bias())
        probs = []
        for s in scores:
            m = jnp.max(s, axis=-1, keepdims=True)
            p = jnp.exp2(s - m)
            probs.append((p.astype(BF16), m, jnp.sum(p, axis=-1, keepdims=True)))
        for (rows, q2, _, v2, _), (p, m, den) in zip(items, probs):
            nq = q2.shape[0]
            o = _dot(p, v2)
            acc_ref[rows, :] = jnp.where(low, o[:nq], o[nq:])
            m_ref[rows, :] = jnp.where(low, m[:nq], m[nq:])
            l_ref[rows, :] = jnp.where(low, den[:nq], den[nq:])

    for pat, (_, d) in enumerate(DIL_PATTERNS):
        length = seq // d
        tbl_ref, acc_ref, m_ref, l_ref = tbl_refs[pat], acc_refs[pat], m_refs[pat], l_refs[pat]

        if _dil_whole_class(seq, d):
            group = min(DIL_GROUP_WHOLE, d)

            def whole(g0, carry, d=d, length=length, group=group, tbl_ref=tbl_ref,
                      acc_ref=acc_ref, m_ref=m_ref, l_ref=l_ref):
                items = []
                for g in range(group):
                    r = g0 * group + g
                    rows = pl.ds(r, length, stride=d) if d > 1 else pl.ds(r, length)
                    q2 = q_ref[rows, :].astype(BF16)
                    items.append((rows, q2, k_ref[rows, :].astype(BF16), v_ref[rows, :].astype(BF16),
                                  lambda: tbl_ref[...].reshape(2 * length, length)))
                attend_group(items, acc_ref, m_ref, l_ref)
                return carry

            lax.fori_loop(0, d // group, whole, 0)
        else:
            nblk = length // DIL_BLK
            group = min(DIL_GROUP, nblk)

            def block(t0, carry, d=d, length=length, nblk=nblk, group=group, tbl_ref=tbl_ref,
                      acc_ref=acc_ref, m_ref=m_ref, l_ref=l_ref):
                items = []
                for g in range(group):
                    t = t0 * group + g
                    r = t // nblk
                    n = t % nblk
                    variant = jnp.where(n == 0, 1, 0) + jnp.where(n == nblk - 1, 2, 0)
                    q_start = r + d * DIL_BLK * n
                    k_start = r + d * jnp.clip(DIL_BLK * n - DIL_HALF, 0, length - DIL_KB)
                    if d == 1:
                        rows_q = pl.ds(q_start, DIL_BLK)
                        rows_k = pl.ds(k_start, DIL_KB)
                    else:
                        rows_q = pl.ds(q_start, DIL_BLK, stride=d)
                        rows_k = pl.ds(k_start, DIL_KB, stride=d)
                    q2 = q_ref[rows_q, :].astype(BF16)
                    items.append((rows_q, q2, k_ref[rows_k, :].astype(BF16), v_ref[rows_k, :].astype(BF16),
                                  lambda variant=variant: tbl_ref[variant].reshape(2 * DIL_BLK, DIL_KB)))
                attend_group(items, acc_ref, m_ref, l_ref)
                return carry

            lax.fori_loop(0, d * nblk // group, block, 0)

    def merge(c, carry):
        rows = pl.ds(pl.multiple_of(c * DIL_MERGE_ROWS, DIL_MERGE_ROWS), DIL_MERGE_ROWS)
        ms = [m_ref[rows, :] for m_ref in m_refs]
        m_all = functools.reduce(jnp.maximum, ms)
        num = jnp.zeros((DIL_MERGE_ROWS, LANES), F32)
        den = jnp.zeros((DIL_MERGE_ROWS, LANES), F32)
        for m_i, acc_ref, l_ref in zip(ms, acc_refs, l_refs):
            e = jnp.exp2(m_i - m_all)
            num = num + acc_ref[rows, :] * e
            den = den + l_ref[rows, :] * e
        o_ref[rows, :] = (num / den).astype(BF16)
        return carry

    lax.fori_loop(0, seq // DIL_MERGE_ROWS, merge, 0)


def _dil_attention(zb3, tbls):
    bsz, seq, _ = zb3.shape
    npair = DIL_HEADS // 2
    blk = lambda off: pl.BlockSpec((None, seq, LANES), lambda b, p: (b, 0, off + p))
    tbl_specs = []
    for t in tbls:
        if t.ndim == 3:
            tbl_specs.append(pl.BlockSpec((2,) + t.shape[1:], lambda b, p: (p, 0, 0)))
        else:
            tbl_specs.append(pl.BlockSpec((t.shape[0], 2) + t.shape[2:], lambda b, p: (0, p, 0, 0)))
    stat = pltpu.VMEM((seq, LANES), F32)
    return pl.pallas_call(
        functools.partial(_dil_kernel, seq=seq),
        grid=(bsz, npair),
        in_specs=[blk(0), blk(npair), blk(2 * npair)] + tbl_specs,
        out_specs=pl.BlockSpec((None, seq, LANES), lambda b, p: (b, 0, p)),
        out_shape=jax.ShapeDtypeStruct((bsz, seq, D_B), BF16),
        scratch_shapes=[stat] * (3 * len(DIL_PATTERNS)),
        compiler_params=_params(2),
        name="dil_attn",
    )(zb3, zb3, zb3, *tbls)


def _t5_bucket(rel):
    nb = T5_BUCKETS // 2
    max_exact = nb // 2
    n = np.abs(rel)
    large = max_exact + (np.log(np.maximum(n, 1) / max_exact)
                         / math.log(T5_MAX_DIST / max_exact) * (nb - max_exact)).astype(np.int64)
    large = np.minimum(large, nb - 1)
    return (np.where(rel > 0, nb, 0) + np.where(n < max_exact, n, large)).astype(np.int32)


def _t5_bias(t5_table, rel):
    onehot = _t5_bucket(rel)[:, :, None] == np.arange(T5_BUCKETS)[None, None, :]
    return jnp.einsum("qkb,bh->hqk", jnp.asarray(onehot, F32), t5_table.astype(F32),
                      precision=lax.Precision.HIGHEST)


def _dil_bias_tables(t5_table, seq):
    tables = []
    for _, d in DIL_PATTERNS:
        if _dil_whole_class(seq, d):
            pos = np.arange(seq // d)
            rel = pos[None, :] - pos[:, None]
            valid = np.abs(rel) <= DIL_HALF
            bias = _t5_bias(t5_table, rel * d)
            tables.append(jnp.where(valid[None], bias * LOG2E, NEG_INF))
        else:
            a = np.arange(DIL_BLK)[:, None]
            j = np.arange(DIL_KB)[None, :]
            per_position = []
            for shift in DIL_WINDOW_SHIFT:
                rel = j - shift - a
                per_position.append(jnp.where((np.abs(rel) <= DIL_HALF)[None], _t5_bias(t5_table, rel * d) * LOG2E,
                                              NEG_INF))
            tables.append(jnp.stack(per_position))
    return tables


MLA_TQ = 1024
MLA_CQ = 512
MLA_VT_ROWS = 80


def _mla_kernel(q_ref, k_ref, vt_ref, o_ref):
    chains = [(h, c) for c in range(MLA_TQ // MLA_CQ) for h in range(MLA_HEADS)]

    def scores(h, c):
        cols = slice(h * LANES, (h + 1) * LANES)
        return _dot_nt(k_ref[:, cols], q_ref[c * MLA_CQ:(c + 1) * MLA_CQ, cols])

    outs = []
    s_next = scores(*chains[0])
    for idx, (h, c) in enumerate(chains):
        s = s_next
        if idx + 1 < len(chains):
            s_next = scores(*chains[idx + 1])
        sb = s.astype(BF16)
        p = jnp.exp2(sb - jnp.max(sb, axis=0, keepdims=True))
        ov = _dot(vt_ref[h * MLA_VT_ROWS:(h + 1) * MLA_VT_ROWS, :], p)
        outs.append(ov[0:MLA_V] / ov[MLA_V:MLA_V + 1])
        if h == MLA_HEADS - 1:
            o_ref[c * MLA_CQ:(c + 1) * MLA_CQ, :] = jnp.concatenate(outs, axis=0).T.astype(BF16)
            outs = []


def _mla_attention(qm3, km3, vt3):
    bsz, seq, _ = qm3.shape
    return pl.pallas_call(
        _mla_kernel,
        grid=(bsz, seq // MLA_TQ),
        in_specs=[pl.BlockSpec((None, MLA_TQ, MLA_QK_W), lambda b, i: (b, i, 0)),
                  pl.BlockSpec((None, seq, MLA_QK_W), lambda b, i: (b, 0, 0), pipeline_mode=pl.Buffered(1)),
                  pl.BlockSpec((None, MLA_HEADS * MLA_VT_ROWS, seq), lambda b, i: (b, 0, 0),
                               pipeline_mode=pl.Buffered(1))],
        out_specs=pl.BlockSpec((None, MLA_TQ, D_C), lambda b, i: (b, i, 0)),
        out_shape=jax.ShapeDtypeStruct((bsz, seq, D_C), BF16),
        compiler_params=_params(2),
        name="mla_attn",
    )(qm3, km3, vt3)


def _out_kernel(x_ref, oa_ref, ob_ref, oc_ref, w1_ref, w2_ref, w3_ref, g_ref, gate_ref, o_ref):
    y = _dot(oa_ref[...], w1_ref[...]) + _dot(ob_ref[...], w2_ref[...]) + _dot(oc_ref[...], w3_ref[...])
    o_ref[...] = x_ref[...] + gate_ref[...] * _rms(y, g_ref[...])


def _out_proj(x, oa, ob, oc, w1, w2, w3, g_post, mod3, l, bsz, seq):
    n = x.shape[0]
    tm = 1024
    tps = seq // tm
    row = lambda i: (i, 0)
    const = lambda i: (0, 0)
    full = lambda a: pl.BlockSpec(a.shape, const)
    return pl.pallas_call(
        _out_kernel,
        grid=(n // tm,),
        in_specs=[pl.BlockSpec((tm, D_MODEL), row),
                  pl.BlockSpec((tm, D_A), row), pl.BlockSpec((tm, D_B), row), pl.BlockSpec((tm, D_C), row),
                  full(w1), full(w2), full(w3), full(g_post),
                  pl.BlockSpec((None, 1, D_MODEL), lambda i: ((l * bsz + i // tps) * N_MOD + 2, 0, 0))],
        out_specs=pl.BlockSpec((tm, D_MODEL), row),
        out_shape=jax.ShapeDtypeStruct((n, D_MODEL), F32),
        compiler_params=_params(1),
        name="out_proj",
    )(x, oa, ob, oc, w1, w2, w3, g_post, mod3)


FFN_TM = 512
FFN_CHUNK = 256


def _ffn_kernel(x_ref, xp_ref, xn_ref, g_ref, sh_ref, sc_ref, wa_ref, wg_ref, cw_ref, cb_ref, wd_ref,
                gpost_ref, gate_ref, o_ref, u_ref, *, tiles_per_seq):
    i = pl.program_id(0)
    has_prev = (i % tiles_per_seq) != 0
    has_next = (i % tiles_per_seq) != tiles_per_seq - 1
    g = g_ref[...]
    sc = 1.0 + sc_ref[...]
    sh = sh_ref[...]
    x = x_ref[...]
    pre = lambda t: _rms(t, g) * sc + sh
    h_prev = jnp.where(has_prev, pre(xp_ref[...]), 0.0)
    h_next = jnp.where(has_next, pre(xn_ref[...]), 0.0)
    h_mid = pre(x)
    h = h_mid.astype(BF16)
    h_ext = jnp.concatenate([h_prev, h_mid, h_next], axis=0).astype(BF16)
    tm = x.shape[0]
    ext = tm + 2 * SUBLANES
    mid = slice(SUBLANES, SUBLANES + tm)
    for c0 in range(0, D_FF, FFN_CHUNK):
        cols = slice(c0, c0 + FFN_CHUNK)
        a = _dot(h, wa_ref[:, cols])
        ge = _dot(h_ext, wg_ref[:, cols])
        g_prev = pltpu.roll(ge, 1, axis=0)[mid]
        g_next = pltpu.roll(ge, ext - 1, axis=0)[mid]
        gc = cb_ref[:, cols] + g_prev * cw_ref[0:1, cols]
        gc = gc + ge[mid] * cw_ref[1:2, cols]
        gc = gc + g_next * cw_ref[2:3, cols]
        u_ref[:, cols] = (jax.nn.gelu(gc) * a).astype(BF16)
    y = _dot(u_ref[...], wd_ref[...])
    o_ref[...] = x + gate_ref[...] * _rms(y, gpost_ref[...])


def _ffn(x, mod3, l, bsz, seq, g_pre, wa, wg, cw, cb, wd, g_post):
    n = x.shape[0]
    tm = FFN_TM
    tps = seq // tm
    r8 = tm // SUBLANES
    last8 = n // SUBLANES - 1
    const = lambda i: (0, 0)
    full = lambda a: pl.BlockSpec(a.shape, const)
    weight = lambda a: pl.BlockSpec(a.shape, const, pipeline_mode=pl.Buffered(1))
    mrow = lambda k: (lambda i: ((l * bsz + i // tps) * N_MOD + k, 0, 0))
    return pl.pallas_call(
        functools.partial(_ffn_kernel, tiles_per_seq=tps),
        grid=(n // tm,),
        in_specs=[pl.BlockSpec((tm, D_MODEL), lambda i: (i, 0)),
                  pl.BlockSpec((SUBLANES, D_MODEL), lambda i: (jnp.maximum(i * r8 - 1, 0), 0)),
                  pl.BlockSpec((SUBLANES, D_MODEL), lambda i: (jnp.minimum((i + 1) * r8, last8), 0)),
                  full(g_pre),
                  pl.BlockSpec((None, 1, D_MODEL), mrow(3)),
                  pl.BlockSpec((None, 1, D_MODEL), mrow(4)),
                  weight(wa), weight(wg), full(cw), full(cb), weight(wd), full(g_post),
                  pl.BlockSpec((None, 1, D_MODEL), mrow(5))],
        out_specs=pl.BlockSpec((tm, D_MODEL), lambda i: (i, 0)),
        out_shape=jax.ShapeDtypeStruct((n, D_MODEL), F32),
        scratch_shapes=[pltpu.VMEM((tm, D_FF), BF16)],
        compiler_params=_params(1),
        name="conv_ffn",
    )(x, x, x, g_pre, mod3, mod3, wa, wg, cw, cb, wd, g_post, mod3)


def _rope_tables(seq):
    inv_freq = jnp.asarray(ROPE_THETA ** (-np.arange(0, MLA_ROPE, 2, dtype=np.float32) / MLA_ROPE), F32)
    ang = jnp.arange(seq, dtype=F32)[:, None] * inv_freq[None, :]
    cos, sin = jnp.cos(ang), jnp.sin(ang)
    cos2 = jnp.concatenate([cos, cos], axis=1)
    sin2 = jnp.concatenate([sin, sin], axis=1)
    head_c = jnp.concatenate([cos2, jnp.ones((seq, MLA_NOPE), F32), jnp.zeros((seq, LANES - MLA_NOPE - MLA_ROPE), F32)], axis=1)
    head_s = jnp.concatenate([sin2, jnp.zeros((seq, LANES - MLA_ROPE), F32)], axis=1)
    cq = jnp.concatenate([head_c] * MLA_HEADS, axis=1)
    sq = jnp.concatenate([head_s] * MLA_HEADS, axis=1)
    ck = jnp.concatenate([cos2, jnp.zeros((seq, LANES - MLA_ROPE), F32)], axis=1)
    return cq, sq, ck, head_s


def _rot_half_cols(w):
    half = w.shape[1] // 2
    return jnp.concatenate([-w[:, half:], w[:, :half]], axis=1)


def _layer_weights(w_in, w_uq, w_ukv, w_out, w_up, w_down):
    d = w_in.shape[0]
    o_b = 3 * D_A
    o_c = o_b + 3 * D_B
    o_kr = o_c + MLA_Q_LORA + MLA_KV_LORA
    wa = w_in[:, :o_b].astype(BF16)
    wb = w_in[:, o_b:o_c].astype(BF16)
    w_kr = w_in[:, o_kr:o_kr + MLA_ROPE]
    zpad = jnp.zeros((d, LANES - MLA_ROPE), F32)
    wc = jnp.concatenate([w_in[:, o_c:o_kr], w_kr, zpad, _rot_half_cols(w_kr), zpad], axis=1).astype(BF16)

    hq = MLA_NOPE + MLA_ROPE
    q_cols, qr_cols, k_cols, v_cols = [], [], [], []
    zq = jnp.zeros((MLA_Q_LORA, LANES - hq), F32)
    for h in range(MLA_HEADS):
        nope = w_uq[:, h * hq:h * hq + MLA_NOPE]
        rope = w_uq[:, h * hq + MLA_NOPE:(h + 1) * hq]
        q_cols += [rope, nope, zq]
        qr_cols += [_rot_half_cols(rope), jnp.zeros((MLA_Q_LORA, LANES - MLA_ROPE), F32)]
        hk = MLA_NOPE + MLA_V
        k_cols += [jnp.zeros((MLA_KV_LORA, MLA_ROPE), F32), w_ukv[:, h * hk:h * hk + MLA_NOPE],
                   jnp.zeros((MLA_KV_LORA, LANES - hq), F32)]
        v_cols += [w_ukv[:, h * hk + MLA_NOPE:(h + 1) * hk], jnp.zeros((MLA_KV_LORA, MLA_VT_ROWS - MLA_V), F32)]
    cat = lambda cols: jnp.concatenate(cols, axis=1).astype(BF16)
    w1 = w_out[:D_A].astype(BF16)
    w2 = w_out[D_A:D_A + D_B].astype(BF16)
    w3 = w_out[D_A + D_B:].astype(BF16)
    w_a = w_up[:, :D_FF].astype(BF16)
    w_g = w_up[:, D_FF:].astype(BF16)
    return (wa, wb, wc, cat(q_cols), cat(qr_cols), cat(k_cols), cat(v_cols).T, w1, w2, w3, w_a, w_g,
            w_down.astype(BF16))


def kernel(x, c, w_ada, b_ada, g_pre_mix, g_post_mix, g_pre_ffn, g_post_ffn, w_in, na_rpb, t5_table,
           mla_g_q, mla_g_kv, w_uq, w_ukv, w_out, w_up, conv_w, conv_b, w_down):
    bsz, seq, d = x.shape
    depth = w_in.shape[0]
    n = bsz * seq
    rows = seq // GRID_W
    assert d == D_MODEL and seq % (DIL_BLK * max(dd for _, dd in DIL_PATTERNS)) == 0 and rows >= NA_KH_MAX

    mod3 = _modulation(c, w_ada, b_ada).reshape(depth * bsz * N_MOD, 1, D_MODEL)
    cq, sq, ck, sk = _rope_tables(seq)
    dil_tbls = _dil_bias_tables(t5_table, seq)
    na_tbls = _na_bias_tables(na_rpb, rows)
    row2 = lambda v: v.reshape(1, -1)

    xf = x.reshape(n, d)
    for l in range(depth):
        (wa, wb, wc, wuq, wuqr, wuk, wuv, w1, w2, w3, w_a, w_g, w_d) = _layer_weights(
            w_in[l], w_uq[l], w_ukv[l], w_out[l], w_up[l], w_down[l])
        za, zb, qm, km, vt = _in_proj(xf, mod3, l, bsz, seq, row2(g_pre_mix[l]), wa, wb, wc,
                                      row2(mla_g_q[l]), row2(mla_g_kv[l]), wuq, wuqr, wuk, wuv,
                                      cq, sq, ck, sk)
        o_a = _na_attention(za.reshape(bsz, seq, 3 * D_A), na_tbls[l])
        o_b = _dil_attention(zb.reshape(bsz, seq, 3 * D_B), dil_tbls)
        o_c = _mla_attention(qm.reshape(bsz, seq, MLA_QK_W), km.reshape(bsz, seq, MLA_QK_W), vt)
        xf = _out_proj(xf, o_a.reshape(n, D_A), o_b.reshape(n, D_B), o_c.reshape(n, D_C),
                       w1, w2, w3, row2(g_post_mix[l]), mod3, l, bsz, seq)
        xf = _ffn(xf, mod3, l, bsz, seq, row2(g_pre_ffn[l]), w_a, w_g, conv_w[l], row2(conv_b[l]), w_d,
                  row2(g_post_ffn[l]))
    return xf.reshape(bsz, seq, d)
```

```python
import functools
import math

import numpy as np
import jax
import jax.numpy as jnp
from jax import lax
from jax.experimental import pallas as pl
from jax.experimental.pallas import tpu as pltpu

F32 = jnp.float32
BF16 = jnp.bfloat16

D_MODEL = 1024
HEAD_DIM = 64
GRID_W = 64
EPS = 1e-6
NEG_INF = -1e30

NA_HEADS = 4
NA_KH_MAX = 8
NA_KW = 16

DIL_HEADS = 8
DIL_PATTERNS = ((128, 1), (512, 4), (2048, 16))
DIL_BLK = 128
DIL_HALF = 64

MLA_HEADS = 4
MLA_Q_LORA = 384
MLA_KV_LORA = 256
MLA_NOPE = 64
MLA_ROPE = 32
MLA_V = 64
ROPE_THETA = 10000.0

T5_BUCKETS = 32
T5_MAX_DIST = 1024

D_FF = 2816
CONV_W = 3

D_A = NA_HEADS * HEAD_DIM
D_B = DIL_HEADS * HEAD_DIM
D_C = MLA_HEADS * MLA_V
N_MOD = 6

LANES = 128
SUBLANES = 8
VMEM_LIMIT = 56 * 1024 * 1024

LOG2E = math.log2(math.e)
Q_SCALE = HEAD_DIM ** -0.5 * LOG2E
MLA_Q_SCALE = (MLA_NOPE + MLA_ROPE) ** -0.5 * LOG2E

ZC_W = MLA_Q_LORA + MLA_KV_LORA + LANES
MLA_QK_W = MLA_HEADS * LANES


def _params(n_axes):
    return pltpu.CompilerParams(dimension_semantics=("arbitrary",) * n_axes,
                                vmem_limit_bytes=VMEM_LIMIT)


def _dot(a, b):
    return jnp.dot(a, b, preferred_element_type=F32)


def _dot_nt(a, b):
    return lax.dot_general(a, b, (((1,), (1,)), ((), ())), preferred_element_type=F32)


def _rms(x, g):
    return x * lax.rsqrt(jnp.mean(x * x, axis=-1, keepdims=True) + EPS) * g


def _mod_kernel(c_ref, w_ref, b_ref, o_ref):
    ca = jax.nn.silu(c_ref[...]).astype(BF16)
    o_ref[0] = _dot(ca, w_ref[0].astype(BF16)) + b_ref[0]


def _modulation(c, w_ada, b_ada):
    depth, d, n = w_ada.shape
    bsz = c.shape[0]
    tn = 1536
    return pl.pallas_call(
        _mod_kernel,
        grid=(depth, n // tn),
        in_specs=[pl.BlockSpec((bsz, d), lambda l, j: (0, 0)),
                  pl.BlockSpec((1, d, tn), lambda l, j: (l, 0, j)),
                  pl.BlockSpec((1, 1, tn), lambda l, j: (l, 0, j))],
        out_specs=pl.BlockSpec((1, bsz, tn), lambda l, j: (l, 0, j)),
        out_shape=jax.ShapeDtypeStruct((depth, bsz, n), F32),
        compiler_params=_params(2),
        name="adaln_mod",
    )(c, w_ada, b_ada.reshape(depth, 1, n))


def _in_kernel(x_ref, g_ref, sh_ref, sc_ref, wa_ref, wb_ref, wc_ref, gq_ref, gkv_ref,
               wuq_ref, wuk_ref, wuvt_ref, cq_ref, sq_ref, ck_ref, sk_ref,
               za_ref, zb_ref, qm_ref, km_ref, vt_ref):
    x = x_ref[...]
    h = (_rms(x, g_ref[...]) * (1.0 + sc_ref[...]) + sh_ref[...]).astype(BF16)
    zc = _dot(h, wc_ref[...])
    c_q = zc[:, 0:MLA_Q_LORA]
    c_kv = zc[:, MLA_Q_LORA:MLA_Q_LORA + MLA_KV_LORA]
    k_rope = zc[:, MLA_Q_LORA + MLA_KV_LORA:ZC_W]
    cqn = _rms(c_q, gq_ref[...]).astype(BF16)
    ckvn = _rms(c_kv, gkv_ref[...]).astype(BF16)
    q = _dot(cqn, wuq_ref[...])
    k_nope = _dot(ckvn, wuk_ref[...])
    vt = _dot_nt(wuvt_ref[...], ckvn)
    za_ref[:, 0:D_A] = (_dot(h, wa_ref[:, 0:D_A]) * Q_SCALE).astype(BF16)
    za_ref[:, D_A:3 * D_A] = _dot(h, wa_ref[:, D_A:3 * D_A]).astype(BF16)
    zb_ref[:, 0:D_B] = _dot(h, wb_ref[:, 0:D_B]) * Q_SCALE
    for j in range(D_B, 3 * D_B, D_B):
        zb_ref[:, j:j + D_B] = _dot(h, wb_ref[:, j:j + D_B])
    q = q * cq_ref[...] + _rotate_half_lanes(q) * sq_ref[...]
    qm_ref[...] = (q * MLA_Q_SCALE).astype(BF16)
    kr = k_rope * ck_ref[...] + _rotate_half_lanes(k_rope) * sk_ref[...]
    km_ref[...] = (k_nope + jnp.concatenate([kr] * MLA_HEADS, axis=1)).astype(BF16)
    vrow = lax.broadcasted_iota(jnp.int32, (MLA_HEADS * MLA_VT_ROWS, 1), 0)
    ones_rows = functools.reduce(jnp.logical_or, [vrow == h * MLA_VT_ROWS + MLA_V for h in range(MLA_HEADS)])
    vt_ref[...] = (vt + jnp.where(ones_rows, 1.0, 0.0)).astype(BF16)


def _rotate_half_lanes(x):
    half = MLA_ROPE // 2
    width = x.shape[1]
    lane = lax.broadcasted_iota(jnp.int32, (1, width), 1)
    from_above = pltpu.roll(x, width - half, axis=1)
    from_below = pltpu.roll(x, half, axis=1)
    return jnp.where(lane % LANES < half, -from_above, from_below)


def _in_proj(x, mod3, l, bsz, seq, g_pre, wa, wb, wc, gq, gkv, wuq, wuk, wuv, cq, sq, ck, sk):
    n = x.shape[0]
    tm = 512
    tps = seq // tm
    row = lambda i: (i, 0)
    const = lambda i: (0, 0)
    pos = lambda i: (i % tps, 0)
    mrow = lambda k: (lambda i: ((l * bsz + i // tps) * N_MOD + k, 0, 0))
    full = lambda a: pl.BlockSpec(a.shape, const)
    return pl.pallas_call(
        _in_kernel,
        grid=(n // tm,),
        in_specs=[pl.BlockSpec((tm, D_MODEL), row),
                  full(g_pre),
                  pl.BlockSpec((None, 1, D_MODEL), mrow(0)),
                  pl.BlockSpec((None, 1, D_MODEL), mrow(1)),
                  full(wa), full(wb), full(wc), full(gq), full(gkv),
                  full(wuq), full(wuk), full(wuv),
                  pl.BlockSpec((tm, MLA_QK_W), pos), pl.BlockSpec((tm, MLA_QK_W), pos),
                  pl.BlockSpec((tm, LANES), pos), pl.BlockSpec((tm, LANES), pos)],
        out_specs=[pl.BlockSpec((tm, 3 * D_A), row),
                   pl.BlockSpec((tm, 3 * D_B), row),
                   pl.BlockSpec((tm, MLA_QK_W), row),
                   pl.BlockSpec((tm, MLA_QK_W), row),
                   pl.BlockSpec((None, MLA_HEADS * MLA_VT_ROWS, tm), lambda i: (i // tps, 0, i % tps))],
        out_shape=[jax.ShapeDtypeStruct((n, 3 * D_A), BF16),
                   jax.ShapeDtypeStruct((n, 3 * D_B), F32),
                   jax.ShapeDtypeStruct((n, MLA_QK_W), BF16),
                   jax.ShapeDtypeStruct((n, MLA_QK_W), BF16),
                   jax.ShapeDtypeStruct((bsz, MLA_HEADS * MLA_VT_ROWS, seq), BF16)],
        compiler_params=_params(1),
        name="in_proj",
    )(x, g_pre, mod3, mod3, wa, wb, wc, gq, gkv, wuq, wuk, wuv, cq, sq, ck, sk)


NA_ROWS_PER_STEP = 8
NA_GROUP = 4


def _na_kernel(q_ref, k_ref, v_ref, tbl_ref, o_ref, *, rows):
    i = pl.program_id(1)
    lane = lax.broadcasted_iota(jnp.int32, (1, LANES), 1)
    low = lane < HEAD_DIM
    kh = NA_KH_MAX

    def row_group(j0, carry):
        items = []
        scores = []
        for g in range(NA_GROUP):
            j = j0 * NA_GROUP + g
            r = i * NA_ROWS_PER_STEP + j
            r0 = jnp.clip(r - kh // 2, 0, rows - kh)
            variant = r - r0
            qs = pl.multiple_of(j * GRID_W, GRID_W)
            ks = pl.multiple_of(r0 * GRID_W, GRID_W)
            for pair in range(NA_HEADS // 2):
                cols = slice(pair * LANES, (pair + 1) * LANES)
                q2 = q_ref[pl.ds(qs, GRID_W), cols]
                k2 = k_ref[pl.ds(ks, kh * GRID_W), cols]
                items.append((qs, ks, cols))
                zero = jnp.zeros_like(q2)
                q_st = jnp.concatenate([jnp.where(low, q2, zero), jnp.where(low, zero, q2)], axis=0)
                bias = tbl_ref[variant, 2 * pair:2 * pair + 2].reshape(2 * GRID_W, kh * GRID_W)
                scores.append(_dot_nt(q_st, k2) + bias)
        probs = []
        for s in scores:
            m = jnp.max(s, axis=-1, keepdims=True)
            p = jnp.exp2(s - m)
            probs.append((p.astype(BF16), jnp.sum(p, axis=-1, keepdims=True)))
        for (qs, ks, cols), (p, den) in zip(items, probs):
            o = _dot(p, v_ref[pl.ds(ks, kh * GRID_W), cols]) / den
            o_ref[pl.ds(qs, GRID_W), cols] = jnp.where(low, o[:GRID_W], o[GRID_W:]).astype(BF16)
        return carry

    lax.fori_loop(0, NA_ROWS_PER_STEP // NA_GROUP, row_group, 0)


def _na_attention(za3, tbl):
    bsz, seq, _ = za3.shape
    rows = seq // GRID_W
    tq = NA_ROWS_PER_STEP * GRID_W
    return pl.pallas_call(
        functools.partial(_na_kernel, rows=rows),
        grid=(bsz, seq // tq),
        in_specs=[pl.BlockSpec((None, tq, D_A), lambda b, i: (b, i, 0)),
                  pl.BlockSpec((None, seq, D_A), lambda b, i: (b, 0, 1)),
                  pl.BlockSpec((None, seq, D_A), lambda b, i: (b, 0, 2)),
                  pl.BlockSpec(tbl.shape, lambda b, i: (0, 0, 0, 0))],
        out_specs=pl.BlockSpec((None, tq, D_A), lambda b, i: (b, i, 0)),
        out_shape=jax.ShapeDtypeStruct((bsz, seq, D_A), BF16),
        compiler_params=_params(2),
        name="na_attn",
    )(za3, za3, za3, tbl)


def _na_bias_tables(rpb, rows):
    kh = min(NA_KH_MAX, rows)
    col = np.arange(GRID_W)
    c_start = np.clip(col - NA_KW // 2, 0, GRID_W - NA_KW)
    valid = (col[None, :] >= c_start[:, None]) & (col[None, :] < c_start[:, None] + NA_KW)
    d_col = col[None, :] - col[:, None] + (NA_KW - 1)
    onehot = (d_col[:, :, None] == np.arange(2 * NA_KW - 1)[None, None, :]) & valid[:, :, None]
    by_row = jnp.stack([rpb[:, :, NA_KH_MAX - 1 - v:NA_KH_MAX - 1 - v + kh, :] for v in range(kh)], axis=1)
    t = jnp.einsum("lvhib,cdb->lvhcid", by_row.astype(F32), jnp.asarray(onehot, F32),
                   precision=lax.Precision.HIGHEST)
    t = jnp.where(valid[None, None, None, :, None, :], t * LOG2E, NEG_INF)
    return t.reshape(rpb.shape[0], kh, NA_HEADS, GRID_W, kh * GRID_W)


DIL_KB = DIL_BLK + 2 * DIL_HALF
DIL_GROUP = 8
DIL_GROUP_WHOLE = 2
DIL_MERGE_ROWS = 512
DIL_WINDOW_SHIFT = (DIL_HALF, 0, 2 * DIL_HALF)


def _dil_whole_class(seq, d):
    return seq // d <= 2 * DIL_BLK


def _dil_kernel(q_ref, k_ref, v_ref, *rest, seq):
    npat = len(DIL_PATTERNS)
    tbl_refs = rest[:npat]
    o_ref = rest[npat]
    stats = rest[npat + 1:]
    acc_refs, m_refs, l_refs = stats[:npat], stats[npat:2 * npat], stats[2 * npat:]

    lane = lax.broadcasted_iota(jnp.int32, (1, LANES), 1)
    low = lane < HEAD_DIM

    def attend_group(items, acc_ref, m_ref, l_ref):
        scores = []
        for _, q2, k2, _, bias in items:
            zero = jnp.zeros_like(q2)
            qs = jnp.concatenate([jnp.where(low, q2, zero), jnp.where(low, zero, q2)], axis=0)
            scores.append(_dot_nt(qs, k2) + bias())
        probs = []
        for s in scores:
            m = jnp.max(s, axis=-1, keepdims=True)
            p = jnp.exp2(s - m)
            probs.append((p.astype(BF16), m, jnp.sum(p, axis=-1, keepdims=True)))
        for (rows, q2, _, v2, _), (p, m, den) in zip(items, probs):
            nq = q2.shape[0]
            o = _dot(p, v2)
            acc_ref[rows, :] = jnp.where(low, o[:nq], o[nq:])
            m_ref[rows, :] = jnp.where(low, m[:nq], m[nq:])
            l_ref[rows, :] = jnp.where(low, den[:nq], den[nq:])

    for pat, (_, d) in enumerate(DIL_PATTERNS):
        length = seq // d
        tbl_ref, acc_ref, m_ref, l_ref = tbl_refs[pat], acc_refs[pat], m_refs[pat], l_refs[pat]

        if _dil_whole_class(seq, d):
            group = min(DIL_GROUP_WHOLE, d)

            def whole(g0, carry, d=d, length=length, group=group, tbl_ref=tbl_ref,
                      acc_ref=acc_ref, m_ref=m_ref, l_ref=l_ref):
                items = []
                for g in range(group):
                    r = g0 * group + g
                    rows = pl.ds(r, length, stride=d) if d > 1 else pl.ds(r, length)
                    q2 = q_ref[rows, :].astype(BF16)
                    items.append((rows, q2, k_ref[rows, :].astype(BF16), v_ref[rows, :].astype(BF16),
                                  lambda: tbl_ref[...].reshape(2 * length, length)))
                attend_group(items, acc_ref, m_ref, l_ref)
                return carry

            lax.fori_loop(0, d // group, whole, 0)
        else:
            nblk = length // DIL_BLK
            group = min(DIL_GROUP, nblk)

            def block(t0, carry, d=d, length=length, nblk=nblk, group=group, tbl_ref=tbl_ref,
                      acc_ref=acc_ref, m_ref=m_ref, l_ref=l_ref):
                items = []
                for g in range(group):
                    t = t0 * group + g
                    r = t // nblk
                    n = t % nblk
                    variant = jnp.where(n == 0, 1, 0) + jnp.where(n == nblk - 1, 2, 0)
                    q_start = r + d * DIL_BLK * n
                    k_start = r + d * jnp.clip(DIL_BLK * n - DIL_HALF, 0, length - DIL_KB)
                    if d == 1:
                        rows_q = pl.ds(q_start, DIL_BLK)
                        rows_k = pl.ds(k_start, DIL_KB)
                    else:
                        rows_q = pl.ds(q_start, DIL_BLK, stride=d)
                        rows_k = pl.ds(k_start, DIL_KB, stride=d)
                    q2 = q_ref[rows_q, :].astype(BF16)
                    items.append((rows_q, q2, k_ref[rows_k, :].astype(BF16), v_ref[rows_k, :].astype(BF16),
                                  lambda variant=variant: tbl_ref[variant].reshape(2 * DIL_BLK, DIL_KB)))
                attend_group(items, acc_ref, m_ref, l_ref)
                return carry

            lax.fori_loop(0, d * nblk // group, block, 0)

    def merge(c, carry):
        rows = pl.ds(pl.multiple_of(c * DIL_MERGE_ROWS, DIL_MERGE_ROWS), DIL_MERGE_ROWS)
        ms = [m_ref[rows, :] for m_ref in m_refs]
        m_all = functools.reduce(jnp.maximum, ms)
        num = jnp.zeros((DIL_MERGE_ROWS, LANES), F32)
        den = jnp.zeros((DIL_MERGE_ROWS, LANES), F32)
        for m_i, acc_ref, l_ref in zip(ms, acc_refs, l_refs):
            e = jnp.exp2(m_i - m_all)
            num = num + acc_ref[rows, :] * e
            den = den + l_ref[rows, :] * e
        o_ref[rows, :] = (num / den).astype(BF16)
        return carry

    lax.fori_loop(0, seq // DIL_MERGE_ROWS, merge, 0)


def _dil_attention(zb3, tbls):
    bsz, seq, _ = zb3.shape
    npair = DIL_HEADS // 2
    blk = lambda off: pl.BlockSpec((None, seq, LANES), lambda b, p: (b, 0, off + p))
    tbl_specs = []
    for t in tbls:
        if t.ndim == 3:
            tbl_specs.append(pl.BlockSpec((2,) + t.shape[1:], lambda b, p: (p, 0, 0)))
        else:
            tbl_specs.append(pl.BlockSpec((t.shape[0], 2) + t.shape[2:], lambda b, p: (0, p, 0, 0)))
    stat = pltpu.VMEM((seq, LANES), F32)
    return pl.pallas_call(
        functools.partial(_dil_kernel, seq=seq),
        grid=(bsz, npair),
        in_specs=[blk(0), blk(npair), blk(2 * npair)] + tbl_specs,
        out_specs=pl.BlockSpec((None, seq, LANES), lambda b, p: (b, 0, p)),
        out_shape=jax.ShapeDtypeStruct((bsz, seq, D_B), BF16),
        scratch_shapes=[stat] * (3 * len(DIL_PATTERNS)),
        compiler_params=_params(2),
        name="dil_attn",
    )(zb3, zb3, zb3, *tbls)


def _t5_bucket(rel):
    nb = T5_BUCKETS // 2
    max_exact = nb // 2
    n = np.abs(rel)
    large = max_exact + (np.log(np.maximum(n, 1) / max_exact)
                         / math.log(T5_MAX_DIST / max_exact) * (nb - max_exact)).astype(np.int64)
    large = np.minimum(large, nb - 1)
    return (np.where(rel > 0, nb, 0) + np.where(n < max_exact, n, large)).astype(np.int32)


def _t5_bias(t5_table, rel):
    onehot = _t5_bucket(rel)[:, :, None] == np.arange(T5_BUCKETS)[None, None, :]
    return jnp.einsum("qkb,bh->hqk", jnp.asarray(onehot, F32), t5_table.astype(F32),
                      precision=lax.Precision.HIGHEST)


def _dil_bias_tables(t5_table, seq):
    tables = []
    for _, d in DIL_PATTERNS:
        if _dil_whole_class(seq, d):
            pos = np.arange(seq // d)
            rel = pos[None, :] - pos[:, None]
            valid = np.abs(rel) <= DIL_HALF
            bias = _t5_bias(t5_table, rel * d)
            tables.append(jnp.where(valid[None], bias * LOG2E, NEG_INF))
        else:
            a = np.arange(DIL_BLK)[:, None]
            j = np.arange(DIL_KB)[None, :]
            per_position = []
            for shift in DIL_WINDOW_SHIFT:
                rel = j - shift - a
                per_position.append(jnp.where((np.abs(rel) <= DIL_HALF)[None], _t5_bias(t5_table, rel * d) * LOG2E,
                                              NEG_INF))
            tables.append(jnp.stack(per_position))
    return tables


MLA_TQ = 1024
MLA_CQ = 512
MLA_VT_ROWS = 80


def _mla_kernel(q_ref, k_ref, vt_ref, o_ref):
    chains = [(h, c) for c in range(MLA_TQ // MLA_CQ) for h in range(MLA_HEADS)]

    def scores(h, c):
        cols = slice(h * LANES, (h + 1) * LANES)
        return _dot_nt(k_ref[:, cols], q_ref[c * MLA_CQ:(c + 1) * MLA_CQ, cols])

    outs = []
    s_next = scores(*chains[0])
    for idx, (h, c) in enumerate(chains):
        s = s_next
        if idx + 1 < len(chains):
            s_next = scores(*chains[idx + 1])
        sb = s.astype(BF16)
        p = jnp.exp2(sb - jnp.max(sb, axis=0, keepdims=True))
        ov = _dot(vt_ref[h * MLA_VT_ROWS:(h + 1) * MLA_VT_ROWS, :], p)
        outs.append(ov[0:MLA_V] / ov[MLA_V:MLA_V + 1])
        if h == MLA_HEADS - 1:
            o_ref[c * MLA_CQ:(c + 1) * MLA_CQ, :] = jnp.concatenate(outs, axis=0).T.astype(BF16)
            outs = []


def _mla_attention(qm3, km3, vt3):
    bsz, seq, _ = qm3.shape
    return pl.pallas_call(
        _mla_kernel,
        grid=(bsz, seq // MLA_TQ),
        in_specs=[pl.BlockSpec((None, MLA_TQ, MLA_QK_W), lambda b, i: (b, i, 0)),
                  pl.BlockSpec((None, seq, MLA_QK_W), lambda b, i: (b, 0, 0), pipeline_mode=pl.Buffered(1)),
                  pl.BlockSpec((None, MLA_HEADS * MLA_VT_ROWS, seq), lambda b, i: (b, 0, 0),
                               pipeline_mode=pl.Buffered(1))],
        out_specs=pl.BlockSpec((None, MLA_TQ, D_C), lambda b, i: (b, i, 0)),
        out_shape=jax.ShapeDtypeStruct((bsz, seq, D_C), BF16),
        compiler_params=_params(2),
        name="mla_attn",
    )(qm3, km3, vt3)


def _out_kernel(x_ref, oa_ref, ob_ref, oc_ref, w1_ref, w2_ref, w3_ref, g_ref, gate_ref, o_ref):
    tm = x_ref.shape[0]
    for r0 in range(0, tm, tm // 2):
        rows = slice(r0, r0 + tm // 2)
        y = (_dot(oa_ref[rows, :], w1_ref[...]) + _dot(ob_ref[rows, :], w2_ref[...])
             + _dot(oc_ref[rows, :], w3_ref[...]))
        o_ref[rows, :] = x_ref[rows, :] + gate_ref[...] * _rms(y, g_ref[...])


def _out_proj(x, oa, ob, oc, w1, w2, w3, g_post, mod3, l, bsz, seq):
    n = x.shape[0]
    tm = 1024
    tps = seq // tm
    row = lambda i: (i, 0)
    const = lambda i: (0, 0)
    full = lambda a: pl.BlockSpec(a.shape, const)
    return pl.pallas_call(
        _out_kernel,
        grid=(n // tm,),
        in_specs=[pl.BlockSpec((tm, D_MODEL), row),
                  pl.BlockSpec((tm, D_A), row), pl.BlockSpec((tm, D_B), row), pl.BlockSpec((tm, D_C), row),
                  full(w1), full(w2), full(w3), full(g_post),
                  pl.BlockSpec((None, 1, D_MODEL), lambda i: ((l * bsz + i // tps) * N_MOD + 2, 0, 0))],
        out_specs=pl.BlockSpec((tm, D_MODEL), row),
        out_shape=jax.ShapeDtypeStruct((n, D_MODEL), F32),
        compiler_params=_params(1),
        name="out_proj",
    )(x, oa, ob, oc, w1, w2, w3, g_post, mod3)


FFN_TM = 512
FFN_CHUNK = 256


def _ffn_kernel(x_ref, xp_ref, xn_ref, g_ref, sh_ref, sc_ref, wa_ref, wg_ref, cw_ref, cb_ref, wd_ref,
                gpost_ref, gate_ref, o_ref, u_ref, *, tiles_per_seq):
    i = pl.program_id(0)
    has_prev = (i % tiles_per_seq) != 0
    has_next = (i % tiles_per_seq) != tiles_per_seq - 1
    g = g_ref[...]
    sc = 1.0 + sc_ref[...]
    sh = sh_ref[...]
    x = x_ref[...]
    pre = lambda t: _rms(t, g) * sc + sh
    h_prev = jnp.where(has_prev, pre(xp_ref[...]), 0.0)
    h_next = jnp.where(has_next, pre(xn_ref[...]), 0.0)
    h_mid = pre(x)
    h = h_mid.astype(BF16)
    h_ext = jnp.concatenate([h_prev, h_mid, h_next], axis=0).astype(BF16)
    tm = x.shape[0]
    ext = tm + 2 * SUBLANES
    mid = slice(SUBLANES, SUBLANES + tm)
    for c0 in range(0, D_FF, FFN_CHUNK):
        cols = slice(c0, c0 + FFN_CHUNK)
        a = _dot(h, wa_ref[:, cols])
        ge = _dot(h_ext, wg_ref[:, cols])
        g_prev = pltpu.roll(ge, 1, axis=0)[mid]
        g_next = pltpu.roll(ge, ext - 1, axis=0)[mid]
        gc = cb_ref[:, cols] + g_prev * cw_ref[0:1, cols]
        gc = gc + ge[mid] * cw_ref[1:2, cols]
        gc = gc + g_next * cw_ref[2:3, cols]
        u_ref[:, cols] = (jax.nn.gelu(gc) * a).astype(BF16)
    for r0 in range(0, tm, tm // 2):
        rows = slice(r0, r0 + tm // 2)
        y = _dot(u_ref[rows, :], wd_ref[...])
        o_ref[rows, :] = x[rows] + gate_ref[...] * _rms(y, gpost_ref[...])


def _ffn(x, mod3, l, bsz, seq, g_pre, wa, wg, cw, cb, wd, g_post):
    n = x.shape[0]
    tm = FFN_TM
    tps = seq // tm
    r8 = tm // SUBLANES
    last8 = n // SUBLANES - 1
    const = lambda i: (0, 0)
    full = lambda a: pl.BlockSpec(a.shape, const)
    weight = lambda a: pl.BlockSpec(a.shape, const, pipeline_mode=pl.Buffered(1))
    mrow = lambda k: (lambda i: ((l * bsz + i // tps) * N_MOD + k, 0, 0))
    return pl.pallas_call(
        functools.partial(_ffn_kernel, tiles_per_seq=tps),
        grid=(n // tm,),
        in_specs=[pl.BlockSpec((tm, D_MODEL), lambda i: (i, 0)),
                  pl.BlockSpec((SUBLANES, D_MODEL), lambda i: (jnp.maximum(i * r8 - 1, 0), 0)),
                  pl.BlockSpec((SUBLANES, D_MODEL), lambda i: (jnp.minimum((i + 1) * r8, last8), 0)),
                  full(g_pre),
                  pl.BlockSpec((None, 1, D_MODEL), mrow(3)),
                  pl.BlockSpec((None, 1, D_MODEL), mrow(4)),
                  weight(wa), weight(wg), full(cw), full(cb), weight(wd), full(g_post),
                  pl.BlockSpec((None, 1, D_MODEL), mrow(5))],
        out_specs=pl.BlockSpec((tm, D_MODEL), lambda i: (i, 0)),
        out_shape=jax.ShapeDtypeStruct((n, D_MODEL), F32),
        scratch_shapes=[pltpu.VMEM((tm, D_FF), BF16)],
        compiler_params=_params(1),
        name="conv_ffn",
    )(x, x, x, g_pre, mod3, mod3, wa, wg, cw, cb, wd, g_post, mod3)


def _rope_tables(seq):
    inv_freq = jnp.asarray(ROPE_THETA ** (-np.arange(0, MLA_ROPE, 2, dtype=np.float32) / MLA_ROPE), F32)
    ang = jnp.arange(seq, dtype=F32)[:, None] * inv_freq[None, :]
    cos, sin = jnp.cos(ang), jnp.sin(ang)
    cos2 = jnp.concatenate([cos, cos], axis=1)
    sin2 = jnp.concatenate([sin, sin], axis=1)
    head_c = jnp.concatenate([cos2, jnp.ones((seq, MLA_NOPE), F32), jnp.zeros((seq, LANES - MLA_NOPE - MLA_ROPE), F32)], axis=1)
    head_s = jnp.concatenate([sin2, jnp.zeros((seq, LANES - MLA_ROPE), F32)], axis=1)
    cq = jnp.concatenate([head_c] * MLA_HEADS, axis=1)
    sq = jnp.concatenate([head_s] * MLA_HEADS, axis=1)
    ck = jnp.concatenate([cos2, jnp.zeros((seq, LANES - MLA_ROPE), F32)], axis=1)
    return cq, sq, ck, head_s


def _layer_weights(w_in, w_uq, w_ukv, w_out, w_up, w_down):
    d = w_in.shape[0]
    o_b = 3 * D_A
    o_c = o_b + 3 * D_B
    o_kr = o_c + MLA_Q_LORA + MLA_KV_LORA
    wa = w_in[:, :o_b].astype(BF16)
    wb = w_in[:, o_b:o_c].astype(BF16)
    w_kr = w_in[:, o_kr:o_kr + MLA_ROPE]
    zpad = jnp.zeros((d, LANES - MLA_ROPE), F32)
    wc = jnp.concatenate([w_in[:, o_c:o_kr], w_kr, zpad], axis=1).astype(BF16)

    hq = MLA_NOPE + MLA_ROPE
    q_cols, k_cols, v_cols = [], [], []
    zq = jnp.zeros((MLA_Q_LORA, LANES - hq), F32)
    for h in range(MLA_HEADS):
        nope = w_uq[:, h * hq:h * hq + MLA_NOPE]
        rope = w_uq[:, h * hq + MLA_NOPE:(h + 1) * hq]
        q_cols += [rope, nope, zq]
        hk = MLA_NOPE + MLA_V
        k_cols += [jnp.zeros((MLA_KV_LORA, MLA_ROPE), F32), w_ukv[:, h * hk:h * hk + MLA_NOPE],
                   jnp.zeros((MLA_KV_LORA, LANES - hq), F32)]
        v_cols += [w_ukv[:, h * hk + MLA_NOPE:(h + 1) * hk], jnp.zeros((MLA_KV_LORA, MLA_VT_ROWS - MLA_V), F32)]
    cat = lambda cols: jnp.concatenate(cols, axis=1).astype(BF16)
    w1 = w_out[:D_A].astype(BF16)
    w2 = w_out[D_A:D_A + D_B].astype(BF16)
    w3 = w_out[D_A + D_B:].astype(BF16)
    w_a = w_up[:, :D_FF].astype(BF16)
    w_g = w_up[:, D_FF:].astype(BF16)
    return (wa, wb, wc, cat(q_cols), cat(k_cols), cat(v_cols).T, w1, w2, w3, w_a, w_g,
            w_down.astype(BF16))


def kernel(x, c, w_ada, b_ada, g_pre_mix, g_post_mix, g_pre_ffn, g_post_ffn, w_in, na_rpb, t5_table,
           mla_g_q, mla_g_kv, w_uq, w_ukv, w_out, w_up, conv_w, conv_b, w_down):
    bsz, seq, d = x.shape
    depth = w_in.shape[0]
    n = bsz * seq
    rows = seq // GRID_W
    assert d == D_MODEL and seq % (DIL_BLK * max(dd for _, dd in DIL_PATTERNS)) == 0 and rows >= NA_KH_MAX

    mod3 = _modulation(c, w_ada, b_ada).reshape(depth * bsz * N_MOD, 1, D_MODEL)
    cq, sq, ck, sk = _rope_tables(seq)
    dil_tbls = _dil_bias_tables(t5_table, seq)
    na_tbls = _na_bias_tables(na_rpb, rows)
    row2 = lambda v: v.reshape(1, -1)

    xf = x.reshape(n, d)
    for l in range(depth):
        (wa, wb, wc, wuq, wuk, wuv, w1, w2, w3, w_a, w_g, w_d) = _layer_weights(
            w_in[l], w_uq[l], w_ukv[l], w_out[l], w_up[l], w_down[l])
        za, zb, qm, km, vt = _in_proj(xf, mod3, l, bsz, seq, row2(g_pre_mix[l]), wa, wb, wc,
                                      row2(mla_g_q[l]), row2(mla_g_kv[l]), wuq, wuk, wuv,
                                      cq, sq, ck, sk)
        o_a = _na_attention(za.reshape(bsz, seq, 3 * D_A), na_tbls[l])
        o_b = _dil_attention(zb.reshape(bsz, seq, 3 * D_B), dil_tbls)
        o_c = _mla_attention(qm.reshape(bsz, seq, MLA_QK_W), km.reshape(bsz, seq, MLA_QK_W), vt)
        xf = _out_proj(xf, o_a.reshape(n, D_A), o_b.reshape(n, D_B), o_c.reshape(n, D_C),
                       w1, w2, w3, row2(g_post_mix[l]), mod3, l, bsz, seq)
        xf = _ffn(xf, mod3, l, bsz, seq, row2(g_pre_ffn[l]), w_a, w_g, conv_w[l], row2(conv_b[l]), w_d,
                  row2(g_post_ffn[l]))
    return xf.reshape(bsz, seq, d)
```

```python
import functools
import math

import numpy as np
import jax
import jax.numpy as jnp
from jax import lax
from jax.experimental import pallas as pl
from jax.experimental.pallas import tpu as pltpu

F32 = jnp.float32
BF16 = jnp.bfloat16

D_MODEL = 1024
HEAD_DIM = 64
GRID_W = 64
EPS = 1e-6
NEG_INF = -1e30

NA_HEADS = 4
NA_KH_MAX = 8
NA_KW = 16

DIL_HEADS = 8
DIL_PATTERNS = ((128, 1), (512, 4), (2048, 16))
DIL_BLK = 128
DIL_HALF = 64

MLA_HEADS = 4
MLA_Q_LORA = 384
MLA_KV_LORA = 256
MLA_NOPE = 64
MLA_ROPE = 32
MLA_V = 64
ROPE_THETA = 10000.0

T5_BUCKETS = 32
T5_MAX_DIST = 1024

D_FF = 2816
CONV_W = 3

D_A = NA_HEADS * HEAD_DIM
D_B = DIL_HEADS * HEAD_DIM
D_C = MLA_HEADS * MLA_V
N_MOD = 6

LANES = 128
SUBLANES = 8
VMEM_LIMIT = 56 * 1024 * 1024

LOG2E = math.log2(math.e)
Q_SCALE = HEAD_DIM ** -0.5 * LOG2E
MLA_Q_SCALE = (MLA_NOPE + MLA_ROPE) ** -0.5 * LOG2E

ZC_W = MLA_Q_LORA + MLA_KV_LORA + LANES
MLA_QK_W = MLA_HEADS * LANES


def _params(n_axes):
    return pltpu.CompilerParams(dimension_semantics=("arbitrary",) * n_axes,
                                vmem_limit_bytes=VMEM_LIMIT)


def _dot(a, b):
    return jnp.dot(a, b, preferred_element_type=F32)


def _dot_nt(a, b):
    return lax.dot_general(a, b, (((1,), (1,)), ((), ())), preferred_element_type=F32)


def _rms(x, g):
    return x * lax.rsqrt(jnp.mean(x * x, axis=-1, keepdims=True) + EPS) * g


def _mod_kernel(c_ref, w_ref, b_ref, o_ref):
    ca = jax.nn.silu(c_ref[...]).astype(BF16)
    o_ref[0] = _dot(ca, w_ref[0].astype(BF16)) + b_ref[0]


def _modulation(c, w_ada, b_ada):
    depth, d, n = w_ada.shape
    bsz = c.shape[0]
    tn = 1536
    return pl.pallas_call(
        _mod_kernel,
        grid=(depth, n // tn),
        in_specs=[pl.BlockSpec((bsz, d), lambda l, j: (0, 0)),
                  pl.BlockSpec((1, d, tn), lambda l, j: (l, 0, j)),
                  pl.BlockSpec((1, 1, tn), lambda l, j: (l, 0, j))],
        out_specs=pl.BlockSpec((1, bsz, tn), lambda l, j: (l, 0, j)),
        out_shape=jax.ShapeDtypeStruct((depth, bsz, n), F32),
        compiler_params=_params(2),
        name="adaln_mod",
    )(c, w_ada, b_ada.reshape(depth, 1, n))


def _in_kernel(x_ref, g_ref, sh_ref, sc_ref, wa_ref, wb_ref, wc_ref, gq_ref, gkv_ref,
               wuq_ref, wuk_ref, wuvt_ref, cq_ref, sq_ref, ck_ref, sk_ref,
               za_ref, zb_ref, qm_ref, km_ref, vt_ref):
    x = x_ref[...]
    h = (_rms(x, g_ref[...]) * (1.0 + sc_ref[...]) + sh_ref[...]).astype(BF16)
    zc = _dot(h, wc_ref[...])
    c_q = zc[:, 0:MLA_Q_LORA]
    c_kv = zc[:, MLA_Q_LORA:MLA_Q_LORA + MLA_KV_LORA]
    k_rope = zc[:, MLA_Q_LORA + MLA_KV_LORA:ZC_W]
    cqn = _rms(c_q, gq_ref[...]).astype(BF16)
    ckvn = _rms(c_kv, gkv_ref[...]).astype(BF16)
    q = _dot(cqn, wuq_ref[...])
    k_nope = _dot(ckvn, wuk_ref[...])
    vt = _dot_nt(wuvt_ref[...], ckvn)
    za_ref[:, 0:D_A] = (_dot(h, wa_ref[:, 0:D_A]) * Q_SCALE).astype(BF16)
    za_ref[:, D_A:3 * D_A] = _dot(h, wa_ref[:, D_A:3 * D_A]).astype(BF16)
    zb_ref[:, 0:D_B] = _dot(h, wb_ref[:, 0:D_B]) * Q_SCALE
    for j in range(D_B, 3 * D_B, D_B):
        zb_ref[:, j:j + D_B] = _dot(h, wb_ref[:, j:j + D_B])
    q = q * cq_ref[...] + _rotate_half_lanes(q) * sq_ref[...]
    qm_ref[...] = (q * MLA_Q_SCALE).astype(BF16)
    kr = k_rope * ck_ref[...] + _rotate_half_lanes(k_rope) * sk_ref[...]
    km_ref[...] = (k_nope + jnp.concatenate([kr] * MLA_HEADS, axis=1)).astype(BF16)
    vrow = lax.broadcasted_iota(jnp.int32, (MLA_HEADS * MLA_VT_ROWS, 1), 0)
    ones_rows = functools.reduce(jnp.logical_or, [vrow == h * MLA_VT_ROWS + MLA_V for h in range(MLA_HEADS)])
    vt_ref[...] = (vt + jnp.where(ones_rows, 1.0, 0.0)).astype(BF16)


def _rotate_half_lanes(x):
    half = MLA_ROPE // 2
    width = x.shape[1]
    lane = lax.broadcasted_iota(jnp.int32, (1, width), 1)
    from_above = pltpu.roll(x, width - half, axis=1)
    from_below = pltpu.roll(x, half, axis=1)
    return jnp.where(lane % LANES < half, -from_above, from_below)


def _in_proj(x, mod3, l, bsz, seq, g_pre, wa, wb, wc, gq, gkv, wuq, wuk, wuv, cq, sq, ck, sk):
    n = x.shape[0]
    tm = 1024
    tps = seq // tm
    row = lambda i: (i, 0)
    const = lambda i: (0, 0)
    pos = lambda i: (i % tps, 0)
    mrow = lambda k: (lambda i: ((l * bsz + i // tps) * N_MOD + k, 0, 0))
    full = lambda a: pl.BlockSpec(a.shape, const)
    return pl.pallas_call(
        _in_kernel,
        grid=(n // tm,),
        in_specs=[pl.BlockSpec((tm, D_MODEL), row),
                  full(g_pre),
                  pl.BlockSpec((None, 1, D_MODEL), mrow(0)),
                  pl.BlockSpec((None, 1, D_MODEL), mrow(1)),
                  full(wa), full(wb), full(wc), full(gq), full(gkv),
                  full(wuq), full(wuk), full(wuv),
                  pl.BlockSpec((tm, MLA_QK_W), pos), pl.BlockSpec((tm, MLA_QK_W), pos),
                  pl.BlockSpec((tm, LANES), pos), pl.BlockSpec((tm, LANES), pos)],
        out_specs=[pl.BlockSpec((tm, 3 * D_A), row),
                   pl.BlockSpec((tm, 3 * D_B), row),
                   pl.BlockSpec((tm, MLA_QK_W), row),
                   pl.BlockSpec((tm, MLA_QK_W), row),
                   pl.BlockSpec((None, MLA_HEADS * MLA_VT_ROWS, tm), lambda i: (i // tps, 0, i % tps))],
        out_shape=[jax.ShapeDtypeStruct((n, 3 * D_A), BF16),
                   jax.ShapeDtypeStruct((n, 3 * D_B), F32),
                   jax.ShapeDtypeStruct((n, MLA_QK_W), BF16),
                   jax.ShapeDtypeStruct((n, MLA_QK_W), BF16),
                   jax.ShapeDtypeStruct((bsz, MLA_HEADS * MLA_VT_ROWS, seq), BF16)],
        compiler_params=_params(1),
        name="in_proj",
    )(x, g_pre, mod3, mod3, wa, wb, wc, gq, gkv, wuq, wuk, wuv, cq, sq, ck, sk)


NA_ROWS_PER_STEP = 16
NA_GROUP = 4


def _na_kernel(q_ref, k_ref, v_ref, tbl_ref, o_ref, *, rows):
    i = pl.program_id(1)
    lane = lax.broadcasted_iota(jnp.int32, (1, LANES), 1)
    low = lane < HEAD_DIM
    kh = NA_KH_MAX

    def row_group(j0, carry):
        items = []
        scores = []
        for g in range(NA_GROUP):
            j = j0 * NA_GROUP + g
            r = i * NA_ROWS_PER_STEP + j
            r0 = jnp.clip(r - kh // 2, 0, rows - kh)
            variant = r - r0
            qs = pl.multiple_of(j * GRID_W, GRID_W)
            ks = pl.multiple_of(r0 * GRID_W, GRID_W)
            for pair in range(NA_HEADS // 2):
                cols = slice(pair * LANES, (pair + 1) * LANES)
                q2 = q_ref[pl.ds(qs, GRID_W), cols]
                k2 = k_ref[pl.ds(ks, kh * GRID_W), cols]
                items.append((qs, ks, cols))
                zero = jnp.zeros_like(q2)
                q_st = jnp.concatenate([jnp.where(low, q2, zero), jnp.where(low, zero, q2)], axis=0)
                bias = tbl_ref[variant, 2 * pair:2 * pair + 2].reshape(2 * GRID_W, kh * GRID_W)
                scores.append(_dot_nt(q_st, k2) + bias)
        probs = []
        for s in scores:
            m = jnp.max(s, axis=-1, keepdims=True)
            p = jnp.exp2(s - m)
            probs.append((p.astype(BF16), jnp.sum(p, axis=-1, keepdims=True)))
        for (qs, ks, cols), (p, den) in zip(items, probs):
            o = _dot(p, v_ref[pl.ds(ks, kh * GRID_W), cols]) / den
            o_ref[pl.ds(qs, GRID_W), cols] = jnp.where(low, o[:GRID_W], o[GRID_W:]).astype(BF16)
        return carry

    lax.fori_loop(0, NA_ROWS_PER_STEP // NA_GROUP, row_group, 0)


def _na_attention(za3, tbl):
    bsz, seq, _ = za3.shape
    rows = seq // GRID_W
    tq = NA_ROWS_PER_STEP * GRID_W
    return pl.pallas_call(
        functools.partial(_na_kernel, rows=rows),
        grid=(bsz, seq // tq),
        in_specs=[pl.BlockSpec((None, tq, D_A), lambda b, i: (b, i, 0)),
                  pl.BlockSpec((None, seq, D_A), lambda b, i: (b, 0, 1)),
                  pl.BlockSpec((None, seq, D_A), lambda b, i: (b, 0, 2)),
                  pl.BlockSpec(tbl.shape, lambda b, i: (0, 0, 0, 0))],
        out_specs=pl.BlockSpec((None, tq, D_A), lambda b, i: (b, i, 0)),
        out_shape=jax.ShapeDtypeStruct((bsz, seq, D_A), BF16),
        compiler_params=_params(2),
        name="na_attn",
    )(za3, za3, za3, tbl)


def _na_bias_tables(rpb, rows):
    kh = min(NA_KH_MAX, rows)
    col = np.arange(GRID_W)
    c_start = np.clip(col - NA_KW // 2, 0, GRID_W - NA_KW)
    valid = (col[None, :] >= c_start[:, None]) & (col[None, :] < c_start[:, None] + NA_KW)
    d_col = col[None, :] - col[:, None] + (NA_KW - 1)
    onehot = (d_col[:, :, None] == np.arange(2 * NA_KW - 1)[None, None, :]) & valid[:, :, None]
    by_row = jnp.stack([rpb[:, :, NA_KH_MAX - 1 - v:NA_KH_MAX - 1 - v + kh, :] for v in range(kh)], axis=1)
    t = jnp.einsum("lvhib,cdb->lvhcid", by_row.astype(F32), jnp.asarray(onehot, F32),
                   precision=lax.Precision.HIGHEST)
    t = jnp.where(valid[None, None, None, :, None, :], t * LOG2E, NEG_INF)
    return t.reshape(rpb.shape[0], kh, NA_HEADS, GRID_W, kh * GRID_W)


DIL_KB = DIL_BLK + 2 * DIL_HALF
DIL_GROUP = 8
DIL_GROUP_WHOLE = 2
DIL_MERGE_ROWS = 512
DIL_WINDOW_SHIFT = (DIL_HALF, 0, 2 * DIL_HALF)


def _dil_whole_class(seq, d):
    return seq // d <= 2 * DIL_BLK


def _dil_kernel(q_ref, k_ref, v_ref, *rest, seq):
    npat = len(DIL_PATTERNS)
    tbl_refs = rest[:npat]
    o_ref = rest[npat]
    stats = rest[npat + 1:]
    acc_refs, m_refs, l_refs = stats[:npat], stats[npat:2 * npat], stats[2 * npat:]

    lane = lax.broadcasted_iota(jnp.int32, (1, LANES), 1)
    low = lane < HEAD_DIM

    def attend_group(items, acc_ref, m_ref, l_ref):
        scores = []
        for _, q2, k2, _, bias in items:
            zero = jnp.zeros_like(q2)
            qs = jnp.concatenate([jnp.where(low, q2, zero), jnp.where(low, zero, q2)], axis=0)
            scores.append(_dot_nt(qs, k2) + bias())
        probs = []
        for s in scores:
            m = jnp.max(s, axis=-1, keepdims=True)
            p = jnp.exp2(s - m)
            probs.append((p.astype(BF16), m, jnp.sum(p, axis=-1, keepdims=True)))
        for (rows, q2, _, v2, _), (p, m, den) in zip(items, probs):
            nq = q2.shape[0]
            o = _dot(p, v2)
            acc_ref[rows, :] = jnp.where(low, o[:nq], o[nq:])
            m_ref[rows, :] = jnp.where(low, m[:nq], m[nq:])
            l_ref[rows, :] = jnp.where(low, den[:nq], den[nq:])

    for pat, (_, d) in enumerate(DIL_PATTERNS):
        length = seq // d
        tbl_ref, acc_ref, m_ref, l_ref = tbl_refs[pat], acc_refs[pat], m_refs[pat], l_refs[pat]

        if _dil_whole_class(seq, d):
            group = min(DIL_GROUP_WHOLE, d)

            def whole(g0, carry, d=d, length=length, group=group, tbl_ref=tbl_ref,
                      acc_ref=acc_ref, m_ref=m_ref, l_ref=l_ref):
                items = []
                for g in range(group):
                    r = g0 * group + g
                    rows = pl.ds(r, length, stride=d) if d > 1 else pl.ds(r, length)
                    q2 = q_ref[rows, :].astype(BF16)
                    items.append((rows, q2, k_ref[rows, :].astype(BF16), v_ref[rows, :].astype(BF16),
                                  lambda: tbl_ref[...].reshape(2 * length, length)))
                attend_group(items, acc_ref, m_ref, l_ref)
                return carry

            lax.fori_loop(0, d // group, whole, 0)
        else:
            nblk = length // DIL_BLK
            group = min(DIL_GROUP, nblk)

            def block(t0, carry, d=d, length=length, nblk=nblk, group=group, tbl_ref=tbl_ref,
                      acc_ref=acc_ref, m_ref=m_ref, l_ref=l_ref):
                items = []
                for g in range(group):
                    t = t0 * group + g
                    r = t // nblk
                    n = t % nblk
                    variant = jnp.where(n == 0, 1, 0) + jnp.where(n == nblk - 1, 2, 0)
                    q_start = r + d * DIL_BLK * n
                    k_start = r + d * jnp.clip(DIL_BLK * n - DIL_HALF, 0, length - DIL_KB)
                    if d == 1:
                        rows_q = pl.ds(q_start, DIL_BLK)
                        rows_k = pl.ds(k_start, DIL_KB)
                    else:
                        rows_q = pl.ds(q_start, DIL_BLK, stride=d)
                        rows_k = pl.ds(k_start, DIL_KB, stride=d)
                    q2 = q_ref[rows_q, :].astype(BF16)
                    items.append((rows_q, q2, k_ref[rows_k, :].astype(BF16), v_ref[rows_k, :].astype(BF16),
                                  lambda variant=variant: tbl_ref[variant].reshape(2 * DIL_BLK, DIL_KB)))
                attend_group(items, acc_ref, m_ref, l_ref)
                return carry

            lax.fori_loop(0, d * nblk // group, block, 0)

    def merge(c, carry):
        rows = pl.ds(pl.multiple_of(c * DIL_MERGE_ROWS, DIL_MERGE_ROWS), DIL_MERGE_ROWS)
        ms = [m_ref[rows, :] for m_ref in m_refs]
        m_all = functools.reduce(jnp.maximum, ms)
        num = jnp.zeros((DIL_MERGE_ROWS, LANES), F32)
        den = jnp.zeros((DIL_MERGE_ROWS, LANES), F32)
        for m_i, acc_ref, l_ref in zip(ms, acc_refs, l_refs):
            e = jnp.exp2(m_i - m_all)
            num = num + acc_ref[rows, :] * e
            den = den + l_ref[rows, :] * e
        o_ref[rows, :] = (num / den).astype(BF16)
        return carry

    lax.fori_loop(0, seq // DIL_MERGE_ROWS, merge, 0)


def _dil_attention(zb3, tbls):
    bsz, seq, _ = zb3.shape
    npair = DIL_HEADS // 2
    blk = lambda off: pl.BlockSpec((None, seq, LANES), lambda b, p: (b, 0, off + p))
    tbl_specs = []
    for t in tbls:
        if t.ndim == 3:
            tbl_specs.append(pl.BlockSpec((2,) + t.shape[1:], lambda b, p: (p, 0, 0)))
        else:
            tbl_specs.append(pl.BlockSpec((t.shape[0], 2) + t.shape[2:], lambda b, p: (0, p, 0, 0)))
    stat = pltpu.VMEM((seq, LANES), F32)
    return pl.pallas_call(
        functools.partial(_dil_kernel, seq=seq),
        grid=(bsz, npair),
        in_specs=[blk(0), blk(npair), blk(2 * npair)] + tbl_specs,
        out_specs=pl.BlockSpec((None, seq, LANES), lambda b, p: (b, 0, p)),
        out_shape=jax.ShapeDtypeStruct((bsz, seq, D_B), BF16),
        scratch_shapes=[stat] * (3 * len(DIL_PATTERNS)),
        compiler_params=_params(2),
        name="dil_attn",
    )(zb3, zb3, zb3, *tbls)


def _t5_bucket(rel):
    nb = T5_BUCKETS // 2
    max_exact = nb // 2
    n = np.abs(rel)
    large = max_exact + (np.log(np.maximum(n, 1) / max_exact)
                         / math.log(T5_MAX_DIST / max_exact) * (nb - max_exact)).astype(np.int64)
    large = np.minimum(large, nb - 1)
    return (np.where(rel > 0, nb, 0) + np.where(n < max_exact, n, large)).astype(np.int32)


def _t5_bias(t5_table, rel):
    onehot = _t5_bucket(rel)[:, :, None] == np.arange(T5_BUCKETS)[None, None, :]
    return jnp.einsum("qkb,bh->hqk", jnp.asarray(onehot, F32), t5_table.astype(F32),
                      precision=lax.Precision.HIGHEST)


def _dil_bias_tables(t5_table, seq):
    tables = []
    for _, d in DIL_PATTERNS:
        if _dil_whole_class(seq, d):
            pos = np.arange(seq // d)
            rel = pos[None, :] - pos[:, None]
            valid = np.abs(rel) <= DIL_HALF
            bias = _t5_bias(t5_table, rel * d)
            tables.append(jnp.where(valid[None], bias * LOG2E, NEG_INF))
        else:
            a = np.arange(DIL_BLK)[:, None]
            j = np.arange(DIL_KB)[None, :]
            per_position = []
            for shift in DIL_WINDOW_SHIFT:
                rel = j - shift - a
                per_position.append(jnp.where((np.abs(rel) <= DIL_HALF)[None], _t5_bias(t5_table, rel * d) * LOG2E,
                                              NEG_INF))
            tables.append(jnp.stack(per_position))
    return tables


MLA_TQ = 1024
MLA_CQ = 512
MLA_VT_ROWS = 80


def _mla_kernel(q_ref, k_ref, vt_ref, o_ref):
    chains = [(h, c) for c in range(MLA_TQ // MLA_CQ) for h in range(MLA_HEADS)]

    def scores(h, c):
        cols = slice(h * LANES, (h + 1) * LANES)
        return _dot_nt(k_ref[:, cols], q_ref[c * MLA_CQ:(c + 1) * MLA_CQ, cols])

    outs = []
    s_next = scores(*chains[0])
    for idx, (h, c) in enumerate(chains):
        s = s_next
        if idx + 1 < len(chains):
            s_next = scores(*chains[idx + 1])
        sb = s.astype(BF16)
        p = jnp.exp2(sb - jnp.max(sb, axis=0, keepdims=True))
        ov = _dot(vt_ref[h * MLA_VT_ROWS:(h + 1) * MLA_VT_ROWS, :], p)
        outs.append(ov[0:MLA_V] / ov[MLA_V:MLA_V + 1])
        if h == MLA_HEADS - 1:
            o_ref[c * MLA_CQ:(c + 1) * MLA_CQ, :] = jnp.concatenate(outs, axis=0).T.astype(BF16)
            outs = []


def _mla_attention(qm3, km3, vt3):
    bsz, seq, _ = qm3.shape
    return pl.pallas_call(
        _mla_kernel,
        grid=(bsz, seq // MLA_TQ),
        in_specs=[pl.BlockSpec((None, MLA_TQ, MLA_QK_W), lambda b, i: (b, i, 0)),
                  pl.BlockSpec((None, seq, MLA_QK_W), lambda b, i: (b, 0, 0), pipeline_mode=pl.Buffered(1)),
                  pl.BlockSpec((None, MLA_HEADS * MLA_VT_ROWS, seq), lambda b, i: (b, 0, 0),
                               pipeline_mode=pl.Buffered(1))],
        out_specs=pl.BlockSpec((None, MLA_TQ, D_C), lambda b, i: (b, i, 0)),
        out_shape=jax.ShapeDtypeStruct((bsz, seq, D_C), BF16),
        compiler_params=_params(2),
        name="mla_attn",
    )(qm3, km3, vt3)


def _out_kernel(x_ref, oa_ref, ob_ref, oc_ref, w1_ref, w2_ref, w3_ref, g_ref, gate_ref, o_ref):
    tm = x_ref.shape[0]
    for r0 in range(0, tm, tm // 2):
        rows = slice(r0, r0 + tm // 2)
        y = (_dot(oa_ref[rows, :], w1_ref[...]) + _dot(ob_ref[rows, :], w2_ref[...])
             + _dot(oc_ref[rows, :], w3_ref[...]))
        o_ref[rows, :] = x_ref[rows, :] + gate_ref[...] * _rms(y, g_ref[...])


def _out_proj(x, oa, ob, oc, w1, w2, w3, g_post, mod3, l, bsz, seq):
    n = x.shape[0]
    tm = 1024
    tps = seq // tm
    row = lambda i: (i, 0)
    const = lambda i: (0, 0)
    full = lambda a: pl.BlockSpec(a.shape, const)
    return pl.pallas_call(
        _out_kernel,
        grid=(n // tm,),
        in_specs=[pl.BlockSpec((tm, D_MODEL), row),
                  pl.BlockSpec((tm, D_A), row), pl.BlockSpec((tm, D_B), row), pl.BlockSpec((tm, D_C), row),
                  full(w1), full(w2), full(w3), full(g_post),
                  pl.BlockSpec((None, 1, D_MODEL), lambda i: ((l * bsz + i // tps) * N_MOD + 2, 0, 0))],
        out_specs=pl.BlockSpec((tm, D_MODEL), row),
        out_shape=jax.ShapeDtypeStruct((n, D_MODEL), F32),
        compiler_params=_params(1),
        name="out_proj",
    )(x, oa, ob, oc, w1, w2, w3, g_post, mod3)


FFN_TM = 512
FFN_CHUNK = 256


def _ffn_kernel(x_ref, xp_ref, xn_ref, g_ref, sh_ref, sc_ref, wa_ref, wg_ref, cw_ref, cb_ref, wd_ref,
                gpost_ref, gate_ref, o_ref, u_ref, *, tiles_per_seq):
    i = pl.program_id(0)
    has_prev = (i % tiles_per_seq) != 0
    has_next = (i % tiles_per_seq) != tiles_per_seq - 1
    g = g_ref[...]
    sc = 1.0 + sc_ref[...]
    sh = sh_ref[...]
    x = x_ref[...]
    pre = lambda t: _rms(t, g) * sc + sh
    h_prev = jnp.where(has_prev, pre(xp_ref[...]), 0.0)
    h_next = jnp.where(has_next, pre(xn_ref[...]), 0.0)
    h_mid = pre(x)
    h = h_mid.astype(BF16)
    h_ext = jnp.concatenate([h_prev, h_mid, h_next], axis=0).astype(BF16)
    tm = x.shape[0]
    ext = tm + 2 * SUBLANES
    mid = slice(SUBLANES, SUBLANES + tm)
    for c0 in range(0, D_FF, FFN_CHUNK):
        cols = slice(c0, c0 + FFN_CHUNK)
        a = _dot(h, wa_ref[:, cols])
        ge = _dot(h_ext, wg_ref[:, cols])
        g_prev = pltpu.roll(ge, 1, axis=0)[mid]
        g_next = pltpu.roll(ge, ext - 1, axis=0)[mid]
        gc = cb_ref[:, cols] + g_prev * cw_ref[0:1, cols]
        gc = gc + ge[mid] * cw_ref[1:2, cols]
        gc = gc + g_next * cw_ref[2:3, cols]
        u_ref[:, cols] = (jax.nn.gelu(gc) * a).astype(BF16)
    for r0 in range(0, tm, tm // 2):
        rows = slice(r0, r0 + tm // 2)
        y = _dot(u_ref[rows, :], wd_ref[...])
        o_ref[rows, :] = x[rows] + gate_ref[...] * _rms(y, gpost_ref[...])


def _ffn(x, mod3, l, bsz, seq, g_pre, wa, wg, cw, cb, wd, g_post):
    n = x.shape[0]
    tm = FFN_TM
    tps = seq // tm
    r8 = tm // SUBLANES
    last8 = n // SUBLANES - 1
    const = lambda i: (0, 0)
    full = lambda a: pl.BlockSpec(a.shape, const)
    weight = lambda a: pl.BlockSpec(a.shape, const, pipeline_mode=pl.Buffered(1))
    mrow = lambda k: (lambda i: ((l * bsz + i // tps) * N_MOD + k, 0, 0))
    return pl.pallas_call(
        functools.partial(_ffn_kernel, tiles_per_seq=tps),
        grid=(n // tm,),
        in_specs=[pl.BlockSpec((tm, D_MODEL), lambda i: (i, 0)),
                  pl.BlockSpec((SUBLANES, D_MODEL), lambda i: (jnp.maximum(i * r8 - 1, 0), 0)),
                  pl.BlockSpec((SUBLANES, D_MODEL), lambda i: (jnp.minimum((i + 1) * r8, last8), 0)),
                  full(g_pre),
                  pl.BlockSpec((None, 1, D_MODEL), mrow(3)),
                  pl.BlockSpec((None, 1, D_MODEL), mrow(4)),
                  weight(wa), weight(wg), full(cw), full(cb), weight(wd), full(g_post),
                  pl.BlockSpec((None, 1, D_MODEL), mrow(5))],
        out_specs=pl.BlockSpec((tm, D_MODEL), lambda i: (i, 0)),
        out_shape=jax.ShapeDtypeStruct((n, D_MODEL), F32),
        scratch_shapes=[pltpu.VMEM((tm, D_FF), BF16)],
        compiler_params=_params(1),
        name="conv_ffn",
    )(x, x, x, g_pre, mod3, mod3, wa, wg, cw, cb, wd, g_post, mod3)


def _rope_tables(seq):
    inv_freq = jnp.asarray(ROPE_THETA ** (-np.arange(0, MLA_ROPE, 2, dtype=np.float32) / MLA_ROPE), F32)
    ang = jnp.arange(seq, dtype=F32)[:, None] * inv_freq[None, :]
    cos, sin = jnp.cos(ang), jnp.sin(ang)
    cos2 = jnp.concatenate([cos, cos], axis=1)
    sin2 = jnp.concatenate([sin, sin], axis=1)
    head_c = jnp.concatenate([cos2, jnp.ones((seq, MLA_NOPE), F32), jnp.zeros((seq, LANES - MLA_NOPE - MLA_ROPE), F32)], axis=1)
    head_s = jnp.concatenate([sin2, jnp.zeros((seq, LANES - MLA_ROPE), F32)], axis=1)
    cq = jnp.concatenate([head_c] * MLA_HEADS, axis=1)
    sq = jnp.concatenate([head_s] * MLA_HEADS, axis=1)
    ck = jnp.concatenate([cos2, jnp.zeros((seq, LANES - MLA_ROPE), F32)], axis=1)
    return cq, sq, ck, head_s


def _layer_weights(w_in, w_uq, w_ukv, w_out, w_up, w_down):
    d = w_in.shape[0]
    o_b = 3 * D_A
    o_c = o_b + 3 * D_B
    o_kr = o_c + MLA_Q_LORA + MLA_KV_LORA
    wa = w_in[:, :o_b].astype(BF16)
    wb = w_in[:, o_b:o_c].astype(BF16)
    w_kr = w_in[:, o_kr:o_kr + MLA_ROPE]
    zpad = jnp.zeros((d, LANES - MLA_ROPE), F32)
    wc = jnp.concatenate([w_in[:, o_c:o_kr], w_kr, zpad], axis=1).astype(BF16)

    hq = MLA_NOPE + MLA_ROPE
    q_cols, k_cols, v_cols = [], [], []
    zq = jnp.zeros((MLA_Q_LORA, LANES - hq), F32)
    for h in range(MLA_HEADS):
        nope = w_uq[:, h * hq:h * hq + MLA_NOPE]
        rope = w_uq[:, h * hq + MLA_NOPE:(h + 1) * hq]
        q_cols += [rope, nope, zq]
        hk = MLA_NOPE + MLA_V
        k_cols += [jnp.zeros((MLA_KV_LORA, MLA_ROPE), F32), w_ukv[:, h * hk:h * hk + MLA_NOPE],
                   jnp.zeros((MLA_KV_LORA, LANES - hq), F32)]
        v_cols += [w_ukv[:, h * hk + MLA_NOPE:(h + 1) * hk], jnp.zeros((MLA_KV_LORA, MLA_VT_ROWS - MLA_V), F32)]
    cat = lambda cols: jnp.concatenate(cols, axis=1).astype(BF16)
    w1 = w_out[:D_A].astype(BF16)
    w2 = w_out[D_A:D_A + D_B].astype(BF16)
    w3 = w_out[D_A + D_B:].astype(BF16)
    w_a = w_up[:, :D_FF].astype(BF16)
    w_g = w_up[:, D_FF:].astype(BF16)
    return (wa, wb, wc, cat(q_cols), cat(k_cols), cat(v_cols).T, w1, w2, w3, w_a, w_g,
            w_down.astype(BF16))


def kernel(x, c, w_ada, b_ada, g_pre_mix, g_post_mix, g_pre_ffn, g_post_ffn, w_in, na_rpb, t5_table,
           mla_g_q, mla_g_kv, w_uq, w_ukv, w_out, w_up, conv_w, conv_b, w_down):
    bsz, seq, d = x.shape
    depth = w_in.shape[0]
    n = bsz * seq
    rows = seq // GRID_W
    assert d == D_MODEL and seq % (DIL_BLK * max(dd for _, dd in DIL_PATTERNS)) == 0 and rows >= NA_KH_MAX

    mod3 = _modulation(c, w_ada, b_ada).reshape(depth * bsz * N_MOD, 1, D_MODEL)
    cq, sq, ck, sk = _rope_tables(seq)
    dil_tbls = _dil_bias_tables(t5_table, seq)
    na_tbls = _na_bias_tables(na_rpb, rows)
    row2 = lambda v: v.reshape(1, -1)

    xf = x.reshape(n, d)
    for l in range(depth):
        (wa, wb, wc, wuq, wuk, wuv, w1, w2, w3, w_a, w_g, w_d) = _layer_weights(
            w_in[l], w_uq[l], w_ukv[l], w_out[l], w_up[l], w_down[l])
        za, zb, qm, km, vt = _in_proj(xf, mod3, l, bsz, seq, row2(g_pre_mix[l]), wa, wb, wc,
                                      row2(mla_g_q[l]), row2(mla_g_kv[l]), wuq, wuk, wuv,
                                      cq, sq, ck, sk)
        o_a = _na_attention(za.reshape(bsz, seq, 3 * D_A), na_tbls[l])
        o_b = _dil_attention(zb.reshape(bsz, seq, 3 * D_B), dil_tbls)
        o_c = _mla_attention(qm.reshape(bsz, seq, MLA_QK_W), km.reshape(bsz, seq, MLA_QK_W), vt)
        xf = _out_proj(xf, o_a.reshape(n, D_A), o_b.reshape(n, D_B), o_c.reshape(n, D_C),
                       w1, w2, w3, row2(g_post_mix[l]), mod3, l, bsz, seq)
        xf = _ffn(xf, mod3, l, bsz, seq, row2(g_pre_ffn[l]), w_a, w_g, conv_w[l], row2(conv_b[l]), w_d,
                  row2(g_post_ffn[l]))
    return xf.reshape(bsz, seq, d)
```

```python
import functools
import math

import numpy as np
import jax
import jax.numpy as jnp
from jax import lax
from jax.experimental import pallas as pl
from jax.experimental.pallas import tpu as pltpu

F32 = jnp.float32
BF16 = jnp.bfloat16

D_MODEL = 1024
HEAD_DIM = 64
GRID_W = 64
EPS = 1e-6
NEG_INF = -1e30

NA_HEADS = 4
NA_KH_MAX = 8
NA_KW = 16

DIL_HEADS = 8
DIL_PATTERNS = ((128, 1), (512, 4), (2048, 16))
DIL_BLK = 128
DIL_HALF = 64

MLA_HEADS = 4
MLA_Q_LORA = 384
MLA_KV_LORA = 256
MLA_NOPE = 64
MLA_ROPE = 32
MLA_V = 64
ROPE_THETA = 10000.0

T5_BUCKETS = 32
T5_MAX_DIST = 1024

D_FF = 2816
CONV_W = 3

D_A = NA_HEADS * HEAD_DIM
D_B = DIL_HEADS * HEAD_DIM
D_C = MLA_HEADS * MLA_V
N_MOD = 6

LANES = 128
SUBLANES = 8
VMEM_LIMIT = 56 * 1024 * 1024

LOG2E = math.log2(math.e)
Q_SCALE = HEAD_DIM ** -0.5 * LOG2E
MLA_Q_SCALE = (MLA_NOPE + MLA_ROPE) ** -0.5 * LOG2E

ZC_W = MLA_Q_LORA + MLA_KV_LORA + LANES
MLA_QK_W = MLA_HEADS * LANES


def _params(n_axes):
    return pltpu.CompilerParams(dimension_semantics=("arbitrary",) * n_axes,
                                vmem_limit_bytes=VMEM_LIMIT)


def _dot(a, b):
    return jnp.dot(a, b, preferred_element_type=F32)


def _dot_nt(a, b):
    return lax.dot_general(a, b, (((1,), (1,)), ((), ())), preferred_element_type=F32)


def _rms(x, g):
    return x * lax.rsqrt(jnp.mean(x * x, axis=-1, keepdims=True) + EPS) * g


def _mod_kernel(c_ref, w_ref, b_ref, o_ref):
    ca = jax.nn.silu(c_ref[...]).astype(BF16)
    o_ref[0] = _dot(ca, w_ref[0].astype(BF16)) + b_ref[0]


def _modulation(c, w_ada, b_ada):
    depth, d, n = w_ada.shape
    bsz = c.shape[0]
    tn = 1536
    return pl.pallas_call(
        _mod_kernel,
        grid=(depth, n // tn),
        in_specs=[pl.BlockSpec((bsz, d), lambda l, j: (0, 0)),
                  pl.BlockSpec((1, d, tn), lambda l, j: (l, 0, j)),
                  pl.BlockSpec((1, 1, tn), lambda l, j: (l, 0, j))],
        out_specs=pl.BlockSpec((1, bsz, tn), lambda l, j: (l, 0, j)),
        out_shape=jax.ShapeDtypeStruct((depth, bsz, n), F32),
        compiler_params=_params(2),
        name="adaln_mod",
    )(c, w_ada, b_ada.reshape(depth, 1, n))


def _in_kernel(x_ref, g_ref, sh_ref, sc_ref, wa_ref, wb_ref, wc_ref, gq_ref, gkv_ref,
               wuq_ref, wuk_ref, wuvt_ref, cq_ref, sq_ref, ck_ref, sk_ref,
               za_ref, zb_ref, qm_ref, km_ref, vt_ref):
    x = x_ref[...]
    h = (_rms(x, g_ref[...]) * (1.0 + sc_ref[...]) + sh_ref[...]).astype(BF16)
    zc = _dot(h, wc_ref[...])
    c_q = zc[:, 0:MLA_Q_LORA]
    c_kv = zc[:, MLA_Q_LORA:MLA_Q_LORA + MLA_KV_LORA]
    k_rope = zc[:, MLA_Q_LORA + MLA_KV_LORA:ZC_W]
    cqn = _rms(c_q, gq_ref[...]).astype(BF16)
    ckvn = _rms(c_kv, gkv_ref[...]).astype(BF16)
    q = _dot(cqn, wuq_ref[...])
    k_nope = _dot(ckvn, wuk_ref[...])
    vt = _dot_nt(wuvt_ref[...], ckvn)
    za_ref[:, 0:D_A] = (_dot(h, wa_ref[:, 0:D_A]) * Q_SCALE).astype(BF16)
    za_ref[:, D_A:3 * D_A] = _dot(h, wa_ref[:, D_A:3 * D_A]).astype(BF16)
    zb_ref[:, 0:D_B] = _dot(h, wb_ref[:, 0:D_B]) * Q_SCALE
    for j in range(D_B, 3 * D_B, D_B):
        zb_ref[:, j:j + D_B] = _dot(h, wb_ref[:, j:j + D_B])
    q = q * cq_ref[...] + _rotate_half_lanes(q) * sq_ref[...]
    qm_ref[...] = (q * MLA_Q_SCALE).astype(BF16)
    kr = k_rope * ck_ref[...] + _rotate_half_lanes(k_rope) * sk_ref[...]
    km_ref[...] = (k_nope + jnp.concatenate([kr] * MLA_HEADS, axis=1)).astype(BF16)
    vrow = lax.broadcasted_iota(jnp.int32, (MLA_HEADS * MLA_VT_ROWS, 1), 0)
    ones_rows = functools.reduce(jnp.logical_or, [vrow == h * MLA_VT_ROWS + MLA_V for h in range(MLA_HEADS)])
    vt_ref[...] = (vt + jnp.where(ones_rows, 1.0, 0.0)).astype(BF16)


def _rotate_half_lanes(x):
    half = MLA_ROPE // 2
    width = x.shape[1]
    lane = lax.broadcasted_iota(jnp.int32, (1, width), 1)
    from_above = pltpu.roll(x, width - half, axis=1)
    from_below = pltpu.roll(x, half, axis=1)
    return jnp.where(lane % LANES < half, -from_above, from_below)


def _in_proj(x, mod3, l, bsz, seq, g_pre, wa, wb, wc, gq, gkv, wuq, wuk, wuv, cq, sq, ck, sk):
    n = x.shape[0]
    tm = 1024
    tps = seq // tm
    row = lambda i: (i, 0)
    const = lambda i: (0, 0)
    pos = lambda i: (i % tps, 0)
    mrow = lambda k: (lambda i: ((l * bsz + i // tps) * N_MOD + k, 0, 0))
    full = lambda a: pl.BlockSpec(a.shape, const)
    return pl.pallas_call(
        _in_kernel,
        grid=(n // tm,),
        in_specs=[pl.BlockSpec((tm, D_MODEL), row),
                  full(g_pre),
                  pl.BlockSpec((None, 1, D_MODEL), mrow(0)),
                  pl.BlockSpec((None, 1, D_MODEL), mrow(1)),
                  full(wa), full(wb), full(wc), full(gq), full(gkv),
                  full(wuq), full(wuk), full(wuv),
                  pl.BlockSpec((tm, MLA_QK_W), pos), pl.BlockSpec((tm, MLA_QK_W), pos),
                  pl.BlockSpec((tm, LANES), pos), pl.BlockSpec((tm, LANES), pos)],
        out_specs=[pl.BlockSpec((tm, 3 * D_A), row),
                   pl.BlockSpec((tm, 3 * D_B), row),
                   pl.BlockSpec((tm, MLA_QK_W), row),
                   pl.BlockSpec((tm, MLA_QK_W), row),
                   pl.BlockSpec((None, MLA_HEADS * MLA_VT_ROWS, tm), lambda i: (i // tps, 0, i % tps))],
        out_shape=[jax.ShapeDtypeStruct((n, 3 * D_A), BF16),
                   jax.ShapeDtypeStruct((n, 3 * D_B), F32),
                   jax.ShapeDtypeStruct((n, MLA_QK_W), BF16),
                   jax.ShapeDtypeStruct((n, MLA_QK_W), BF16),
                   jax.ShapeDtypeStruct((bsz, MLA_HEADS * MLA_VT_ROWS, seq), BF16)],
        compiler_params=_params(1),
        name="in_proj",
    )(x, g_pre, mod3, mod3, wa, wb, wc, gq, gkv, wuq, wuk, wuv, cq, sq, ck, sk)


NA_ROWS_PER_STEP = 16
NA_GROUP = 4


def _na_kernel(q_ref, k_ref, v_ref, tbl_ref, o_ref, *, rows):
    i = pl.program_id(1)
    lane = lax.broadcasted_iota(jnp.int32, (1, LANES), 1)
    low = lane < HEAD_DIM
    kh = NA_KH_MAX

    def row_group(j0, carry):
        items = []
        scores = []
        for g in range(NA_GROUP):
            j = j0 * NA_GROUP + g
            r = i * NA_ROWS_PER_STEP + j
            r0 = jnp.clip(r - kh // 2, 0, rows - kh)
            variant = r - r0
            qs = pl.multiple_of(j * GRID_W, GRID_W)
            ks = pl.multiple_of(r0 * GRID_W, GRID_W)
            for pair in range(NA_HEADS // 2):
                cols = slice(pair * LANES, (pair + 1) * LANES)
                q2 = q_ref[pl.ds(qs, GRID_W), cols]
                k2 = k_ref[pl.ds(ks, kh * GRID_W), cols]
                items.append((qs, ks, cols))
                zero = jnp.zeros_like(q2)
                q_st = jnp.concatenate([jnp.where(low, q2, zero), jnp.where(low, zero, q2)], axis=0)
                bias = tbl_ref[variant, 2 * pair:2 * pair + 2].reshape(2 * GRID_W, kh * GRID_W)
                scores.append(_dot_nt(q_st, k2) + bias)
        probs = []
        for s in scores:
            m = jnp.max(s, axis=-1, keepdims=True)
            p = jnp.exp2(s - m)
            probs.append((p.astype(BF16), jnp.sum(p, axis=-1, keepdims=True)))
        for (qs, ks, cols), (p, den) in zip(items, probs):
            o = _dot(p, v_ref[pl.ds(ks, kh * GRID_W), cols]) / den
            o_ref[pl.ds(qs, GRID_W), cols] = jnp.where(low, o[:GRID_W], o[GRID_W:]).astype(BF16)
        return carry

    lax.fori_loop(0, NA_ROWS_PER_STEP // NA_GROUP, row_group, 0)


def _na_attention(za3, tbl):
    bsz, seq, _ = za3.shape
    rows = seq // GRID_W
    tq = NA_ROWS_PER_STEP * GRID_W
    return pl.pallas_call(
        functools.partial(_na_kernel, rows=rows),
        grid=(bsz, seq // tq),
        in_specs=[pl.BlockSpec((None, tq, D_A), lambda b, i: (b, i, 0)),
                  pl.BlockSpec((None, seq, D_A), lambda b, i: (b, 0, 1)),
                  pl.BlockSpec((None, seq, D_A), lambda b, i: (b, 0, 2)),
                  pl.BlockSpec(tbl.shape, lambda b, i: (0, 0, 0, 0))],
        out_specs=pl.BlockSpec((None, tq, D_A), lambda b, i: (b, i, 0)),
        out_shape=jax.ShapeDtypeStruct((bsz, seq, D_A), BF16),
        compiler_params=_params(2),
        name="na_attn",
    )(za3, za3, za3, tbl)


def _na_bias_tables(rpb, rows):
    kh = min(NA_KH_MAX, rows)
    col = np.arange(GRID_W)
    c_start = np.clip(col - NA_KW // 2, 0, GRID_W - NA_KW)
    valid = (col[None, :] >= c_start[:, None]) & (col[None, :] < c_start[:, None] + NA_KW)
    d_col = col[None, :] - col[:, None] + (NA_KW - 1)
    onehot = (d_col[:, :, None] == np.arange(2 * NA_KW - 1)[None, None, :]) & valid[:, :, None]
    by_row = jnp.stack([rpb[:, :, NA_KH_MAX - 1 - v:NA_KH_MAX - 1 - v + kh, :] for v in range(kh)], axis=1)
    t = jnp.einsum("lvhib,cdb->lvhcid", by_row.astype(F32), jnp.asarray(onehot, F32),
                   precision=lax.Precision.HIGHEST)
    t = jnp.where(valid[None, None, None, :, None, :], t * LOG2E, NEG_INF)
    return t.reshape(rpb.shape[0], kh, NA_HEADS, GRID_W, kh * GRID_W)


DIL_KB = DIL_BLK + 2 * DIL_HALF
DIL_GROUP = 8
DIL_GROUP_WHOLE = 2
DIL_MERGE_ROWS = 512
DIL_WINDOW_SHIFT = (DIL_HALF, 0, 2 * DIL_HALF)


def _dil_whole_class(seq, d):
    return seq // d <= 2 * DIL_BLK


def _dil_kernel(q_ref, k_ref, v_ref, *rest, seq):
    npat = len(DIL_PATTERNS)
    tbl_refs = rest[:npat]
    o_ref = rest[npat]
    stats = rest[npat + 1:]
    acc_refs, m_refs, l_refs = stats[:npat], stats[npat:2 * npat], stats[2 * npat:]

    lane = lax.broadcasted_iota(jnp.int32, (1, LANES), 1)
    low = lane < HEAD_DIM

    def attend_group(items, acc_ref, m_ref, l_ref):
        scores = []
        for _, q2, k2, _, bias in items:
            zero = jnp.zeros_like(q2)
            qs = jnp.concatenate([jnp.where(low, q2, zero), jnp.where(low, zero, q2)], axis=0)
            scores.append(_dot_nt(qs, k2) + bias())
        probs = []
        for s in scores:
            m = jnp.max(s, axis=-1, keepdims=True)
            p = jnp.exp2(s - m)
            probs.append((p.astype(BF16), m, jnp.sum(p, axis=-1, keepdims=True)))
        for (rows, q2, _, v2, _), (p, m, den) in zip(items, probs):
            nq = q2.shape[0]
            o = _dot(p, v2)
            acc_ref[rows, :] = jnp.where(low, o[:nq], o[nq:])
            m_ref[rows, :] = jnp.where(low, m[:nq], m[nq:])
            l_ref[rows, :] = jnp.where(low, den[:nq], den[nq:])

    for pat, (_, d) in enumerate(DIL_PATTERNS):
        length = seq // d
        tbl_ref, acc_ref, m_ref, l_ref = tbl_refs[pat], acc_refs[pat], m_refs[pat], l_refs[pat]

        if _dil_whole_class(seq, d):
            group = min(DIL_GROUP_WHOLE, d)

            def whole(g0, carry, d=d, length=length, group=group, tbl_ref=tbl_ref,
                      acc_ref=acc_ref, m_ref=m_ref, l_ref=l_ref):
                items = []
                for g in range(group):
                    r = g0 * group + g
                    rows = pl.ds(r, length, stride=d) if d > 1 else pl.ds(r, length)
                    q2 = q_ref[rows, :].astype(BF16)
                    items.append((rows, q2, k_ref[rows, :].astype(BF16), v_ref[rows, :].astype(BF16),
                                  lambda: tbl_ref[...].reshape(2 * length, length)))
                attend_group(items, acc_ref, m_ref, l_ref)
                return carry

            lax.fori_loop(0, d // group, whole, 0)
        else:
            nblk = length // DIL_BLK
            group = min(DIL_GROUP, nblk)

            def block(t0, carry, d=d, length=length, nblk=nblk, group=group, tbl_ref=tbl_ref,
                      acc_ref=acc_ref, m_ref=m_ref, l_ref=l_ref):
                items = []
                for g in range(group):
                    t = t0 * group + g
                    r = t // nblk
                    n = t % nblk
                    variant = jnp.where(n == 0, 1, 0) + jnp.where(n == nblk - 1, 2, 0)
                    q_start = r + d * DIL_BLK * n
                    k_start = r + d * jnp.clip(DIL_BLK * n - DIL_HALF, 0, length - DIL_KB)
                    if d == 1:
                        rows_q = pl.ds(q_start, DIL_BLK)
                        rows_k = pl.ds(k_start, DIL_KB)
                    else:
                        rows_q = pl.ds(q_start, DIL_BLK, stride=d)
                        rows_k = pl.ds(k_start, DIL_KB, stride=d)
                    q2 = q_ref[rows_q, :].astype(BF16)
                    items.append((rows_q, q2, k_ref[rows_k, :].astype(BF16), v_ref[rows_k, :].astype(BF16),
                                  lambda variant=variant: tbl_ref[variant].reshape(2 * DIL_BLK, DIL_KB)))
                attend_group(items, acc_ref, m_ref, l_ref)
                return carry

            lax.fori_loop(0, d * nblk // group, block, 0)

    def merge(c, carry):
        rows = pl.ds(pl.multiple_of(c * DIL_MERGE_ROWS, DIL_MERGE_ROWS), DIL_MERGE_ROWS)
        ms = [m_ref[rows, :] for m_ref in m_refs]
        m_all = functools.reduce(jnp.maximum, ms)
        num = jnp.zeros((DIL_MERGE_ROWS, LANES), F32)
        den = jnp.zeros((DIL_MERGE_ROWS, LANES), F32)
        for m_i, acc_ref, l_ref in zip(ms, acc_refs, l_refs):
            e = jnp.exp2(m_i - m_all)
            num = num + acc_ref[rows, :] * e
            den = den + l_ref[rows, :] * e
        o_ref[rows, :] = (num / den).astype(BF16)
        return carry

    lax.fori_loop(0, seq // DIL_MERGE_ROWS, merge, 0)


def _dil_attention(zb3, tbls):
    bsz, seq, _ = zb3.shape
    npair = DIL_HEADS // 2
    blk = lambda off: pl.BlockSpec((None, seq, LANES), lambda b, p: (b, 0, off + p))
    tbl_specs = []
    for t in tbls:
        if t.ndim == 3:
            tbl_specs.append(pl.BlockSpec((2,) + t.shape[1:], lambda b, p: (p, 0, 0)))
        else:
            tbl_specs.append(pl.BlockSpec((t.shape[0], 2) + t.shape[2:], lambda b, p: (0, p, 0, 0)))
    stat = pltpu.VMEM((seq, LANES), F32)
    return pl.pallas_call(
        functools.partial(_dil_kernel, seq=seq),
        grid=(bsz, npair),
        in_specs=[blk(0), blk(npair), blk(2 * npair)] + tbl_specs,
        out_specs=pl.BlockSpec((None, seq, LANES), lambda b, p: (b, 0, p)),
        out_shape=jax.ShapeDtypeStruct((bsz, seq, D_B), BF16),
        scratch_shapes=[stat] * (3 * len(DIL_PATTERNS)),
        compiler_params=_params(2),
        name="dil_attn",
    )(zb3, zb3, zb3, *tbls)


def _t5_bucket(rel):
    nb = T5_BUCKETS // 2
    max_exact = nb // 2
    n = np.abs(rel)
    large = max_exact + (np.log(np.maximum(n, 1) / max_exact)
                         / math.log(T5_MAX_DIST / max_exact) * (nb - max_exact)).astype(np.int64)
    large = np.minimum(large, nb - 1)
    return (np.where(rel > 0, nb, 0) + np.where(n < max_exact, n, large)).astype(np.int32)


def _t5_bias(t5_table, rel):
    onehot = _t5_bucket(rel)[:, :, None] == np.arange(T5_BUCKETS)[None, None, :]
    return jnp.einsum("qkb,bh->hqk", jnp.asarray(onehot, F32), t5_table.astype(F32),
                      precision=lax.Precision.HIGHEST)


def _dil_bias_tables(t5_table, seq):
    tables = []
    for _, d in DIL_PATTERNS:
        if _dil_whole_class(seq, d):
            pos = np.arange(seq // d)
            rel = pos[None, :] - pos[:, None]
            valid = np.abs(rel) <= DIL_HALF
            bias = _t5_bias(t5_table, rel * d)
            tables.append(jnp.where(valid[None], bias * LOG2E, NEG_INF))
        else:
            a = np.arange(DIL_BLK)[:, None]
            j = np.arange(DIL_KB)[None, :]
            per_position = []
            for shift in DIL_WINDOW_SHIFT:
                rel = j - shift - a
                per_position.append(jnp.where((np.abs(rel) <= DIL_HALF)[None], _t5_bias(t5_table, rel * d) * LOG2E,
                                              NEG_INF))
            tables.append(jnp.stack(per_position))
    return tables


MLA_TQ = 1024
MLA_CQ = 512
MLA_VT_ROWS = 80


def _mla_kernel(q_ref, k_ref, vt_ref, o_ref):
    chains = [(h, c) for c in range(MLA_TQ // MLA_CQ) for h in range(MLA_HEADS)]

    def scores(h, c):
        cols = slice(h * LANES, (h + 1) * LANES)
        return _dot_nt(k_ref[:, cols], q_ref[c * MLA_CQ:(c + 1) * MLA_CQ, cols])

    outs = []
    s_next = scores(*chains[0])
    for idx, (h, c) in enumerate(chains):
        s = s_next
        if idx + 1 < len(chains):
            s_next = scores(*chains[idx + 1])
        sb = s.astype(BF16)
        p = jnp.exp2(sb - jnp.max(sb, axis=0, keepdims=True))
        ov = _dot(vt_ref[h * MLA_VT_ROWS:(h + 1) * MLA_VT_ROWS, :], p)
        outs.append(ov[0:MLA_V] / ov[MLA_V:MLA_V + 1])
        if h == MLA_HEADS - 1:
            o_ref[c * MLA_CQ:(c + 1) * MLA_CQ, :] = jnp.concatenate(outs, axis=0).T.astype(BF16)
            outs = []


def _mla_attention(qm3, km3, vt3):
    bsz, seq, _ = qm3.shape
    return pl.pallas_call(
        _mla_kernel,
        grid=(bsz, seq // MLA_TQ),
        in_specs=[pl.BlockSpec((None, MLA_TQ, MLA_QK_W), lambda b, i: (b, i, 0)),
                  pl.BlockSpec((None, seq, MLA_QK_W), lambda b, i: (b, 0, 0)),
                  pl.BlockSpec((None, MLA_HEADS * MLA_VT_ROWS, seq), lambda b, i: (b, 0, 0))],
        out_specs=pl.BlockSpec((None, MLA_TQ, D_C), lambda b, i: (b, i, 0)),
        out_shape=jax.ShapeDtypeStruct((bsz, seq, D_C), BF16),
        compiler_params=_params(2),
        name="mla_attn",
    )(qm3, km3, vt3)


def _out_kernel(x_ref, oa_ref, ob_ref, oc_ref, w1_ref, w2_ref, w3_ref, g_ref, gate_ref, o_ref):
    tm = x_ref.shape[0]
    for r0 in range(0, tm, tm // 2):
        rows = slice(r0, r0 + tm // 2)
        y = (_dot(oa_ref[rows, :], w1_ref[...]) + _dot(ob_ref[rows, :], w2_ref[...])
             + _dot(oc_ref[rows, :], w3_ref[...]))
        o_ref[rows, :] = x_ref[rows, :] + gate_ref[...] * _rms(y, g_ref[...])


def _out_proj(x, oa, ob, oc, w1, w2, w3, g_post, mod3, l, bsz, seq):
    n = x.shape[0]
    tm = 1024
    tps = seq // tm
    row = lambda i: (i, 0)
    const = lambda i: (0, 0)
    full = lambda a: pl.BlockSpec(a.shape, const)
    return pl.pallas_call(
        _out_kernel,
        grid=(n // tm,),
        in_specs=[pl.BlockSpec((tm, D_MODEL), row),
                  pl.BlockSpec((tm, D_A), row), pl.BlockSpec((tm, D_B), row), pl.BlockSpec((tm, D_C), row),
                  full(w1), full(w2), full(w3), full(g_post),
                  pl.BlockSpec((None, 1, D_MODEL), lambda i: ((l * bsz + i // tps) * N_MOD + 2, 0, 0))],
        out_specs=pl.BlockSpec((tm, D_MODEL), row),
        out_shape=jax.ShapeDtypeStruct((n, D_MODEL), F32),
        compiler_params=_params(1),
        name="out_proj",
    )(x, oa, ob, oc, w1, w2, w3, g_post, mod3)


FFN_TM = 512
FFN_CHUNK = 256


def _ffn_kernel(x_ref, xp_ref, xn_ref, g_ref, sh_ref, sc_ref, wa_ref, wg_ref, cw_ref, cb_ref, wd_ref,
                gpost_ref, gate_ref, o_ref, u_ref, *, tiles_per_seq):
    i = pl.program_id(0)
    has_prev = (i % tiles_per_seq) != 0
    has_next = (i % tiles_per_seq) != tiles_per_seq - 1
    g = g_ref[...]
    sc = 1.0 + sc_ref[...]
    sh = sh_ref[...]
    x = x_ref[...]
    pre = lambda t: _rms(t, g) * sc + sh
    h_prev = jnp.where(has_prev, pre(xp_ref[...]), 0.0)
    h_next = jnp.where(has_next, pre(xn_ref[...]), 0.0)
    h_mid = pre(x)
    h = h_mid.astype(BF16)
    h_ext = jnp.concatenate([h_prev, h_mid, h_next], axis=0).astype(BF16)
    tm = x.shape[0]
    ext = tm + 2 * SUBLANES
    mid = slice(SUBLANES, SUBLANES + tm)
    for c0 in range(0, D_FF, FFN_CHUNK):
        cols = slice(c0, c0 + FFN_CHUNK)
        a = _dot(h, wa_ref[:, cols])
        ge = _dot(h_ext, wg_ref[:, cols])
        g_prev = pltpu.roll(ge, 1, axis=0)[mid]
        g_next = pltpu.roll(ge, ext - 1, axis=0)[mid]
        gc = cb_ref[:, cols] + g_prev * cw_ref[0:1, cols]
        gc = gc + ge[mid] * cw_ref[1:2, cols]
        gc = gc + g_next * cw_ref[2:3, cols]
        u_ref[:, cols] = (jax.nn.gelu(gc) * a).astype(BF16)
    for r0 in range(0, tm, tm // 2):
        rows = slice(r0, r0 + tm // 2)
        y = _dot(u_ref[rows, :], wd_ref[...])
        o_ref[rows, :] = x[rows] + gate_ref[...] * _rms(y, gpost_ref[...])


def _ffn(x, mod3, l, bsz, seq, g_pre, wa, wg, cw, cb, wd, g_post):
    n = x.shape[0]
    tm = FFN_TM
    tps = seq // tm
    r8 = tm // SUBLANES
    last8 = n // SUBLANES - 1
    const = lambda i: (0, 0)
    full = lambda a: pl.BlockSpec(a.shape, const)
    weight = lambda a: pl.BlockSpec(a.shape, const, pipeline_mode=pl.Buffered(1))
    mrow = lambda k: (lambda i: ((l * bsz + i // tps) * N_MOD + k, 0, 0))
    return pl.pallas_call(
        functools.partial(_ffn_kernel, tiles_per_seq=tps),
        grid=(n // tm,),
        in_specs=[pl.BlockSpec((tm, D_MODEL), lambda i: (i, 0)),
                  pl.BlockSpec((SUBLANES, D_MODEL), lambda i: (jnp.maximum(i * r8 - 1, 0), 0)),
                  pl.BlockSpec((SUBLANES, D_MODEL), lambda i: (jnp.minimum((i + 1) * r8, last8), 0)),
                  full(g_pre),
                  pl.BlockSpec((None, 1, D_MODEL), mrow(3)),
                  pl.BlockSpec((None, 1, D_MODEL), mrow(4)),
                  weight(wa), weight(wg), full(cw), full(cb), weight(wd), full(g_post),
                  pl.BlockSpec((None, 1, D_MODEL), mrow(5))],
        out_specs=pl.BlockSpec((tm, D_MODEL), lambda i: (i, 0)),
        out_shape=jax.ShapeDtypeStruct((n, D_MODEL), F32),
        scratch_shapes=[pltpu.VMEM((tm, D_FF), BF16)],
        compiler_params=_params(1),
        name="conv_ffn",
    )(x, x, x, g_pre, mod3, mod3, wa, wg, cw, cb, wd, g_post, mod3)


def _rope_tables(seq):
    inv_freq = jnp.asarray(ROPE_THETA ** (-np.arange(0, MLA_ROPE, 2, dtype=np.float32) / MLA_ROPE), F32)
    ang = jnp.arange(seq, dtype=F32)[:, None] * inv_freq[None, :]
    cos, sin = jnp.cos(ang), jnp.sin(ang)
    cos2 = jnp.concatenate([cos, cos], axis=1)
    sin2 = jnp.concatenate([sin, sin], axis=1)
    head_c = jnp.concatenate([cos2, jnp.ones((seq, MLA_NOPE), F32), jnp.zeros((seq, LANES - MLA_NOPE - MLA_ROPE), F32)], axis=1)
    head_s = jnp.concatenate([sin2, jnp.zeros((seq, LANES - MLA_ROPE), F32)], axis=1)
    cq = jnp.concatenate([head_c] * MLA_HEADS, axis=1)
    sq = jnp.concatenate([head_s] * MLA_HEADS, axis=1)
    ck = jnp.concatenate([cos2, jnp.zeros((seq, LANES - MLA_ROPE), F32)], axis=1)
    return cq, sq, ck, head_s


def _layer_weights(w_in, w_uq, w_ukv, w_out, w_up, w_down):
    d = w_in.shape[0]
    o_b = 3 * D_A
    o_c = o_b + 3 * D_B
    o_kr = o_c + MLA_Q_LORA + MLA_KV_LORA
    wa = w_in[:, :o_b].astype(BF16)
    wb = w_in[:, o_b:o_c].astype(BF16)
    w_kr = w_in[:, o_kr:o_kr + MLA_ROPE]
    zpad = jnp.zeros((d, LANES - MLA_ROPE), F32)
    wc = jnp.concatenate([w_in[:, o_c:o_kr], w_kr, zpad], axis=1).astype(BF16)

    hq = MLA_NOPE + MLA_ROPE
    q_cols, k_cols, v_cols = [], [], []
    zq = jnp.zeros((MLA_Q_LORA, LANES - hq), F32)
    for h in range(MLA_HEADS):
        nope = w_uq[:, h * hq:h * hq + MLA_NOPE]
        rope = w_uq[:, h * hq + MLA_NOPE:(h + 1) * hq]
        q_cols += [rope, nope, zq]
        hk = MLA_NOPE + MLA_V
        k_cols += [jnp.zeros((MLA_KV_LORA, MLA_ROPE), F32), w_ukv[:, h * hk:h * hk + MLA_NOPE],
                   jnp.zeros((MLA_KV_LORA, LANES - hq), F32)]
        v_cols += [w_ukv[:, h * hk + MLA_NOPE:(h + 1) * hk], jnp.zeros((MLA_KV_LORA, MLA_VT_ROWS - MLA_V), F32)]
    cat = lambda cols: jnp.concatenate(cols, axis=1).astype(BF16)
    w1 = w_out[:D_A].astype(BF16)
    w2 = w_out[D_A:D_A + D_B].astype(BF16)
    w3 = w_out[D_A + D_B:].astype(BF16)
    w_a = w_up[:, :D_FF].astype(BF16)
    w_g = w_up[:, D_FF:].astype(BF16)
    return (wa, wb, wc, cat(q_cols), cat(k_cols), cat(v_cols).T, w1, w2, w3, w_a, w_g,
            w_down.astype(BF16))


def kernel(x, c, w_ada, b_ada, g_pre_mix, g_post_mix, g_pre_ffn, g_post_ffn, w_in, na_rpb, t5_table,
           mla_g_q, mla_g_kv, w_uq, w_ukv, w_out, w_up, conv_w, conv_b, w_down):
    bsz, seq, d = x.shape
    depth = w_in.shape[0]
    n = bsz * seq
    rows = seq // GRID_W
    assert d == D_MODEL and seq % (DIL_BLK * max(dd for _, dd in DIL_PATTERNS)) == 0 and rows >= NA_KH_MAX

    mod3 = _modulation(c, w_ada, b_ada).reshape(depth * bsz * N_MOD, 1, D_MODEL)
    cq, sq, ck, sk = _rope_tables(seq)
    dil_tbls = _dil_bias_tables(t5_table, seq)
    na_tbls = _na_bias_tables(na_rpb, rows)
    row2 = lambda v: v.reshape(1, -1)

    xf = x.reshape(n, d)
    for l in range(depth):
        (wa, wb, wc, wuq, wuk, wuv, w1, w2, w3, w_a, w_g, w_d) = _layer_weights(
            w_in[l], w_uq[l], w_ukv[l], w_out[l], w_up[l], w_down[l])
        za, zb, qm, km, vt = _in_proj(xf, mod3, l, bsz, seq, row2(g_pre_mix[l]), wa, wb, wc,
                                      row2(mla_g_q[l]), row2(mla_g_kv[l]), wuq, wuk, wuv,
                                      cq, sq, ck, sk)
        o_a = _na_attention(za.reshape(bsz, seq, 3 * D_A), na_tbls[l])
        o_b = _dil_attention(zb.reshape(bsz, seq, 3 * D_B), dil_tbls)
        o_c = _mla_attention(qm.reshape(bsz, seq, MLA_QK_W), km.reshape(bsz, seq, MLA_QK_W), vt)
        xf = _out_proj(xf, o_a.reshape(n, D_A), o_b.reshape(n, D_B), o_c.reshape(n, D_C),
                       w1, w2, w3, row2(g_post_mix[l]), mod3, l, bsz, seq)
        xf = _ffn(xf, mod3, l, bsz, seq, row2(g_pre_ffn[l]), w_a, w_g, conv_w[l], row2(conv_b[l]), w_d,
                  row2(g_post_ffn[l]))
    return xf.reshape(bsz, seq, d)
```

```python
import functools
import math

import numpy as np
import jax
import jax.numpy as jnp
from jax import lax
from jax.experimental import pallas as pl
from jax.experimental.pallas import tpu as pltpu

F32 = jnp.float32
BF16 = jnp.bfloat16

D_MODEL = 1024
HEAD_DIM = 64
GRID_W = 64
EPS = 1e-6
NEG_INF = -1e30

NA_HEADS = 4
NA_KH_MAX = 8
NA_KW = 16

DIL_HEADS = 8
DIL_PATTERNS = ((128, 1), (512, 4), (2048, 16))
DIL_BLK = 128
DIL_HALF = 64

MLA_HEADS = 4
MLA_Q_LORA = 384
MLA_KV_LORA = 256
MLA_NOPE = 64
MLA_ROPE = 32
MLA_V = 64
ROPE_THETA = 10000.0

T5_BUCKETS = 32
T5_MAX_DIST = 1024

D_FF = 2816
CONV_W = 3

D_A = NA_HEADS * HEAD_DIM
D_B = DIL_HEADS * HEAD_DIM
D_C = MLA_HEADS * MLA_V
N_MOD = 6

LANES = 128
SUBLANES = 8
VMEM_LIMIT = 56 * 1024 * 1024

LOG2E = math.log2(math.e)
Q_SCALE = HEAD_DIM ** -0.5 * LOG2E
MLA_Q_SCALE = (MLA_NOPE + MLA_ROPE) ** -0.5 * LOG2E

ZC_W = MLA_Q_LORA + MLA_KV_LORA + LANES
MLA_QK_W = MLA_HEADS * LANES


def _params(n_axes):
    return pltpu.CompilerParams(dimension_semantics=("arbitrary",) * n_axes,
                                vmem_limit_bytes=VMEM_LIMIT)


def _dot(a, b):
    return jnp.dot(a, b, preferred_element_type=F32)


def _dot_nt(a, b):
    return lax.dot_general(a, b, (((1,), (1,)), ((), ())), preferred_element_type=F32)


def _rms(x, g):
    return x * lax.rsqrt(jnp.mean(x * x, axis=-1, keepdims=True) + EPS) * g


def _mod_kernel(c_ref, w_ref, b_ref, o_ref):
    ca = jax.nn.silu(c_ref[...]).astype(BF16)
    o_ref[0] = _dot(ca, w_ref[0].astype(BF16)) + b_ref[0]


def _modulation(c, w_ada, b_ada):
    depth, d, n = w_ada.shape
    bsz = c.shape[0]
    tn = 1536
    return pl.pallas_call(
        _mod_kernel,
        grid=(depth, n // tn),
        in_specs=[pl.BlockSpec((bsz, d), lambda l, j: (0, 0)),
                  pl.BlockSpec((1, d, tn), lambda l, j: (l, 0, j)),
                  pl.BlockSpec((1, 1, tn), lambda l, j: (l, 0, j))],
        out_specs=pl.BlockSpec((1, bsz, tn), lambda l, j: (l, 0, j)),
        out_shape=jax.ShapeDtypeStruct((depth, bsz, n), F32),
        compiler_params=_params(2),
        name="adaln_mod",
    )(c, w_ada, b_ada.reshape(depth, 1, n))


def _in_kernel(x_ref, g_ref, sh_ref, sc_ref, wa_ref, wb_ref, wc_ref, gq_ref, gkv_ref,
               wuq_ref, wuk_ref, wuvt_ref, cq_ref, sq_ref, ck_ref, sk_ref,
               za_ref, zb_ref, qm_ref, km_ref, vt_ref):
    x = x_ref[...]
    h = (_rms(x, g_ref[...]) * (1.0 + sc_ref[...]) + sh_ref[...]).astype(BF16)
    zc = _dot(h, wc_ref[...])
    c_q = zc[:, 0:MLA_Q_LORA]
    c_kv = zc[:, MLA_Q_LORA:MLA_Q_LORA + MLA_KV_LORA]
    k_rope = zc[:, MLA_Q_LORA + MLA_KV_LORA:ZC_W]
    cqn = _rms(c_q, gq_ref[...]).astype(BF16)
    ckvn = _rms(c_kv, gkv_ref[...]).astype(BF16)
    q = _dot(cqn, wuq_ref[...])
    k_nope = _dot(ckvn, wuk_ref[...])
    vt = _dot_nt(wuvt_ref[...], ckvn)
    za_ref[:, 0:D_A] = (_dot(h, wa_ref[:, 0:D_A]) * Q_SCALE).astype(BF16)
    za_ref[:, D_A:3 * D_A] = _dot(h, wa_ref[:, D_A:3 * D_A]).astype(BF16)
    zb_ref[:, 0:D_B] = _dot(h, wb_ref[:, 0:D_B]) * Q_SCALE
    for j in range(D_B, 3 * D_B, D_B):
        zb_ref[:, j:j + D_B] = _dot(h, wb_ref[:, j:j + D_B])
    q = q * cq_ref[...] + _rotate_half_lanes(q) * sq_ref[...]
    qm_ref[...] = (q * MLA_Q_SCALE).astype(BF16)
    kr = k_rope * ck_ref[...] + _rotate_half_lanes(k_rope) * sk_ref[...]
    km_ref[...] = (k_nope + jnp.concatenate([kr] * MLA_HEADS, axis=1)).astype(BF16)
    vrow = lax.broadcasted_iota(jnp.int32, (MLA_HEADS * MLA_VT_ROWS, 1), 0)
    ones_rows = functools.reduce(jnp.logical_or, [vrow == h * MLA_VT_ROWS + MLA_V for h in range(MLA_HEADS)])
    vt_ref[...] = (vt + jnp.where(ones_rows, 1.0, 0.0)).astype(BF16)


def _rotate_half_lanes(x):
    half = MLA_ROPE // 2
    width = x.shape[1]
    lane = lax.broadcasted_iota(jnp.int32, (1, width), 1)
    from_above = pltpu.roll(x, width - half, axis=1)
    from_below = pltpu.roll(x, half, axis=1)
    return jnp.where(lane % LANES < half, -from_above, from_below)


def _in_proj(x, mod3, l, bsz, seq, g_pre, wa, wb, wc, gq, gkv, wuq, wuk, wuv, cq, sq, ck, sk):
    n = x.shape[0]
    tm = 1024
    tps = seq // tm
    row = lambda i: (i, 0)
    const = lambda i: (0, 0)
    pos = lambda i: (i % tps, 0)
    mrow = lambda k: (lambda i: ((l * bsz + i // tps) * N_MOD + k, 0, 0))
    full = lambda a: pl.BlockSpec(a.shape, const)
    return pl.pallas_call(
        _in_kernel,
        grid=(n // tm,),
        in_specs=[pl.BlockSpec((tm, D_MODEL), row),
                  full(g_pre),
                  pl.BlockSpec((None, 1, D_MODEL), mrow(0)),
                  pl.BlockSpec((None, 1, D_MODEL), mrow(1)),
                  full(wa), full(wb), full(wc), full(gq), full(gkv),
                  full(wuq), full(wuk), full(wuv),
                  pl.BlockSpec((tm, MLA_QK_W), pos), pl.BlockSpec((tm, MLA_QK_W), pos),
                  pl.BlockSpec((tm, LANES), pos), pl.BlockSpec((tm, LANES), pos)],
        out_specs=[pl.BlockSpec((tm, 3 * D_A), row),
                   pl.BlockSpec((tm, 3 * D_B), row),
                   pl.BlockSpec((tm, MLA_QK_W), row),
                   pl.BlockSpec((tm, MLA_QK_W), row),
                   pl.BlockSpec((None, MLA_HEADS * MLA_VT_ROWS, tm), lambda i: (i // tps, 0, i % tps))],
        out_shape=[jax.ShapeDtypeStruct((n, 3 * D_A), BF16),
                   jax.ShapeDtypeStruct((n, 3 * D_B), F32),
                   jax.ShapeDtypeStruct((n, MLA_QK_W), BF16),
                   jax.ShapeDtypeStruct((n, MLA_QK_W), BF16),
                   jax.ShapeDtypeStruct((bsz, MLA_HEADS * MLA_VT_ROWS, seq), BF16)],
        compiler_params=_params(1),
        name="in_proj",
    )(x, g_pre, mod3, mod3, wa, wb, wc, gq, gkv, wuq, wuk, wuv, cq, sq, ck, sk)


NA_ROWS_PER_STEP = 16
NA_GROUP = 4


def _na_kernel(q_ref, k_ref, v_ref, tbl_ref, o_ref, *, rows):
    i = pl.program_id(1)
    lane = lax.broadcasted_iota(jnp.int32, (1, LANES), 1)
    low = lane < HEAD_DIM
    kh = NA_KH_MAX

    def row_group(j0, carry):
        items = []
        scores = []
        for g in range(NA_GROUP):
            j = j0 * NA_GROUP + g
            r = i * NA_ROWS_PER_STEP + j
            r0 = jnp.clip(r - kh // 2, 0, rows - kh)
            variant = r - r0
            qs = pl.multiple_of(j * GRID_W, GRID_W)
            ks = pl.multiple_of(r0 * GRID_W, GRID_W)
            for pair in range(NA_HEADS // 2):
                cols = slice(pair * LANES, (pair + 1) * LANES)
                q2 = q_ref[pl.ds(qs, GRID_W), cols]
                k2 = k_ref[pl.ds(ks, kh * GRID_W), cols]
                items.append((qs, ks, cols))
                zero = jnp.zeros_like(q2)
                q_st = jnp.concatenate([jnp.where(low, q2, zero), jnp.where(low, zero, q2)], axis=0)
                bias = tbl_ref[variant, 2 * pair:2 * pair + 2].reshape(2 * GRID_W, kh * GRID_W)
                scores.append(_dot_nt(q_st, k2) + bias)
        probs = []
        for s in scores:
            m = jnp.max(s, axis=-1, keepdims=True)
            p = jnp.exp2(s - m)
            probs.append((p.astype(BF16), jnp.sum(p, axis=-1, keepdims=True)))
        for (qs, ks, cols), (p, den) in zip(items, probs):
            o = _dot(p, v_ref[pl.ds(ks, kh * GRID_W), cols]) / den
            o_ref[pl.ds(qs, GRID_W), cols] = jnp.where(low, o[:GRID_W], o[GRID_W:]).astype(BF16)
        return carry

    lax.fori_loop(0, NA_ROWS_PER_STEP // NA_GROUP, row_group, 0)


def _na_attention(za3, tbl):
    bsz, seq, _ = za3.shape
    rows = seq // GRID_W
    tq = NA_ROWS_PER_STEP * GRID_W
    return pl.pallas_call(
        functools.partial(_na_kernel, rows=rows),
        grid=(bsz, seq // tq),
        in_specs=[pl.BlockSpec((None, tq, D_A), lambda b, i: (b, i, 0)),
                  pl.BlockSpec((None, seq, D_A), lambda b, i: (b, 0, 1)),
                  pl.BlockSpec((None, seq, D_A), lambda b, i: (b, 0, 2)),
                  pl.BlockSpec(tbl.shape, lambda b, i: (0, 0, 0, 0))],
        out_specs=pl.BlockSpec((None, tq, D_A), lambda b, i: (b, i, 0)),
        out_shape=jax.ShapeDtypeStruct((bsz, seq, D_A), BF16),
        compiler_params=_params(2),
        name="na_attn",
    )(za3, za3, za3, tbl)


def _na_bias_tables(rpb, rows):
    kh = min(NA_KH_MAX, rows)
    col = np.arange(GRID_W)
    c_start = np.clip(col - NA_KW // 2, 0, GRID_W - NA_KW)
    valid = (col[None, :] >= c_start[:, None]) & (col[None, :] < c_start[:, None] + NA_KW)
    d_col = col[None, :] - col[:, None] + (NA_KW - 1)
    onehot = (d_col[:, :, None] == np.arange(2 * NA_KW - 1)[None, None, :]) & valid[:, :, None]
    by_row = jnp.stack([rpb[:, :, NA_KH_MAX - 1 - v:NA_KH_MAX - 1 - v + kh, :] for v in range(kh)], axis=1)
    t = jnp.einsum("lvhib,cdb->lvhcid", by_row.astype(F32), jnp.asarray(onehot, F32),
                   precision=lax.Precision.HIGHEST)
    t = jnp.where(valid[None, None, None, :, None, :], t * LOG2E, NEG_INF)
    return t.reshape(rpb.shape[0], kh, NA_HEADS, GRID_W, kh * GRID_W)


DIL_KB = DIL_BLK + 2 * DIL_HALF
DIL_GROUP = 8
DIL_GROUP_WHOLE = 2
DIL_MERGE_ROWS = 512
DIL_WINDOW_SHIFT = (DIL_HALF, 0, 2 * DIL_HALF)


def _dil_whole_class(seq, d):
    return seq // d <= 2 * DIL_BLK


def _dil_kernel(q_ref, k_ref, v_ref, *rest, seq):
    npat = len(DIL_PATTERNS)
    tbl_refs = rest[:npat]
    o_ref = rest[npat]
    stats = rest[npat + 1:]
    acc_refs, m_refs, l_refs = stats[:npat], stats[npat:2 * npat], stats[2 * npat:]

    lane = lax.broadcasted_iota(jnp.int32, (1, LANES), 1)
    low = lane < HEAD_DIM

    def attend_group(items, acc_ref, m_ref, l_ref):
        scores = []
        for _, q2, k2, _, bias in items:
            zero = jnp.zeros_like(q2)
            qs = jnp.concatenate([jnp.where(low, q2, zero), jnp.where(low, zero, q2)], axis=0)
            scores.append(_dot_nt(qs, k2) + bias())
        probs = []
        for s in scores:
            m = jnp.max(s, axis=-1, keepdims=True)
            p = jnp.exp2(s - m)
            probs.append((p.astype(BF16), m, jnp.sum(p, axis=-1, keepdims=True)))
        for (rows, q2, _, v2, _), (p, m, den) in zip(items, probs):
            nq = q2.shape[0]
            o = _dot(p, v2)
            acc_ref[rows, :] = jnp.where(low, o[:nq], o[nq:])
            m_ref[rows, :] = jnp.where(low, m[:nq], m[nq:])
            l_ref[rows, :] = jnp.where(low, den[:nq], den[nq:])

    for pat, (_, d) in enumerate(DIL_PATTERNS):
        length = seq // d
        tbl_ref, acc_ref, m_ref, l_ref = tbl_refs[pat], acc_refs[pat], m_refs[pat], l_refs[pat]

        if _dil_whole_class(seq, d):
            group = min(DIL_GROUP_WHOLE, d)

            def whole(g0, carry, d=d, length=length, group=group, tbl_ref=tbl_ref,
                      acc_ref=acc_ref, m_ref=m_ref, l_ref=l_ref):
                items = []
                for g in range(group):
                    r = g0 * group + g
                    rows = pl.ds(r, length, stride=d) if d > 1 else pl.ds(r, length)
                    q2 = q_ref[rows, :].astype(BF16)
                    items.append((rows, q2, k_ref[rows, :].astype(BF16), v_ref[rows, :].astype(BF16),
                                  lambda: tbl_ref[...].reshape(2 * length, length)))
                attend_group(items, acc_ref, m_ref, l_ref)
                return carry

            lax.fori_loop(0, d // group, whole, 0)
        else:
            nblk = length // DIL_BLK
            group = min(DIL_GROUP, nblk)

            def block(t0, carry, d=d, length=length, nblk=nblk, group=group, tbl_ref=tbl_ref,
                      acc_ref=acc_ref, m_ref=m_ref, l_ref=l_ref):
                items = []
                for g in range(group):
                    t = t0 * group + g
                    r = t // nblk
                    n = t % nblk
                    variant = jnp.where(n == 0, 1, 0) + jnp.where(n == nblk - 1, 2, 0)
                    q_start = r + d * DIL_BLK * n
                    k_start = r + d * jnp.clip(DIL_BLK * n - DIL_HALF, 0, length - DIL_KB)
                    if d == 1:
                        rows_q = pl.ds(q_start, DIL_BLK)
                        rows_k = pl.ds(k_start, DIL_KB)
                    else:
                        rows_q = pl.ds(q_start, DIL_BLK, stride=d)
                        rows_k = pl.ds(k_start, DIL_KB, stride=d)
                    q2 = q_ref[rows_q, :].astype(BF16)
                    items.append((rows_q, q2, k_ref[rows_k, :].astype(BF16), v_ref[rows_k, :].astype(BF16),
                                  lambda variant=variant: tbl_ref[variant].reshape(2 * DIL_BLK, DIL_KB)))
                attend_group(items, acc_ref, m_ref, l_ref)
                return carry

            lax.fori_loop(0, d * nblk // group, block, 0)

    def merge(c, carry):
        rows = pl.ds(pl.multiple_of(c * DIL_MERGE_ROWS, DIL_MERGE_ROWS), DIL_MERGE_ROWS)
        ms = [m_ref[rows, :] for m_ref in m_refs]
        m_all = functools.reduce(jnp.maximum, ms)
        num = jnp.zeros((DIL_MERGE_ROWS, LANES), F32)
        den = jnp.zeros((DIL_MERGE_ROWS, LANES), F32)
        for m_i, acc_ref, l_ref in zip(ms, acc_refs, l_refs):
            e = jnp.exp2(m_i - m_all)
            num = num + acc_ref[rows, :] * e
            den = den + l_ref[rows, :] * e
        o_ref[rows, :] = (num / den).astype(BF16)
        return carry

    lax.fori_loop(0, seq // DIL_MERGE_ROWS, merge, 0)


def _dil_attention(zb3, tbls):
    bsz, seq, _ = zb3.shape
    npair = DIL_HEADS // 2
    blk = lambda off: pl.BlockSpec((None, seq, LANES), lambda b, p: (b, 0, off + p))
    tbl_specs = []
    for t in tbls:
        if t.ndim == 3:
            tbl_specs.append(pl.BlockSpec((2,) + t.shape[1:], lambda b, p: (p, 0, 0)))
        else:
            tbl_specs.append(pl.BlockSpec((t.shape[0], 2) + t.shape[2:], lambda b, p: (0, p, 0, 0)))
    stat = pltpu.VMEM((seq, LANES), F32)
    return pl.pallas_call(
        functools.partial(_dil_kernel, seq=seq),
        grid=(bsz, npair),
        in_specs=[blk(0), blk(npair), blk(2 * npair)] + tbl_specs,
        out_specs=pl.BlockSpec((None, seq, LANES), lambda b, p: (b, 0, p)),
        out_shape=jax.ShapeDtypeStruct((bsz, seq, D_B), BF16),
        scratch_shapes=[stat] * (3 * len(DIL_PATTERNS)),
        compiler_params=_params(2),
        name="dil_attn",
    )(zb3, zb3, zb3, *tbls)


def _t5_bucket(rel):
    nb = T5_BUCKETS // 2
    max_exact = nb // 2
    n = np.abs(rel)
    large = max_exact + (np.log(np.maximum(n, 1) / max_exact)
                         / math.log(T5_MAX_DIST / max_exact) * (nb - max_exact)).astype(np.int64)
    large = np.minimum(large, nb - 1)
    return (np.where(rel > 0, nb, 0) + np.where(n < max_exact, n, large)).astype(np.int32)


def _t5_bias(t5_table, rel):
    onehot = _t5_bucket(rel)[:, :, None] == np.arange(T5_BUCKETS)[None, None, :]
    return jnp.einsum("qkb,bh->hqk", jnp.asarray(onehot, F32), t5_table.astype(F32),
                      precision=lax.Precision.HIGHEST)


def _dil_bias_tables(t5_table, seq):
    tables = []
    for _, d in DIL_PATTERNS:
        if _dil_whole_class(seq, d):
            pos = np.arange(seq // d)
            rel = pos[None, :] - pos[:, None]
            valid = np.abs(rel) <= DIL_HALF
            bias = _t5_bias(t5_table, rel * d)
            tables.append(jnp.where(valid[None], bias * LOG2E, NEG_INF))
        else:
            a = np.arange(DIL_BLK)[:, None]
            j = np.arange(DIL_KB)[None, :]
            per_position = []
            for shift in DIL_WINDOW_SHIFT:
                rel = j - shift - a
                per_position.append(jnp.where((np.abs(rel) <= DIL_HALF)[None], _t5_bias(t5_table, rel * d) * LOG2E,
                                              NEG_INF))
            tables.append(jnp.stack(per_position))
    return tables


MLA_TQ = 1024
MLA_CQ = 512
MLA_VT_ROWS = 80


def _mla_kernel(q_ref, k_ref, vt_ref, o_ref):
    chains = [(h, c) for c in range(MLA_TQ // MLA_CQ) for h in range(MLA_HEADS)]

    def scores(h, c):
        cols = slice(h * LANES, (h + 1) * LANES)
        return _dot_nt(k_ref[:, cols], q_ref[c * MLA_CQ:(c + 1) * MLA_CQ, cols])

    outs = []
    s_next = scores(*chains[0])
    for idx, (h, c) in enumerate(chains):
        s = s_next
        if idx + 1 < len(chains):
            s_next = scores(*chains[idx + 1])
        sb = s.astype(BF16)
        p = jnp.exp2(sb - jnp.max(sb, axis=0, keepdims=True))
        ov = _dot(vt_ref[h * MLA_VT_ROWS:(h + 1) * MLA_VT_ROWS, :], p)
        outs.append(ov[0:MLA_V] / ov[MLA_V:MLA_V + 1])
        if h == MLA_HEADS - 1:
            o_ref[c * MLA_CQ:(c + 1) * MLA_CQ, :] = jnp.concatenate(outs, axis=0).T.astype(BF16)
            outs = []


def _mla_attention(qm3, km3, vt3):
    bsz, seq, _ = qm3.shape
    return pl.pallas_call(
        _mla_kernel,
        grid=(bsz, seq // MLA_TQ),
        in_specs=[pl.BlockSpec((None, MLA_TQ, MLA_QK_W), lambda b, i: (b, i, 0)),
                  pl.BlockSpec((None, seq, MLA_QK_W), lambda b, i: (b, 0, 0)),
                  pl.BlockSpec((None, MLA_HEADS * MLA_VT_ROWS, seq), lambda b, i: (b, 0, 0))],
        out_specs=pl.BlockSpec((None, MLA_TQ, D_C), lambda b, i: (b, i, 0)),
        out_shape=jax.ShapeDtypeStruct((bsz, seq, D_C), BF16),
        compiler_params=_params(2),
        name="mla_attn",
    )(qm3, km3, vt3)


def _out_kernel(x_ref, oa_ref, ob_ref, oc_ref, w1_ref, w2_ref, w3_ref, g_ref, gate_ref, o_ref):
    tm = x_ref.shape[0]
    for r0 in range(0, tm, tm // 2):
        rows = slice(r0, r0 + tm // 2)
        y = (_dot(oa_ref[rows, :], w1_ref[...]) + _dot(ob_ref[rows, :], w2_ref[...])
             + _dot(oc_ref[rows, :], w3_ref[...]))
        o_ref[rows, :] = x_ref[rows, :] + gate_ref[...] * _rms(y, g_ref[...])


def _out_proj(x, oa, ob, oc, w1, w2, w3, g_post, mod3, l, bsz, seq):
    n = x.shape[0]
    tm = 1024
    tps = seq // tm
    row = lambda i: (i, 0)
    const = lambda i: (0, 0)
    full = lambda a: pl.BlockSpec(a.shape, const)
    return pl.pallas_call(
        _out_kernel,
        grid=(n // tm,),
        in_specs=[pl.BlockSpec((tm, D_MODEL), row),
                  pl.BlockSpec((tm, D_A), row), pl.BlockSpec((tm, D_B), row), pl.BlockSpec((tm, D_C), row),
                  full(w1), full(w2), full(w3), full(g_post),
                  pl.BlockSpec((None, 1, D_MODEL), lambda i: ((l * bsz + i // tps) * N_MOD + 2, 0, 0))],
        out_specs=pl.BlockSpec((tm, D_MODEL), row),
        out_shape=jax.ShapeDtypeStruct((n, D_MODEL), F32),
        compiler_params=_params(1),
        name="out_proj",
    )(x, oa, ob, oc, w1, w2, w3, g_post, mod3)


FFN_TM = 1024
FFN_CHUNK = 256


def _ffn_kernel(x_ref, xp_ref, xn_ref, g_ref, sh_ref, sc_ref, wa_ref, wg_ref, cw_ref, cb_ref, wd_ref,
                gpost_ref, gate_ref, o_ref, u_ref, *, tiles_per_seq):
    i = pl.program_id(0)
    has_prev = (i % tiles_per_seq) != 0
    has_next = (i % tiles_per_seq) != tiles_per_seq - 1
    g = g_ref[...]
    sc = 1.0 + sc_ref[...]
    sh = sh_ref[...]
    x = x_ref[...]
    pre = lambda t: _rms(t, g) * sc + sh
    h_prev = jnp.where(has_prev, pre(xp_ref[...]), 0.0)
    h_next = jnp.where(has_next, pre(xn_ref[...]), 0.0)
    h_mid = pre(x)
    h = h_mid.astype(BF16)
    h_ext = jnp.concatenate([h_prev, h_mid, h_next], axis=0).astype(BF16)
    tm = x.shape[0]
    ext = tm + 2 * SUBLANES
    mid = slice(SUBLANES, SUBLANES + tm)
    for c0 in range(0, D_FF, FFN_CHUNK):
        cols = slice(c0, c0 + FFN_CHUNK)
        a = _dot(h, wa_ref[:, cols])
        ge = _dot(h_ext, wg_ref[:, cols])
        g_prev = pltpu.roll(ge, 1, axis=0)[mid]
        g_next = pltpu.roll(ge, ext - 1, axis=0)[mid]
        gc = cb_ref[:, cols] + g_prev * cw_ref[0:1, cols]
        gc = gc + ge[mid] * cw_ref[1:2, cols]
        gc = gc + g_next * cw_ref[2:3, cols]
        u_ref[:, cols] = (jax.nn.gelu(gc) * a).astype(BF16)
    for r0 in range(0, tm, tm // 2):
        rows = slice(r0, r0 + tm // 2)
        y = _dot(u_ref[rows, :], wd_ref[...])
        o_ref[rows, :] = x[rows] + gate_ref[...] * _rms(y, gpost_ref[...])


def _ffn(x, mod3, l, bsz, seq, g_pre, wa, wg, cw, cb, wd, g_post):
    n = x.shape[0]
    tm = FFN_TM
    tps = seq // tm
    r8 = tm // SUBLANES
    last8 = n // SUBLANES - 1
    const = lambda i: (0, 0)
    full = lambda a: pl.BlockSpec(a.shape, const)
    weight = lambda a: pl.BlockSpec(a.shape, const, pipeline_mode=pl.Buffered(1))
    mrow = lambda k: (lambda i: ((l * bsz + i // tps) * N_MOD + k, 0, 0))
    return pl.pallas_call(
        functools.partial(_ffn_kernel, tiles_per_seq=tps),
        grid=(n // tm,),
        in_specs=[pl.BlockSpec((tm, D_MODEL), lambda i: (i, 0)),
                  pl.BlockSpec((SUBLANES, D_MODEL), lambda i: (jnp.maximum(i * r8 - 1, 0), 0)),
                  pl.BlockSpec((SUBLANES, D_MODEL), lambda i: (jnp.minimum((i + 1) * r8, last8), 0)),
                  full(g_pre),
                  pl.BlockSpec((None, 1, D_MODEL), mrow(3)),
                  pl.BlockSpec((None, 1, D_MODEL), mrow(4)),
                  weight(wa), weight(wg), full(cw), full(cb), weight(wd), full(g_post),
                  pl.BlockSpec((None, 1, D_MODEL), mrow(5))],
        out_specs=pl.BlockSpec((tm, D_MODEL), lambda i: (i, 0)),
        out_shape=jax.ShapeDtypeStruct((n, D_MODEL), F32),
        scratch_shapes=[pltpu.VMEM((tm, D_FF), BF16)],
        compiler_params=_params(1),
        name="conv_ffn",
    )(x, x, x, g_pre, mod3, mod3, wa, wg, cw, cb, wd, g_post, mod3)


def _rope_tables(seq):
    inv_freq = jnp.asarray(ROPE_THETA ** (-np.arange(0, MLA_ROPE, 2, dtype=np.float32) / MLA_ROPE), F32)
    ang = jnp.arange(seq, dtype=F32)[:, None] * inv_freq[None, :]
    cos, sin = jnp.cos(ang), jnp.sin(ang)
    cos2 = jnp.concatenate([cos, cos], axis=1)
    sin2 = jnp.concatenate([sin, sin], axis=1)
    head_c = jnp.concatenate([cos2, jnp.ones((seq, MLA_NOPE), F32), jnp.zeros((seq, LANES - MLA_NOPE - MLA_ROPE), F32)], axis=1)
    head_s = jnp.concatenate([sin2, jnp.zeros((seq, LANES - MLA_ROPE), F32)], axis=1)
    cq = jnp.concatenate([head_c] * MLA_HEADS, axis=1)
    sq = jnp.concatenate([head_s] * MLA_HEADS, axis=1)
    ck = jnp.concatenate([cos2, jnp.zeros((seq, LANES - MLA_ROPE), F32)], axis=1)
    return cq, sq, ck, head_s


def _layer_weights(w_in, w_uq, w_ukv, w_out, w_up, w_down):
    d = w_in.shape[0]
    o_b = 3 * D_A
    o_c = o_b + 3 * D_B
    o_kr = o_c + MLA_Q_LORA + MLA_KV_LORA
    wa = w_in[:, :o_b].astype(BF16)
    wb = w_in[:, o_b:o_c].astype(BF16)
    w_kr = w_in[:, o_kr:o_kr + MLA_ROPE]
    zpad = jnp.zeros((d, LANES - MLA_ROPE), F32)
    wc = jnp.concatenate([w_in[:, o_c:o_kr], w_kr, zpad], axis=1).astype(BF16)

    hq = MLA_NOPE + MLA_ROPE
    q_cols, k_cols, v_cols = [], [], []
    zq = jnp.zeros((MLA_Q_LORA, LANES - hq), F32)
    for h in range(MLA_HEADS):
        nope = w_uq[:, h * hq:h * hq + MLA_NOPE]
        rope = w_uq[:, h * hq + MLA_NOPE:(h + 1) * hq]
        q_cols += [rope, nope, zq]
        hk = MLA_NOPE + MLA_V
        k_cols += [jnp.zeros((MLA_KV_LORA, MLA_ROPE), F32), w_ukv[:, h * hk:h * hk + MLA_NOPE],
                   jnp.zeros((MLA_KV_LORA, LANES - hq), F32)]
        v_cols += [w_ukv[:, h * hk + MLA_NOPE:(h + 1) * hk], jnp.zeros((MLA_KV_LORA, MLA_VT_ROWS - MLA_V), F32)]
    cat = lambda cols: jnp.concatenate(cols, axis=1).astype(BF16)
    w1 = w_out[:D_A].astype(BF16)
    w2 = w_out[D_A:D_A + D_B].astype(BF16)
    w3 = w_out[D_A + D_B:].astype(BF16)
    w_a = w_up[:, :D_FF].astype(BF16)
    w_g = w_up[:, D_FF:].astype(BF16)
    return (wa, wb, wc, cat(q_cols), cat(k_cols), cat(v_cols).T, w1, w2, w3, w_a, w_g,
            w_down.astype(BF16))


def kernel(x, c, w_ada, b_ada, g_pre_mix, g_post_mix, g_pre_ffn, g_post_ffn, w_in, na_rpb, t5_table,
           mla_g_q, mla_g_kv, w_uq, w_ukv, w_out, w_up, conv_w, conv_b, w_down):
    bsz, seq, d = x.shape
    depth = w_in.shape[0]
    n = bsz * seq
    rows = seq // GRID_W
    assert d == D_MODEL and seq % (DIL_BLK * max(dd for _, dd in DIL_PATTERNS)) == 0 and rows >= NA_KH_MAX

    mod3 = _modulation(c, w_ada, b_ada).reshape(depth * bsz * N_MOD, 1, D_MODEL)
    cq, sq, ck, sk = _rope_tables(seq)
    dil_tbls = _dil_bias_tables(t5_table, seq)
    na_tbls = _na_bias_tables(na_rpb, rows)
    row2 = lambda v: v.reshape(1, -1)

    xf = x.reshape(n, d)
    for l in range(depth):
        (wa, wb, wc, wuq, wuk, wuv, w1, w2, w3, w_a, w_g, w_d) = _layer_weights(
            w_in[l], w_uq[l], w_ukv[l], w_out[l], w_up[l], w_down[l])
        za, zb, qm, km, vt = _in_proj(xf, mod3, l, bsz, seq, row2(g_pre_mix[l]), wa, wb, wc,
                                      row2(mla_g_q[l]), row2(mla_g_kv[l]), wuq, wuk, wuv,
                                      cq, sq, ck, sk)
        o_a = _na_attention(za.reshape(bsz, seq, 3 * D_A), na_tbls[l])
        o_b = _dil_attention(zb.reshape(bsz, seq, 3 * D_B), dil_tbls)
        o_c = _mla_attention(qm.reshape(bsz, seq, MLA_QK_W), km.reshape(bsz, seq, MLA_QK_W), vt)
        xf = _out_proj(xf, o_a.reshape(n, D_A), o_b.reshape(n, D_B), o_c.reshape(n, D_C),
                       w1, w2, w3, row2(g_post_mix[l]), mod3, l, bsz, seq)
        xf = _ffn(xf, mod3, l, bsz, seq, row2(g_pre_ffn[l]), w_a, w_g, conv_w[l], row2(conv_b[l]), w_d,
                  row2(g_post_ffn[l]))
    return xf.reshape(bsz, seq, d)
```

```python
import functools
import math

import numpy as np
import jax
import jax.numpy as jnp
from jax import lax
from jax.experimental import pallas as pl
from jax.experimental.pallas import tpu as pltpu

F32 = jnp.float32
BF16 = jnp.bfloat16

D_MODEL = 1024
HEAD_DIM = 64
GRID_W = 64
EPS = 1e-6
NEG_INF = -1e30

NA_HEADS = 4
NA_KH_MAX = 8
NA_KW = 16

DIL_HEADS = 8
DIL_PATTERNS = ((128, 1), (512, 4), (2048, 16))
DIL_BLK = 128
DIL_HALF = 64

MLA_HEADS = 4
MLA_Q_LORA = 384
MLA_KV_LORA = 256
MLA_NOPE = 64
MLA_ROPE = 32
MLA_V = 64
ROPE_THETA = 10000.0

T5_BUCKETS = 32
T5_MAX_DIST = 1024

D_FF = 2816
CONV_W = 3

D_A = NA_HEADS * HEAD_DIM
D_B = DIL_HEADS * HEAD_DIM
D_C = MLA_HEADS * MLA_V
N_MOD = 6

LANES = 128
SUBLANES = 8
VMEM_LIMIT = 56 * 1024 * 1024

LOG2E = math.log2(math.e)
Q_SCALE = HEAD_DIM ** -0.5 * LOG2E
MLA_Q_SCALE = (MLA_NOPE + MLA_ROPE) ** -0.5 * LOG2E

ZC_W = MLA_Q_LORA + MLA_KV_LORA + LANES
MLA_QK_W = MLA_HEADS * LANES


def _params(n_axes):
    return pltpu.CompilerParams(dimension_semantics=("arbitrary",) * n_axes,
                                vmem_limit_bytes=VMEM_LIMIT)


def _dot(a, b):
    return jnp.dot(a, b, preferred_element_type=F32)


def _dot_nt(a, b):
    return lax.dot_general(a, b, (((1,), (1,)), ((), ())), preferred_element_type=F32)


def _rms(x, g):
    return x * lax.rsqrt(jnp.mean(x * x, axis=-1, keepdims=True) + EPS) * g


def _mod_kernel(c_ref, w_ref, b_ref, o_ref):
    ca = jax.nn.silu(c_ref[...]).astype(BF16)
    o_ref[0] = _dot(ca, w_ref[0].astype(BF16)) + b_ref[0]


def _modulation(c, w_ada, b_ada):
    depth, d, n = w_ada.shape
    bsz = c.shape[0]
    tn = 1536
    return pl.pallas_call(
        _mod_kernel,
        grid=(depth, n // tn),
        in_specs=[pl.BlockSpec((bsz, d), lambda l, j: (0, 0)),
                  pl.BlockSpec((1, d, tn), lambda l, j: (l, 0, j)),
                  pl.BlockSpec((1, 1, tn), lambda l, j: (l, 0, j))],
        out_specs=pl.BlockSpec((1, bsz, tn), lambda l, j: (l, 0, j)),
        out_shape=jax.ShapeDtypeStruct((depth, bsz, n), F32),
        compiler_params=_params(2),
        name="adaln_mod",
    )(c, w_ada, b_ada.reshape(depth, 1, n))


def _in_kernel(x_ref, g_ref, sh_ref, sc_ref, wa_ref, wb_ref, wc_ref, gq_ref, gkv_ref,
               wuq_ref, wuk_ref, wuvt_ref, cq_ref, sq_ref, ck_ref, sk_ref,
               za_ref, zb_ref, qm_ref, km_ref, vt_ref):
    x = x_ref[...]
    h = (_rms(x, g_ref[...]) * (1.0 + sc_ref[...]) + sh_ref[...]).astype(BF16)
    zc = _dot(h, wc_ref[...])
    c_q = zc[:, 0:MLA_Q_LORA]
    c_kv = zc[:, MLA_Q_LORA:MLA_Q_LORA + MLA_KV_LORA]
    k_rope = zc[:, MLA_Q_LORA + MLA_KV_LORA:ZC_W]
    cqn = _rms(c_q, gq_ref[...]).astype(BF16)
    ckvn = _rms(c_kv, gkv_ref[...]).astype(BF16)
    q = _dot(cqn, wuq_ref[...])
    k_nope = _dot(ckvn, wuk_ref[...])
    vt = _dot_nt(wuvt_ref[...], ckvn)
    za_ref[:, 0:D_A] = (_dot(h, wa_ref[:, 0:D_A]) * Q_SCALE).astype(BF16)
    za_ref[:, D_A:3 * D_A] = _dot(h, wa_ref[:, D_A:3 * D_A]).astype(BF16)
    zb_ref[:, 0:D_B] = _dot(h, wb_ref[:, 0:D_B]) * Q_SCALE
    for j in range(D_B, 3 * D_B, D_B):
        zb_ref[:, j:j + D_B] = _dot(h, wb_ref[:, j:j + D_B])
    q = q * cq_ref[...] + _rotate_half_lanes(q) * sq_ref[...]
    qm_ref[...] = (q * MLA_Q_SCALE).astype(BF16)
    kr = k_rope * ck_ref[...] + _rotate_half_lanes(k_rope) * sk_ref[...]
    km_ref[...] = (k_nope + jnp.concatenate([kr] * MLA_HEADS, axis=1)).astype(BF16)
    vrow = lax.broadcasted_iota(jnp.int32, (MLA_HEADS * MLA_VT_ROWS, 1), 0)
    ones_rows = functools.reduce(jnp.logical_or, [vrow == h * MLA_VT_ROWS + MLA_V for h in range(MLA_HEADS)])
    vt_ref[...] = (vt + jnp.where(ones_rows, 1.0, 0.0)).astype(BF16)


def _rotate_half_lanes(x):
    half = MLA_ROPE // 2
    width = x.shape[1]
    lane = lax.broadcasted_iota(jnp.int32, (1, width), 1)
    from_above = pltpu.roll(x, width - half, axis=1)
    from_below = pltpu.roll(x, half, axis=1)
    return jnp.where(lane % LANES < half, -from_above, from_below)


def _in_proj(x, mod3, l, bsz, seq, g_pre, wa, wb, wc, gq, gkv, wuq, wuk, wuv, cq, sq, ck, sk):
    n = x.shape[0]
    tm = 1024
    tps = seq // tm
    row = lambda i: (i, 0)
    const = lambda i: (0, 0)
    pos = lambda i: (i % tps, 0)
    mrow = lambda k: (lambda i: ((l * bsz + i // tps) * N_MOD + k, 0, 0))
    full = lambda a: pl.BlockSpec(a.shape, const)
    return pl.pallas_call(
        _in_kernel,
        grid=(n // tm,),
        in_specs=[pl.BlockSpec((tm, D_MODEL), row),
                  full(g_pre),
                  pl.BlockSpec((None, 1, D_MODEL), mrow(0)),
                  pl.BlockSpec((None, 1, D_MODEL), mrow(1)),
                  full(wa), full(wb), full(wc), full(gq), full(gkv),
                  full(wuq), full(wuk), full(wuv),
                  pl.BlockSpec((tm, MLA_QK_W), pos), pl.BlockSpec((tm, MLA_QK_W), pos),
                  pl.BlockSpec((tm, LANES), pos), pl.BlockSpec((tm, LANES), pos)],
        out_specs=[pl.BlockSpec((tm, 3 * D_A), row),
                   pl.BlockSpec((tm, 3 * D_B), row),
                   pl.BlockSpec((tm, MLA_QK_W), row),
                   pl.BlockSpec((tm, MLA_QK_W), row),
                   pl.BlockSpec((None, MLA_HEADS * MLA_VT_ROWS, tm), lambda i: (i // tps, 0, i % tps))],
        out_shape=[jax.ShapeDtypeStruct((n, 3 * D_A), BF16),
                   jax.ShapeDtypeStruct((n, 3 * D_B), F32),
                   jax.ShapeDtypeStruct((n, MLA_QK_W), BF16),
                   jax.ShapeDtypeStruct((n, MLA_QK_W), BF16),
                   jax.ShapeDtypeStruct((bsz, MLA_HEADS * MLA_VT_ROWS, seq), BF16)],
        compiler_params=_params(1),
        name="in_proj",
    )(x, g_pre, mod3, mod3, wa, wb, wc, gq, gkv, wuq, wuk, wuv, cq, sq, ck, sk)


NA_ROWS_PER_STEP = 32
NA_GROUP = 4


def _na_kernel(q_ref, k_ref, v_ref, tbl_ref, o_ref, *, rows):
    i = pl.program_id(1)
    lane = lax.broadcasted_iota(jnp.int32, (1, LANES), 1)
    low = lane < HEAD_DIM
    kh = NA_KH_MAX

    def row_group(j0, carry):
        items = []
        scores = []
        for g in range(NA_GROUP):
            j = j0 * NA_GROUP + g
            r = i * NA_ROWS_PER_STEP + j
            r0 = jnp.clip(r - kh // 2, 0, rows - kh)
            variant = r - r0
            qs = pl.multiple_of(j * GRID_W, GRID_W)
            ks = pl.multiple_of(r0 * GRID_W, GRID_W)
            for pair in range(NA_HEADS // 2):
                cols = slice(pair * LANES, (pair + 1) * LANES)
                q2 = q_ref[pl.ds(qs, GRID_W), cols]
                k2 = k_ref[pl.ds(ks, kh * GRID_W), cols]
                items.append((qs, ks, cols))
                zero = jnp.zeros_like(q2)
                q_st = jnp.concatenate([jnp.where(low, q2, zero), jnp.where(low, zero, q2)], axis=0)
                bias = tbl_ref[variant, 2 * pair:2 * pair + 2].reshape(2 * GRID_W, kh * GRID_W)
                scores.append(_dot_nt(q_st, k2) + bias)
        probs = []
        for s in scores:
            m = jnp.max(s, axis=-1, keepdims=True)
            p = jnp.exp2(s - m)
            probs.append((p.astype(BF16), jnp.sum(p, axis=-1, keepdims=True)))
        for (qs, ks, cols), (p, den) in zip(items, probs):
            o = _dot(p, v_ref[pl.ds(ks, kh * GRID_W), cols]) / den
            o_ref[pl.ds(qs, GRID_W), cols] = jnp.where(low, o[:GRID_W], o[GRID_W:]).astype(BF16)
        return carry

    lax.fori_loop(0, NA_ROWS_PER_STEP // NA_GROUP, row_group, 0)


def _na_attention(za3, tbl):
    bsz, seq, _ = za3.shape
    rows = seq // GRID_W
    tq = NA_ROWS_PER_STEP * GRID_W
    return pl.pallas_call(
        functools.partial(_na_kernel, rows=rows),
        grid=(bsz, seq // tq),
        in_specs=[pl.BlockSpec((None, tq, D_A), lambda b, i: (b, i, 0)),
                  pl.BlockSpec((None, seq, D_A), lambda b, i: (b, 0, 1)),
                  pl.BlockSpec((None, seq, D_A), lambda b, i: (b, 0, 2)),
                  pl.BlockSpec(tbl.shape, lambda b, i: (0, 0, 0, 0))],
        out_specs=pl.BlockSpec((None, tq, D_A), lambda b, i: (b, i, 0)),
        out_shape=jax.ShapeDtypeStruct((bsz, seq, D_A), BF16),
        compiler_params=_params(2),
        name="na_attn",
    )(za3, za3, za3, tbl)


def _na_bias_tables(rpb, rows):
    kh = min(NA_KH_MAX, rows)
    col = np.arange(GRID_W)
    c_start = np.clip(col - NA_KW // 2, 0, GRID_W - NA_KW)
    valid = (col[None, :] >= c_start[:, None]) & (col[None, :] < c_start[:, None] + NA_KW)
    d_col = col[None, :] - col[:, None] + (NA_KW - 1)
    onehot = (d_col[:, :, None] == np.arange(2 * NA_KW - 1)[None, None, :]) & valid[:, :, None]
    by_row = jnp.stack([rpb[:, :, NA_KH_MAX - 1 - v:NA_KH_MAX - 1 - v + kh, :] for v in range(kh)], axis=1)
    t = jnp.einsum("lvhib,cdb->lvhcid", by_row.astype(F32), jnp.asarray(onehot, F32),
                   precision=lax.Precision.HIGHEST)
    t = jnp.where(valid[None, None, None, :, None, :], t * LOG2E, NEG_INF)
    return t.reshape(rpb.shape[0], kh, NA_HEADS, GRID_W, kh * GRID_W)


DIL_KB = DIL_BLK + 2 * DIL_HALF
DIL_GROUP = 8
DIL_GROUP_WHOLE = 2
DIL_MERGE_ROWS = 512
DIL_WINDOW_SHIFT = (DIL_HALF, 0, 2 * DIL_HALF)


def _dil_whole_class(seq, d):
    return seq // d <= 2 * DIL_BLK


def _dil_kernel(q_ref, k_ref, v_ref, *rest, seq):
    npat = len(DIL_PATTERNS)
    tbl_refs = rest[:npat]
    o_ref = rest[npat]
    stats = rest[npat + 1:]
    acc_refs, m_refs, l_refs = stats[:npat], stats[npat:2 * npat], stats[2 * npat:]

    lane = lax.broadcasted_iota(jnp.int32, (1, LANES), 1)
    low = lane < HEAD_DIM

    def attend_group(items, acc_ref, m_ref, l_ref):
        scores = []
        for _, q2, k2, _, bias in items:
            zero = jnp.zeros_like(q2)
            qs = jnp.concatenate([jnp.where(low, q2, zero), jnp.where(low, zero, q2)], axis=0)
            scores.append(_dot_nt(qs, k2) + bias())
        probs = []
        for s in scores:
            m = jnp.max(s, axis=-1, keepdims=True)
            p = jnp.exp2(s - m)
            probs.append((p.astype(BF16), m, jnp.sum(p, axis=-1, keepdims=True)))
        for (rows, q2, _, v2, _), (p, m, den) in zip(items, probs):
            nq = q2.shape[0]
            o = _dot(p, v2)
            acc_ref[rows, :] = jnp.where(low, o[:nq], o[nq:])
            m_ref[rows, :] = jnp.where(low, m[:nq], m[nq:])
            l_ref[rows, :] = jnp.where(low, den[:nq], den[nq:])

    for pat, (_, d) in enumerate(DIL_PATTERNS):
        length = seq // d
        tbl_ref, acc_ref, m_ref, l_ref = tbl_refs[pat], acc_refs[pat], m_refs[pat], l_refs[pat]

        if _dil_whole_class(seq, d):
            group = min(DIL_GROUP_WHOLE, d)

            def whole(g0, carry, d=d, length=length, group=group, tbl_ref=tbl_ref,
                      acc_ref=acc_ref, m_ref=m_ref, l_ref=l_ref):
                items = []
                for g in range(group):
                    r = g0 * group + g
                    rows = pl.ds(r, length, stride=d) if d > 1 else pl.ds(r, length)
                    q2 = q_ref[rows, :].astype(BF16)
                    items.append((rows, q2, k_ref[rows, :].astype(BF16), v_ref[rows, :].astype(BF16),
                                  lambda: tbl_ref[...].reshape(2 * length, length)))
                attend_group(items, acc_ref, m_ref, l_ref)
                return carry

            lax.fori_loop(0, d // group, whole, 0)
        else:
            nblk = length // DIL_BLK
            group = min(DIL_GROUP, nblk)

            def block(t0, carry, d=d, length=length, nblk=nblk, group=group, tbl_ref=tbl_ref,
                      acc_ref=acc_ref, m_ref=m_ref, l_ref=l_ref):
                items = []
                for g in range(group):
                    t = t0 * group + g
                    r = t // nblk
                    n = t % nblk
                    variant = jnp.where(n == 0, 1, 0) + jnp.where(n == nblk - 1, 2, 0)
                    q_start = r + d * DIL_BLK * n
                    k_start = r + d * jnp.clip(DIL_BLK * n - DIL_HALF, 0, length - DIL_KB)
                    if d == 1:
                        rows_q = pl.ds(q_start, DIL_BLK)
                        rows_k = pl.ds(k_start, DIL_KB)
                    else:
                        rows_q = pl.ds(q_start, DIL_BLK, stride=d)
                        rows_k = pl.ds(k_start, DIL_KB, stride=d)
                    q2 = q_ref[rows_q, :].astype(BF16)
                    items.append((rows_q, q2, k_ref[rows_k, :].astype(BF16), v_ref[rows_k, :].astype(BF16),
                                  lambda variant=variant: tbl_ref[variant].reshape(2 * DIL_BLK, DIL_KB)))
                attend_group(items, acc_ref, m_ref, l_ref)
                return carry

            lax.fori_loop(0, d * nblk // group, block, 0)

    def merge(c, carry):
        rows = pl.ds(pl.multiple_of(c * DIL_MERGE_ROWS, DIL_MERGE_ROWS), DIL_MERGE_ROWS)
        ms = [m_ref[rows, :] for m_ref in m_refs]
        m_all = functools.reduce(jnp.maximum, ms)
        num = jnp.zeros((DIL_MERGE_ROWS, LANES), F32)
        den = jnp.zeros((DIL_MERGE_ROWS, LANES), F32)
        for m_i, acc_ref, l_ref in zip(ms, acc_refs, l_refs):
            e = jnp.exp2(m_i - m_all)
            num = num + acc_ref[rows, :] * e
            den = den + l_ref[rows, :] * e
        o_ref[rows, :] = (num / den).astype(BF16)
        return carry

    lax.fori_loop(0, seq // DIL_MERGE_ROWS, merge, 0)


def _dil_attention(zb3, tbls):
    bsz, seq, _ = zb3.shape
    npair = DIL_HEADS // 2
    blk = lambda off: pl.BlockSpec((None, seq, LANES), lambda b, p: (b, 0, off + p))
    tbl_specs = []
    for t in tbls:
        if t.ndim == 3:
            tbl_specs.append(pl.BlockSpec((2,) + t.shape[1:], lambda b, p: (p, 0, 0)))
        else:
            tbl_specs.append(pl.BlockSpec((t.shape[0], 2) + t.shape[2:], lambda b, p: (0, p, 0, 0)))
    stat = pltpu.VMEM((seq, LANES), F32)
    return pl.pallas_call(
        functools.partial(_dil_kernel, seq=seq),
        grid=(bsz, npair),
        in_specs=[blk(0), blk(npair), blk(2 * npair)] + tbl_specs,
        out_specs=pl.BlockSpec((None, seq, LANES), lambda b, p: (b, 0, p)),
        out_shape=jax.ShapeDtypeStruct((bsz, seq, D_B), BF16),
        scratch_shapes=[stat] * (3 * len(DIL_PATTERNS)),
        compiler_params=_params(2),
        name="dil_attn",
    )(zb3, zb3, zb3, *tbls)


def _t5_bucket(rel):
    nb = T5_BUCKETS // 2
    max_exact = nb // 2
    n = np.abs(rel)
    large = max_exact + (np.log(np.maximum(n, 1) / max_exact)
                         / math.log(T5_MAX_DIST / max_exact) * (nb - max_exact)).astype(np.int64)
    large = np.minimum(large, nb - 1)
    return (np.where(rel > 0, nb, 0) + np.where(n < max_exact, n, large)).astype(np.int32)


def _t5_bias(t5_table, rel):
    onehot = _t5_bucket(rel)[:, :, None] == np.arange(T5_BUCKETS)[None, None, :]
    return jnp.einsum("qkb,bh->hqk", jnp.asarray(onehot, F32), t5_table.astype(F32),
                      precision=lax.Precision.HIGHEST)


def _dil_bias_tables(t5_table, seq):
    tables = []
    for _, d in DIL_PATTERNS:
        if _dil_whole_class(seq, d):
            pos = np.arange(seq // d)
            rel = pos[None, :] - pos[:, None]
            valid = np.abs(rel) <= DIL_HALF
            bias = _t5_bias(t5_table, rel * d)
            tables.append(jnp.where(valid[None], bias * LOG2E, NEG_INF))
        else:
            a = np.arange(DIL_BLK)[:, None]
            j = np.arange(DIL_KB)[None, :]
            per_position = []
            for shift in DIL_WINDOW_SHIFT:
                rel = j - shift - a
                per_position.append(jnp.where((np.abs(rel) <= DIL_HALF)[None], _t5_bias(t5_table, rel * d) * LOG2E,
                                              NEG_INF))
            tables.append(jnp.stack(per_position))
    return tables


MLA_TQ = 2048
MLA_CQ = 512
MLA_VT_ROWS = 80


def _mla_kernel(q_ref, k_ref, vt_ref, o_ref):
    chains = [(h, c) for c in range(MLA_TQ // MLA_CQ) for h in range(MLA_HEADS)]

    def scores(h, c):
        cols = slice(h * LANES, (h + 1) * LANES)
        return _dot_nt(k_ref[:, cols], q_ref[c * MLA_CQ:(c + 1) * MLA_CQ, cols])

    outs = []
    s_next = scores(*chains[0])
    for idx, (h, c) in enumerate(chains):
        s = s_next
        if idx + 1 < len(chains):
            s_next = scores(*chains[idx + 1])
        sb = s.astype(BF16)
        p = jnp.exp2(sb - jnp.max(sb, axis=0, keepdims=True))
        ov = _dot(vt_ref[h * MLA_VT_ROWS:(h + 1) * MLA_VT_ROWS, :], p)
        outs.append(ov[0:MLA_V] / ov[MLA_V:MLA_V + 1])
        if h == MLA_HEADS - 1:
            o_ref[c * MLA_CQ:(c + 1) * MLA_CQ, :] = jnp.concatenate(outs, axis=0).T.astype(BF16)
            outs = []


def _mla_attention(qm3, km3, vt3):
    bsz, seq, _ = qm3.shape
    return pl.pallas_call(
        _mla_kernel,
        grid=(bsz, seq // MLA_TQ),
        in_specs=[pl.BlockSpec((None, MLA_TQ, MLA_QK_W), lambda b, i: (b, i, 0)),
                  pl.BlockSpec((None, seq, MLA_QK_W), lambda b, i: (b, 0, 0)),
                  pl.BlockSpec((None, MLA_HEADS * MLA_VT_ROWS, seq), lambda b, i: (b, 0, 0))],
        out_specs=pl.BlockSpec((None, MLA_TQ, D_C), lambda b, i: (b, i, 0)),
        out_shape=jax.ShapeDtypeStruct((bsz, seq, D_C), BF16),
        compiler_params=_params(2),
        name="mla_attn",
    )(qm3, km3, vt3)


def _out_kernel(x_ref, oa_ref, ob_ref, oc_ref, w1_ref, w2_ref, w3_ref, g_ref, gate_ref, o_ref):
    tm = x_ref.shape[0]
    for r0 in range(0, tm, tm // 2):
        rows = slice(r0, r0 + tm // 2)
        y = (_dot(oa_ref[rows, :], w1_ref[...]) + _dot(ob_ref[rows, :], w2_ref[...])
             + _dot(oc_ref[rows, :], w3_ref[...]))
        o_ref[rows, :] = x_ref[rows, :] + gate_ref[...] * _rms(y, g_ref[...])


def _out_proj(x, oa, ob, oc, w1, w2, w3, g_post, mod3, l, bsz, seq):
    n = x.shape[0]
    tm = 1024
    tps = seq // tm
    row = lambda i: (i, 0)
    const = lambda i: (0, 0)
    full = lambda a: pl.BlockSpec(a.shape, const)
    return pl.pallas_call(
        _out_kernel,
        grid=(n // tm,),
        in_specs=[pl.BlockSpec((tm, D_MODEL), row),
                  pl.BlockSpec((tm, D_A), row), pl.BlockSpec((tm, D_B), row), pl.BlockSpec((tm, D_C), row),
                  full(w1), full(w2), full(w3), full(g_post),
                  pl.BlockSpec((None, 1, D_MODEL), lambda i: ((l * bsz + i // tps) * N_MOD + 2, 0, 0))],
        out_specs=pl.BlockSpec((tm, D_MODEL), row),
        out_shape=jax.ShapeDtypeStruct((n, D_MODEL), F32),
        compiler_params=_params(1),
        name="out_proj",
    )(x, oa, ob, oc, w1, w2, w3, g_post, mod3)


FFN_TM = 1024
FFN_CHUNK = 256


def _ffn_kernel(x_ref, xp_ref, xn_ref, g_ref, sh_ref, sc_ref, wa_ref, wg_ref, cw_ref, cb_ref, wd_ref,
                gpost_ref, gate_ref, o_ref, u_ref, *, tiles_per_seq):
    i = pl.program_id(0)
    has_prev = (i % tiles_per_seq) != 0
    has_next = (i % tiles_per_seq) != tiles_per_seq - 1
    g = g_ref[...]
    sc = 1.0 + sc_ref[...]
    sh = sh_ref[...]
    x = x_ref[...]
    pre = lambda t: _rms(t, g) * sc + sh
    h_prev = jnp.where(has_prev, pre(xp_ref[...]), 0.0)
    h_next = jnp.where(has_next, pre(xn_ref[...]), 0.0)
    h_mid = pre(x)
    h = h_mid.astype(BF16)
    h_ext = jnp.concatenate([h_prev, h_mid, h_next], axis=0).astype(BF16)
    tm = x.shape[0]
    ext = tm + 2 * SUBLANES
    mid = slice(SUBLANES, SUBLANES + tm)
    for c0 in range(0, D_FF, FFN_CHUNK):
        cols = slice(c0, c0 + FFN_CHUNK)
        a = _dot(h, wa_ref[:, cols])
        ge = _dot(h_ext, wg_ref[:, cols])
        g_prev = pltpu.roll(ge, 1, axis=0)[mid]
        g_next = pltpu.roll(ge, ext - 1, axis=0)[mid]
        gc = cb_ref[:, cols] + g_prev * cw_ref[0:1, cols]
        gc = gc + ge[mid] * cw_ref[1:2, cols]
        gc = gc + g_next * cw_ref[2:3, cols]
        u_ref[:, cols] = (jax.nn.gelu(gc) * a).astype(BF16)
    for r0 in range(0, tm, tm // 2):
        rows = slice(r0, r0 + tm // 2)
        y = _dot(u_ref[rows, :], wd_ref[...])
        o_ref[rows, :] = x[rows] + gate_ref[...] * _rms(y, gpost_ref[...])


def _ffn(x, mod3, l, bsz, seq, g_pre, wa, wg, cw, cb, wd, g_post):
    n = x.shape[0]
    tm = FFN_TM
    tps = seq // tm
    r8 = tm // SUBLANES
    last8 = n // SUBLANES - 1
    const = lambda i: (0, 0)
    full = lambda a: pl.BlockSpec(a.shape, const)
    weight = lambda a: pl.BlockSpec(a.shape, const, pipeline_mode=pl.Buffered(1))
    mrow = lambda k: (lambda i: ((l * bsz + i // tps) * N_MOD + k, 0, 0))
    return pl.pallas_call(
        functools.partial(_ffn_kernel, tiles_per_seq=tps),
        grid=(n // tm,),
        in_specs=[pl.BlockSpec((tm, D_MODEL), lambda i: (i, 0)),
                  pl.BlockSpec((SUBLANES, D_MODEL), lambda i: (jnp.maximum(i * r8 - 1, 0), 0)),
                  pl.BlockSpec((SUBLANES, D_MODEL), lambda i: (jnp.minimum((i + 1) * r8, last8), 0)),
                  full(g_pre),
                  pl.BlockSpec((None, 1, D_MODEL), mrow(3)),
                  pl.BlockSpec((None, 1, D_MODEL), mrow(4)),
                  weight(wa), weight(wg), full(cw), full(cb), weight(wd), full(g_post),
                  pl.BlockSpec((None, 1, D_MODEL), mrow(5))],
        out_specs=pl.BlockSpec((tm, D_MODEL), lambda i: (i, 0)),
        out_shape=jax.ShapeDtypeStruct((n, D_MODEL), F32),
        scratch_shapes=[pltpu.VMEM((tm, D_FF), BF16)],
        compiler_params=_params(1),
        name="conv_ffn",
    )(x, x, x, g_pre, mod3, mod3, wa, wg, cw, cb, wd, g_post, mod3)


def _rope_tables(seq):
    inv_freq = jnp.asarray(ROPE_THETA ** (-np.arange(0, MLA_ROPE, 2, dtype=np.float32) / MLA_ROPE), F32)
    ang = jnp.arange(seq, dtype=F32)[:, None] * inv_freq[None, :]
    cos, sin = jnp.cos(ang), jnp.sin(ang)
    cos2 = jnp.concatenate([cos, cos], axis=1)
    sin2 = jnp.concatenate([sin, sin], axis=1)
    head_c = jnp.concatenate([cos2, jnp.ones((seq, MLA_NOPE), F32), jnp.zeros((seq, LANES - MLA_NOPE - MLA_ROPE), F32)], axis=1)
    head_s = jnp.concatenate([sin2, jnp.zeros((seq, LANES - MLA_ROPE), F32)], axis=1)
    cq = jnp.concatenate([head_c] * MLA_HEADS, axis=1)
    sq = jnp.concatenate([head_s] * MLA_HEADS, axis=1)
    ck = jnp.concatenate([cos2, jnp.zeros((seq, LANES - MLA_ROPE), F32)], axis=1)
    return cq, sq, ck, head_s


def _layer_weights(w_in, w_uq, w_ukv, w_out, w_up, w_down):
    d = w_in.shape[0]
    o_b = 3 * D_A
    o_c = o_b + 3 * D_B
    o_kr = o_c + MLA_Q_LORA + MLA_KV_LORA
    wa = w_in[:, :o_b].astype(BF16)
    wb = w_in[:, o_b:o_c].astype(BF16)
    w_kr = w_in[:, o_kr:o_kr + MLA_ROPE]
    zpad = jnp.zeros((d, LANES - MLA_ROPE), F32)
    wc = jnp.concatenate([w_in[:, o_c:o_kr], w_kr, zpad], axis=1).astype(BF16)

    hq = MLA_NOPE + MLA_ROPE
    q_cols, k_cols, v_cols = [], [], []
    zq = jnp.zeros((MLA_Q_LORA, LANES - hq), F32)
    for h in range(MLA_HEADS):
        nope = w_uq[:, h * hq:h * hq + MLA_NOPE]
        rope = w_uq[:, h * hq + MLA_NOPE:(h + 1) * hq]
        q_cols += [rope, nope, zq]
        hk = MLA_NOPE + MLA_V
        k_cols += [jnp.zeros((MLA_KV_LORA, MLA_ROPE), F32), w_ukv[:, h * hk:h * hk + MLA_NOPE],
                   jnp.zeros((MLA_KV_LORA, LANES - hq), F32)]
        v_cols += [w_ukv[:, h * hk + MLA_NOPE:(h + 1) * hk], jnp.zeros((MLA_KV_LORA, MLA_VT_ROWS - MLA_V), F32)]
    cat = lambda cols: jnp.concatenate(cols, axis=1).astype(BF16)
    w1 = w_out[:D_A].astype(BF16)
    w2 = w_out[D_A:D_A + D_B].astype(BF16)
    w3 = w_out[D_A + D_B:].astype(BF16)
    w_a = w_up[:, :D_FF].astype(BF16)
    w_g = w_up[:, D_FF:].astype(BF16)
    return (wa, wb, wc, cat(q_cols), cat(k_cols), cat(v_cols).T, w1, w2, w3, w_a, w_g,
            w_down.astype(BF16))


def kernel(x, c, w_ada, b_ada, g_pre_mix, g_post_mix, g_pre_ffn, g_post_ffn, w_in, na_rpb, t5_table,
           mla_g_q, mla_g_kv, w_uq, w_ukv, w_out, w_up, conv_w, conv_b, w_down):
    bsz, seq, d = x.shape
    depth = w_in.shape[0]
    n = bsz * seq
    rows = seq // GRID_W
    assert d == D_MODEL and seq % (DIL_BLK * max(dd for _, dd in DIL_PATTERNS)) == 0 and rows >= NA_KH_MAX

    mod3 = _modulation(c, w_ada, b_ada).reshape(depth * bsz * N_MOD, 1, D_MODEL)
    cq, sq, ck, sk = _rope_tables(seq)
    dil_tbls = _dil_bias_tables(t5_table, seq)
    na_tbls = _na_bias_tables(na_rpb, rows)
    row2 = lambda v: v.reshape(1, -1)

    xf = x.reshape(n, d)
    for l in range(depth):
        (wa, wb, wc, wuq, wuk, wuv, w1, w2, w3, w_a, w_g, w_d) = _layer_weights(
            w_in[l], w_uq[l], w_ukv[l], w_out[l], w_up[l], w_down[l])
        za, zb, qm, km, vt = _in_proj(xf, mod3, l, bsz, seq, row2(g_pre_mix[l]), wa, wb, wc,
                                      row2(mla_g_q[l]), row2(mla_g_kv[l]), wuq, wuk, wuv,
                                      cq, sq, ck, sk)
        o_a = _na_attention(za.reshape(bsz, seq, 3 * D_A), na_tbls[l])
        o_b = _dil_attention(zb.reshape(bsz, seq, 3 * D_B), dil_tbls)
        o_c = _mla_attention(qm.reshape(bsz, seq, MLA_QK_W), km.reshape(bsz, seq, MLA_QK_W), vt)
        xf = _out_proj(xf, o_a.reshape(n, D_A), o_b.reshape(n, D_B), o_c.reshape(n, D_C),
                       w1, w2, w3, row2(g_post_mix[l]), mod3, l, bsz, seq)
        xf = _ffn(xf, mod3, l, bsz, seq, row2(g_pre_ffn[l]), w_a, w_g, conv_w[l], row2(conv_b[l]), w_d,
                  row2(g_post_ffn[l]))
    return xf.reshape(bsz, seq, d)
```

```python
import functools
import math

import numpy as np
import jax
import jax.numpy as jnp
from jax import lax
from jax.experimental import pallas as pl
from jax.experimental.pallas import tpu as pltpu

F32 = jnp.float32
BF16 = jnp.bfloat16

D_MODEL = 1024
HEAD_DIM = 64
GRID_W = 64
EPS = 1e-6
NEG_INF = -1e30

NA_HEADS = 4
NA_KH_MAX = 8
NA_KW = 16

DIL_HEADS = 8
DIL_PATTERNS = ((128, 1), (512, 4), (2048, 16))
DIL_BLK = 128
DIL_HALF = 64

MLA_HEADS = 4
MLA_Q_LORA = 384
MLA_KV_LORA = 256
MLA_NOPE = 64
MLA_ROPE = 32
MLA_V = 64
ROPE_THETA = 10000.0

T5_BUCKETS = 32
T5_MAX_DIST = 1024

D_FF = 2816
CONV_W = 3

D_A = NA_HEADS * HEAD_DIM
D_B = DIL_HEADS * HEAD_DIM
D_C = MLA_HEADS * MLA_V
N_MOD = 6

LANES = 128
SUBLANES = 8
VMEM_LIMIT = 56 * 1024 * 1024

LOG2E = math.log2(math.e)
Q_SCALE = HEAD_DIM ** -0.5 * LOG2E
MLA_Q_SCALE = (MLA_NOPE + MLA_ROPE) ** -0.5 * LOG2E

ZC_W = MLA_Q_LORA + MLA_KV_LORA + LANES
MLA_QK_W = MLA_HEADS * LANES


def _params(n_axes):
    return pltpu.CompilerParams(dimension_semantics=("arbitrary",) * n_axes,
                                vmem_limit_bytes=VMEM_LIMIT)


def _dot(a, b):
    return jnp.dot(a, b, preferred_element_type=F32)


def _dot_nt(a, b):
    return lax.dot_general(a, b, (((1,), (1,)), ((), ())), preferred_element_type=F32)


def _rms(x, g):
    return x * lax.rsqrt(jnp.mean(x * x, axis=-1, keepdims=True) + EPS) * g


def _mod_kernel(c_ref, w_ref, b_ref, o_ref):
    ca = jax.nn.silu(c_ref[...]).astype(BF16)
    o_ref[0] = _dot(ca, w_ref[0].astype(BF16)) + b_ref[0]


def _modulation(c, w_ada, b_ada):
    depth, d, n = w_ada.shape
    bsz = c.shape[0]
    tn = 1536
    return pl.pallas_call(
        _mod_kernel,
        grid=(depth, n // tn),
        in_specs=[pl.BlockSpec((bsz, d), lambda l, j: (0, 0)),
                  pl.BlockSpec((1, d, tn), lambda l, j: (l, 0, j)),
                  pl.BlockSpec((1, 1, tn), lambda l, j: (l, 0, j))],
        out_specs=pl.BlockSpec((1, bsz, tn), lambda l, j: (l, 0, j)),
        out_shape=jax.ShapeDtypeStruct((depth, bsz, n), F32),
        compiler_params=_params(2),
        name="adaln_mod",
    )(c, w_ada, b_ada.reshape(depth, 1, n))


def _in_kernel(x_ref, g_ref, sh_ref, sc_ref, wa_ref, wb_ref, wc_ref, gq_ref, gkv_ref,
               wuq_ref, wuk_ref, wuvt_ref, cq_ref, sq_ref, ck_ref, sk_ref,
               za_ref, zb_ref, qm_ref, km_ref, vt_ref):
    x = x_ref[...]
    h = (_rms(x, g_ref[...]) * (1.0 + sc_ref[...]) + sh_ref[...]).astype(BF16)
    zc = _dot(h, wc_ref[...])
    c_q = zc[:, 0:MLA_Q_LORA]
    c_kv = zc[:, MLA_Q_LORA:MLA_Q_LORA + MLA_KV_LORA]
    k_rope = zc[:, MLA_Q_LORA + MLA_KV_LORA:ZC_W]
    cqn = _rms(c_q, gq_ref[...]).astype(BF16)
    ckvn = _rms(c_kv, gkv_ref[...]).astype(BF16)
    q = _dot(cqn, wuq_ref[...])
    k_nope = _dot(ckvn, wuk_ref[...])
    vt = _dot_nt(wuvt_ref[...], ckvn)
    za_ref[:, 0:D_A] = (_dot(h, wa_ref[:, 0:D_A]) * Q_SCALE).astype(BF16)
    za_ref[:, D_A:3 * D_A] = _dot(h, wa_ref[:, D_A:3 * D_A]).astype(BF16)
    zb_ref[:, 0:D_B] = _dot(h, wb_ref[:, 0:D_B]) * Q_SCALE
    for j in range(D_B, 3 * D_B, D_B):
        zb_ref[:, j:j + D_B] = _dot(h, wb_ref[:, j:j + D_B])
    q = q * cq_ref[...] + _rotate_half_lanes(q) * sq_ref[...]
    qm_ref[...] = (q * MLA_Q_SCALE).astype(BF16)
    kr = k_rope * ck_ref[...] + _rotate_half_lanes(k_rope) * sk_ref[...]
    km_ref[...] = (k_nope + jnp.concatenate([kr] * MLA_HEADS, axis=1)).astype(BF16)
    vrow = lax.broadcasted_iota(jnp.int32, (MLA_HEADS * MLA_VT_ROWS, 1), 0)
    ones_rows = functools.reduce(jnp.logical_or, [vrow == h * MLA_VT_ROWS + MLA_V for h in range(MLA_HEADS)])
    vt_ref[...] = (vt + jnp.where(ones_rows, 1.0, 0.0)).astype(BF16)


def _rotate_half_lanes(x):
    half = MLA_ROPE // 2
    width = x.shape[1]
    lane = lax.broadcasted_iota(jnp.int32, (1, width), 1)
    from_above = pltpu.roll(x, width - half, axis=1)
    from_below = pltpu.roll(x, half, axis=1)
    return jnp.where(lane % LANES < half, -from_above, from_below)


def _in_proj(x, mod3, l, bsz, seq, g_pre, wa, wb, wc, gq, gkv, wuq, wuk, wuv, cq, sq, ck, sk):
    n = x.shape[0]
    tm = 1024
    tps = seq // tm
    row = lambda i: (i, 0)
    const = lambda i: (0, 0)
    pos = lambda i: (i % tps, 0)
    mrow = lambda k: (lambda i: ((l * bsz + i // tps) * N_MOD + k, 0, 0))
    full = lambda a: pl.BlockSpec(a.shape, const)
    return pl.pallas_call(
        _in_kernel,
        grid=(n // tm,),
        in_specs=[pl.BlockSpec((tm, D_MODEL), row),
                  full(g_pre),
                  pl.BlockSpec((None, 1, D_MODEL), mrow(0)),
                  pl.BlockSpec((None, 1, D_MODEL), mrow(1)),
                  full(wa), full(wb), full(wc), full(gq), full(gkv),
                  full(wuq), full(wuk), full(wuv),
                  pl.BlockSpec((tm, MLA_QK_W), pos), pl.BlockSpec((tm, MLA_QK_W), pos),
                  pl.BlockSpec((tm, LANES), pos), pl.BlockSpec((tm, LANES), pos)],
        out_specs=[pl.BlockSpec((tm, 3 * D_A), row),
                   pl.BlockSpec((tm, 3 * D_B), row),
                   pl.BlockSpec((tm, MLA_QK_W), row),
                   pl.BlockSpec((tm, MLA_QK_W), row),
                   pl.BlockSpec((None, MLA_HEADS * MLA_VT_ROWS, tm), lambda i: (i // tps, 0, i % tps))],
        out_shape=[jax.ShapeDtypeStruct((n, 3 * D_A), BF16),
                   jax.ShapeDtypeStruct((n, 3 * D_B), F32),
                   jax.ShapeDtypeStruct((n, MLA_QK_W), BF16),
                   jax.ShapeDtypeStruct((n, MLA_QK_W), BF16),
                   jax.ShapeDtypeStruct((bsz, MLA_HEADS * MLA_VT_ROWS, seq), BF16)],
        compiler_params=_params(1),
        name="in_proj",
    )(x, g_pre, mod3, mod3, wa, wb, wc, gq, gkv, wuq, wuk, wuv, cq, sq, ck, sk)


NA_ROWS_PER_STEP = 32
NA_GROUP = 4


def _na_kernel(q_ref, k_ref, v_ref, tbl_ref, o_ref, *, rows):
    i = pl.program_id(1)
    lane = lax.broadcasted_iota(jnp.int32, (1, LANES), 1)
    low = lane < HEAD_DIM
    kh = NA_KH_MAX

    def row_group(j0, carry):
        items = []
        scores = []
        for g in range(NA_GROUP):
            j = j0 * NA_GROUP + g
            r = i * NA_ROWS_PER_STEP + j
            r0 = jnp.clip(r - kh // 2, 0, rows - kh)
            variant = r - r0
            qs = pl.multiple_of(j * GRID_W, GRID_W)
            ks = pl.multiple_of(r0 * GRID_W, GRID_W)
            for pair in range(NA_HEADS // 2):
                cols = slice(pair * LANES, (pair + 1) * LANES)
                q2 = q_ref[pl.ds(qs, GRID_W), cols]
                k2 = k_ref[pl.ds(ks, kh * GRID_W), cols]
                items.append((qs, ks, cols))
                zero = jnp.zeros_like(q2)
                q_st = jnp.concatenate([jnp.where(low, q2, zero), jnp.where(low, zero, q2)], axis=0)
                bias = tbl_ref[variant, 2 * pair:2 * pair + 2].reshape(2 * GRID_W, kh * GRID_W)
                scores.append(_dot_nt(q_st, k2) + bias)
        probs = []
        for s in scores:
            sb = s.astype(BF16)
            p = jnp.exp2(sb - jnp.max(sb, axis=-1, keepdims=True))
            probs.append((p, jnp.sum(p.astype(F32), axis=-1, keepdims=True)))
        for (qs, ks, cols), (p, den) in zip(items, probs):
            o = _dot(p, v_ref[pl.ds(ks, kh * GRID_W), cols]) / den
            o_ref[pl.ds(qs, GRID_W), cols] = jnp.where(low, o[:GRID_W], o[GRID_W:]).astype(BF16)
        return carry

    lax.fori_loop(0, NA_ROWS_PER_STEP // NA_GROUP, row_group, 0)


def _na_attention(za3, tbl):
    bsz, seq, _ = za3.shape
    rows = seq // GRID_W
    tq = NA_ROWS_PER_STEP * GRID_W
    return pl.pallas_call(
        functools.partial(_na_kernel, rows=rows),
        grid=(bsz, seq // tq),
        in_specs=[pl.BlockSpec((None, tq, D_A), lambda b, i: (b, i, 0)),
                  pl.BlockSpec((None, seq, D_A), lambda b, i: (b, 0, 1)),
                  pl.BlockSpec((None, seq, D_A), lambda b, i: (b, 0, 2)),
                  pl.BlockSpec(tbl.shape, lambda b, i: (0, 0, 0, 0))],
        out_specs=pl.BlockSpec((None, tq, D_A), lambda b, i: (b, i, 0)),
        out_shape=jax.ShapeDtypeStruct((bsz, seq, D_A), BF16),
        compiler_params=_params(2),
        name="na_attn",
    )(za3, za3, za3, tbl)


def _na_bias_tables(rpb, rows):
    kh = min(NA_KH_MAX, rows)
    col = np.arange(GRID_W)
    c_start = np.clip(col - NA_KW // 2, 0, GRID_W - NA_KW)
    valid = (col[None, :] >= c_start[:, None]) & (col[None, :] < c_start[:, None] + NA_KW)
    d_col = col[None, :] - col[:, None] + (NA_KW - 1)
    onehot = (d_col[:, :, None] == np.arange(2 * NA_KW - 1)[None, None, :]) & valid[:, :, None]
    by_row = jnp.stack([rpb[:, :, NA_KH_MAX - 1 - v:NA_KH_MAX - 1 - v + kh, :] for v in range(kh)], axis=1)
    t = jnp.einsum("lvhib,cdb->lvhcid", by_row.astype(F32), jnp.asarray(onehot, F32),
                   precision=lax.Precision.HIGHEST)
    t = jnp.where(valid[None, None, None, :, None, :], t * LOG2E, NEG_INF)
    return t.reshape(rpb.shape[0], kh, NA_HEADS, GRID_W, kh * GRID_W)


DIL_KB = DIL_BLK + 2 * DIL_HALF
DIL_GROUP = 8
DIL_GROUP_WHOLE = 2
DIL_MERGE_ROWS = 512
DIL_WINDOW_SHIFT = (DIL_HALF, 0, 2 * DIL_HALF)


def _dil_whole_class(seq, d):
    return seq // d <= 2 * DIL_BLK


def _dil_kernel(q_ref, k_ref, v_ref, *rest, seq):
    npat = len(DIL_PATTERNS)
    tbl_refs = rest[:npat]
    o_ref = rest[npat]
    stats = rest[npat + 1:]
    acc_refs, m_refs, l_refs = stats[:npat], stats[npat:2 * npat], stats[2 * npat:]

    lane = lax.broadcasted_iota(jnp.int32, (1, LANES), 1)
    low = lane < HEAD_DIM

    def attend_group(items, acc_ref, m_ref, l_ref):
        scores = []
        for _, q2, k2, _, bias in items:
            zero = jnp.zeros_like(q2)
            qs = jnp.concatenate([jnp.where(low, q2, zero), jnp.where(low, zero, q2)], axis=0)
            scores.append(_dot_nt(qs, k2) + bias())
        probs = []
        for s in scores:
            sb = s.astype(BF16)
            m = jnp.max(sb, axis=-1, keepdims=True)
            p = jnp.exp2(sb - m)
            probs.append((p, m.astype(F32), jnp.sum(p.astype(F32), axis=-1, keepdims=True)))
        for (rows, q2, _, v2, _), (p, m, den) in zip(items, probs):
            nq = q2.shape[0]
            o = _dot(p, v2)
            acc_ref[rows, :] = jnp.where(low, o[:nq], o[nq:])
            m_ref[rows, :] = jnp.where(low, m[:nq], m[nq:])
            l_ref[rows, :] = jnp.where(low, den[:nq], den[nq:])

    for pat, (_, d) in enumerate(DIL_PATTERNS):
        length = seq // d
        tbl_ref, acc_ref, m_ref, l_ref = tbl_refs[pat], acc_refs[pat], m_refs[pat], l_refs[pat]

        if _dil_whole_class(seq, d):
            group = min(DIL_GROUP_WHOLE, d)

            def whole(g0, carry, d=d, length=length, group=group, tbl_ref=tbl_ref,
                      acc_ref=acc_ref, m_ref=m_ref, l_ref=l_ref):
                items = []
                for g in range(group):
                    r = g0 * group + g
                    rows = pl.ds(r, length, stride=d) if d > 1 else pl.ds(r, length)
                    q2 = q_ref[rows, :].astype(BF16)
                    items.append((rows, q2, k_ref[rows, :].astype(BF16), v_ref[rows, :].astype(BF16),
                                  lambda: tbl_ref[...].reshape(2 * length, length)))
                attend_group(items, acc_ref, m_ref, l_ref)
                return carry

            lax.fori_loop(0, d // group, whole, 0)
        else:
            nblk = length // DIL_BLK
            group = min(DIL_GROUP, nblk)

            def block(t0, carry, d=d, length=length, nblk=nblk, group=group, tbl_ref=tbl_ref,
                      acc_ref=acc_ref, m_ref=m_ref, l_ref=l_ref):
                items = []
                for g in range(group):
                    t = t0 * group + g
                    r = t // nblk
                    n = t % nblk
                    variant = jnp.where(n == 0, 1, 0) + jnp.where(n == nblk - 1, 2, 0)
                    q_start = r + d * DIL_BLK * n
                    k_start = r + d * jnp.clip(DIL_BLK * n - DIL_HALF, 0, length - DIL_KB)
                    if d == 1:
                        rows_q = pl.ds(q_start, DIL_BLK)
                        rows_k = pl.ds(k_start, DIL_KB)
                    else:
                        rows_q = pl.ds(q_start, DIL_BLK, stride=d)
                        rows_k = pl.ds(k_start, DIL_KB, stride=d)
                    q2 = q_ref[rows_q, :].astype(BF16)
                    items.append((rows_q, q2, k_ref[rows_k, :].astype(BF16), v_ref[rows_k, :].astype(BF16),
                                  lambda variant=variant: tbl_ref[variant].reshape(2 * DIL_BLK, DIL_KB)))
                attend_group(items, acc_ref, m_ref, l_ref)
                return carry

            lax.fori_loop(0, d * nblk // group, block, 0)

    def merge(c, carry):
        rows = pl.ds(pl.multiple_of(c * DIL_MERGE_ROWS, DIL_MERGE_ROWS), DIL_MERGE_ROWS)
        ms = [m_ref[rows, :] for m_ref in m_refs]
        m_all = functools.reduce(jnp.maximum, ms)
        num = jnp.zeros((DIL_MERGE_ROWS, LANES), F32)
        den = jnp.zeros((DIL_MERGE_ROWS, LANES), F32)
        for m_i, acc_ref, l_ref in zip(ms, acc_refs, l_refs):
            e = jnp.exp2(m_i - m_all)
            num = num + acc_ref[rows, :] * e
            den = den + l_ref[rows, :] * e
        o_ref[rows, :] = (num / den).astype(BF16)
        return carry

    lax.fori_loop(0, seq // DIL_MERGE_ROWS, merge, 0)


def _dil_attention(zb3, tbls):
    bsz, seq, _ = zb3.shape
    npair = DIL_HEADS // 2
    blk = lambda off: pl.BlockSpec((None, seq, LANES), lambda b, p: (b, 0, off + p))
    tbl_specs = []
    for t in tbls:
        if t.ndim == 3:
            tbl_specs.append(pl.BlockSpec((2,) + t.shape[1:], lambda b, p: (p, 0, 0)))
        else:
            tbl_specs.append(pl.BlockSpec((t.shape[0], 2) + t.shape[2:], lambda b, p: (0, p, 0, 0)))
    stat = pltpu.VMEM((seq, LANES), F32)
    return pl.pallas_call(
        functools.partial(_dil_kernel, seq=seq),
        grid=(bsz, npair),
        in_specs=[blk(0), blk(npair), blk(2 * npair)] + tbl_specs,
        out_specs=pl.BlockSpec((None, seq, LANES), lambda b, p: (b, 0, p)),
        out_shape=jax.ShapeDtypeStruct((bsz, seq, D_B), BF16),
        scratch_shapes=[stat] * (3 * len(DIL_PATTERNS)),
        compiler_params=_params(2),
        name="dil_attn",
    )(zb3, zb3, zb3, *tbls)


def _t5_bucket(rel):
    nb = T5_BUCKETS // 2
    max_exact = nb // 2
    n = np.abs(rel)
    large = max_exact + (np.log(np.maximum(n, 1) / max_exact)
                         / math.log(T5_MAX_DIST / max_exact) * (nb - max_exact)).astype(np.int64)
    large = np.minimum(large, nb - 1)
    return (np.where(rel > 0, nb, 0) + np.where(n < max_exact, n, large)).astype(np.int32)


def _t5_bias(t5_table, rel):
    onehot = _t5_bucket(rel)[:, :, None] == np.arange(T5_BUCKETS)[None, None, :]
    return jnp.einsum("qkb,bh->hqk", jnp.asarray(onehot, F32), t5_table.astype(F32),
                      precision=lax.Precision.HIGHEST)


def _dil_bias_tables(t5_table, seq):
    tables = []
    for _, d in DIL_PATTERNS:
        if _dil_whole_class(seq, d):
            pos = np.arange(seq // d)
            rel = pos[None, :] - pos[:, None]
            valid = np.abs(rel) <= DIL_HALF
            bias = _t5_bias(t5_table, rel * d)
            tables.append(jnp.where(valid[None], bias * LOG2E, NEG_INF))
        else:
            a = np.arange(DIL_BLK)[:, None]
            j = np.arange(DIL_KB)[None, :]
            per_position = []
            for shift in DIL_WINDOW_SHIFT:
                rel = j - shift - a
                per_position.append(jnp.where((np.abs(rel) <= DIL_HALF)[None], _t5_bias(t5_table, rel * d) * LOG2E,
                                              NEG_INF))
            tables.append(jnp.stack(per_position))
    return tables


MLA_TQ = 2048
MLA_CQ = 512
MLA_VT_ROWS = 80


def _mla_kernel(q_ref, k_ref, vt_ref, o_ref):
    chains = [(h, c) for c in range(MLA_TQ // MLA_CQ) for h in range(MLA_HEADS)]

    def scores(h, c):
        cols = slice(h * LANES, (h + 1) * LANES)
        return _dot_nt(k_ref[:, cols], q_ref[c * MLA_CQ:(c + 1) * MLA_CQ, cols])

    outs = []
    s_next = scores(*chains[0])
    for idx, (h, c) in enumerate(chains):
        s = s_next
        if idx + 1 < len(chains):
            s_next = scores(*chains[idx + 1])
        sb = s.astype(BF16)
        p = jnp.exp2(sb - jnp.max(sb, axis=0, keepdims=True))
        ov = _dot(vt_ref[h * MLA_VT_ROWS:(h + 1) * MLA_VT_ROWS, :], p)
        outs.append(ov[0:MLA_V] / ov[MLA_V:MLA_V + 1])
        if h == MLA_HEADS - 1:
            o_ref[c * MLA_CQ:(c + 1) * MLA_CQ, :] = jnp.concatenate(outs, axis=0).T.astype(BF16)
            outs = []


def _mla_attention(qm3, km3, vt3):
    bsz, seq, _ = qm3.shape
    return pl.pallas_call(
        _mla_kernel,
        grid=(bsz, seq // MLA_TQ),
        in_specs=[pl.BlockSpec((None, MLA_TQ, MLA_QK_W), lambda b, i: (b, i, 0)),
                  pl.BlockSpec((None, seq, MLA_QK_W), lambda b, i: (b, 0, 0)),
                  pl.BlockSpec((None, MLA_HEADS * MLA_VT_ROWS, seq), lambda b, i: (b, 0, 0))],
        out_specs=pl.BlockSpec((None, MLA_TQ, D_C), lambda b, i: (b, i, 0)),
        out_shape=jax.ShapeDtypeStruct((bsz, seq, D_C), BF16),
        compiler_params=_params(2),
        name="mla_attn",
    )(qm3, km3, vt3)


def _out_kernel(x_ref, oa_ref, ob_ref, oc_ref, w1_ref, w2_ref, w3_ref, g_ref, gate_ref, o_ref):
    tm = x_ref.shape[0]
    for r0 in range(0, tm, tm // 2):
        rows = slice(r0, r0 + tm // 2)
        y = (_dot(oa_ref[rows, :], w1_ref[...]) + _dot(ob_ref[rows, :], w2_ref[...])
             + _dot(oc_ref[rows, :], w3_ref[...]))
        o_ref[rows, :] = x_ref[rows, :] + gate_ref[...] * _rms(y, g_ref[...])


def _out_proj(x, oa, ob, oc, w1, w2, w3, g_post, mod3, l, bsz, seq):
    n = x.shape[0]
    tm = 1024
    tps = seq // tm
    row = lambda i: (i, 0)
    const = lambda i: (0, 0)
    full = lambda a: pl.BlockSpec(a.shape, const)
    return pl.pallas_call(
        _out_kernel,
        grid=(n // tm,),
        in_specs=[pl.BlockSpec((tm, D_MODEL), row),
                  pl.BlockSpec((tm, D_A), row), pl.BlockSpec((tm, D_B), row), pl.BlockSpec((tm, D_C), row),
                  full(w1), full(w2), full(w3), full(g_post),
                  pl.BlockSpec((None, 1, D_MODEL), lambda i: ((l * bsz + i // tps) * N_MOD + 2, 0, 0))],
        out_specs=pl.BlockSpec((tm, D_MODEL), row),
        out_shape=jax.ShapeDtypeStruct((n, D_MODEL), F32),
        compiler_params=_params(1),
        name="out_proj",
    )(x, oa, ob, oc, w1, w2, w3, g_post, mod3)


FFN_TM = 1024
FFN_CHUNK = 256


def _ffn_kernel(x_ref, xp_ref, xn_ref, g_ref, sh_ref, sc_ref, wa_ref, wg_ref, cw_ref, cb_ref, wd_ref,
                gpost_ref, gate_ref, o_ref, u_ref, *, tiles_per_seq):
    i = pl.program_id(0)
    has_prev = (i % tiles_per_seq) != 0
    has_next = (i % tiles_per_seq) != tiles_per_seq - 1
    g = g_ref[...]
    sc = 1.0 + sc_ref[...]
    sh = sh_ref[...]
    x = x_ref[...]
    pre = lambda t: _rms(t, g) * sc + sh
    h_prev = jnp.where(has_prev, pre(xp_ref[...]), 0.0)
    h_next = jnp.where(has_next, pre(xn_ref[...]), 0.0)
    h_mid = pre(x)
    h = h_mid.astype(BF16)
    h_ext = jnp.concatenate([h_prev, h_mid, h_next], axis=0).astype(BF16)
    tm = x.shape[0]
    ext = tm + 2 * SUBLANES
    mid = slice(SUBLANES, SUBLANES + tm)
    for c0 in range(0, D_FF, FFN_CHUNK):
        cols = slice(c0, c0 + FFN_CHUNK)
        a = _dot(h, wa_ref[:, cols])
        ge = _dot(h_ext, wg_ref[:, cols])
        g_prev = pltpu.roll(ge, 1, axis=0)[mid]
        g_next = pltpu.roll(ge, ext - 1, axis=0)[mid]
        gc = cb_ref[:, cols] + g_prev * cw_ref[0:1, cols]
        gc = gc + ge[mid] * cw_ref[1:2, cols]
        gc = gc + g_next * cw_ref[2:3, cols]
        u_ref[:, cols] = (jax.nn.gelu(gc) * a).astype(BF16)
    for r0 in range(0, tm, tm // 2):
        rows = slice(r0, r0 + tm // 2)
        y = _dot(u_ref[rows, :], wd_ref[...])
        o_ref[rows, :] = x[rows] + gate_ref[...] * _rms(y, gpost_ref[...])


def _ffn(x, mod3, l, bsz, seq, g_pre, wa, wg, cw, cb, wd, g_post):
    n = x.shape[0]
    tm = FFN_TM
    tps = seq // tm
    r8 = tm // SUBLANES
    last8 = n // SUBLANES - 1
    const = lambda i: (0, 0)
    full = lambda a: pl.BlockSpec(a.shape, const)
    weight = lambda a: pl.BlockSpec(a.shape, const, pipeline_mode=pl.Buffered(1))
    mrow = lambda k: (lambda i: ((l * bsz + i // tps) * N_MOD + k, 0, 0))
    return pl.pallas_call(
        functools.partial(_ffn_kernel, tiles_per_seq=tps),
        grid=(n // tm,),
        in_specs=[pl.BlockSpec((tm, D_MODEL), lambda i: (i, 0)),
                  pl.BlockSpec((SUBLANES, D_MODEL), lambda i: (jnp.maximum(i * r8 - 1, 0), 0)),
                  pl.BlockSpec((SUBLANES, D_MODEL), lambda i: (jnp.minimum((i + 1) * r8, last8), 0)),
                  full(g_pre),
                  pl.BlockSpec((None, 1, D_MODEL), mrow(3)),
                  pl.BlockSpec((None, 1, D_MODEL), mrow(4)),
                  weight(wa), weight(wg), full(cw), full(cb), weight(wd), full(g_post),
                  pl.BlockSpec((None, 1, D_MODEL), mrow(5))],
        out_specs=pl.BlockSpec((tm, D_MODEL), lambda i: (i, 0)),
        out_shape=jax.ShapeDtypeStruct((n, D_MODEL), F32),
        scratch_shapes=[pltpu.VMEM((tm, D_FF), BF16)],
        compiler_params=_params(1),
        name="conv_ffn",
    )(x, x, x, g_pre, mod3, mod3, wa, wg, cw, cb, wd, g_post, mod3)


def _rope_tables(seq):
    inv_freq = jnp.asarray(ROPE_THETA ** (-np.arange(0, MLA_ROPE, 2, dtype=np.float32) / MLA_ROPE), F32)
    ang = jnp.arange(seq, dtype=F32)[:, None] * inv_freq[None, :]
    cos, sin = jnp.cos(ang), jnp.sin(ang)
    cos2 = jnp.concatenate([cos, cos], axis=1)
    sin2 = jnp.concatenate([sin, sin], axis=1)
    head_c = jnp.concatenate([cos2, jnp.ones((seq, MLA_NOPE), F32), jnp.zeros((seq, LANES - MLA_NOPE - MLA_ROPE), F32)], axis=1)
    head_s = jnp.concatenate([sin2, jnp.zeros((seq, LANES - MLA_ROPE), F32)], axis=1)
    cq = jnp.concatenate([head_c] * MLA_HEADS, axis=1)
    sq = jnp.concatenate([head_s] * MLA_HEADS, axis=1)
    ck = jnp.concatenate([cos2, jnp.zeros((seq, LANES - MLA_ROPE), F32)], axis=1)
    return cq, sq, ck, head_s


def _layer_weights(w_in, w_uq, w_ukv, w_out, w_up, w_down):
    d = w_in.shape[0]
    o_b = 3 * D_A
    o_c = o_b + 3 * D_B
    o_kr = o_c + MLA_Q_LORA + MLA_KV_LORA
    wa = w_in[:, :o_b].astype(BF16)
    wb = w_in[:, o_b:o_c].astype(BF16)
    w_kr = w_in[:, o_kr:o_kr + MLA_ROPE]
    zpad = jnp.zeros((d, LANES - MLA_ROPE), F32)
    wc = jnp.concatenate([w_in[:, o_c:o_kr], w_kr, zpad], axis=1).astype(BF16)

    hq = MLA_NOPE + MLA_ROPE
    q_cols, k_cols, v_cols = [], [], []
    zq = jnp.zeros((MLA_Q_LORA, LANES - hq), F32)
    for h in range(MLA_HEADS):
        nope = w_uq[:, h * hq:h * hq + MLA_NOPE]
        rope = w_uq[:, h * hq + MLA_NOPE:(h + 1) * hq]
        q_cols += [rope, nope, zq]
        hk = MLA_NOPE + MLA_V
        k_cols += [jnp.zeros((MLA_KV_LORA, MLA_ROPE), F32), w_ukv[:, h * hk:h * hk + MLA_NOPE],
                   jnp.zeros((MLA_KV_LORA, LANES - hq), F32)]
        v_cols += [w_ukv[:, h * hk + MLA_NOPE:(h + 1) * hk], jnp.zeros((MLA_KV_LORA, MLA_VT_ROWS - MLA_V), F32)]
    cat = lambda cols: jnp.concatenate(cols, axis=1).astype(BF16)
    w1 = w_out[:D_A].astype(BF16)
    w2 = w_out[D_A:D_A + D_B].astype(BF16)
    w3 = w_out[D_A + D_B:].astype(BF16)
    w_a = w_up[:, :D_FF].astype(BF16)
    w_g = w_up[:, D_FF:].astype(BF16)
    return (wa, wb, wc, cat(q_cols), cat(k_cols), cat(v_cols).T, w1, w2, w3, w_a, w_g,
            w_down.astype(BF16))


def kernel(x, c, w_ada, b_ada, g_pre_mix, g_post_mix, g_pre_ffn, g_post_ffn, w_in, na_rpb, t5_table,
           mla_g_q, mla_g_kv, w_uq, w_ukv, w_out, w_up, conv_w, conv_b, w_down):
    bsz, seq, d = x.shape
    depth = w_in.shape[0]
    n = bsz * seq
    rows = seq // GRID_W
    assert d == D_MODEL and seq % (DIL_BLK * max(dd for _, dd in DIL_PATTERNS)) == 0 and rows >= NA_KH_MAX

    mod3 = _modulation(c, w_ada, b_ada).reshape(depth * bsz * N_MOD, 1, D_MODEL)
    cq, sq, ck, sk = _rope_tables(seq)
    dil_tbls = _dil_bias_tables(t5_table, seq)
    na_tbls = _na_bias_tables(na_rpb, rows)
    row2 = lambda v: v.reshape(1, -1)

    xf = x.reshape(n, d)
    for l in range(depth):
        (wa, wb, wc, wuq, wuk, wuv, w1, w2, w3, w_a, w_g, w_d) = _layer_weights(
            w_in[l], w_uq[l], w_ukv[l], w_out[l], w_up[l], w_down[l])
        za, zb, qm, km, vt = _in_proj(xf, mod3, l, bsz, seq, row2(g_pre_mix[l]), wa, wb, wc,
                                      row2(mla_g_q[l]), row2(mla_g_kv[l]), wuq, wuk, wuv,
                                      cq, sq, ck, sk)
        o_a = _na_attention(za.reshape(bsz, seq, 3 * D_A), na_tbls[l])
        o_b = _dil_attention(zb.reshape(bsz, seq, 3 * D_B), dil_tbls)
        o_c = _mla_attention(qm.reshape(bsz, seq, MLA_QK_W), km.reshape(bsz, seq, MLA_QK_W), vt)
        xf = _out_proj(xf, o_a.reshape(n, D_A), o_b.reshape(n, D_B), o_c.reshape(n, D_C),
                       w1, w2, w3, row2(g_post_mix[l]), mod3, l, bsz, seq)
        xf = _ffn(xf, mod3, l, bsz, seq, row2(g_pre_ffn[l]), w_a, w_g, conv_w[l], row2(conv_b[l]), w_d,
                  row2(g_post_ffn[l]))
    return xf.reshape(bsz, seq, d)
```
